```python
import jax, jax.numpy as jnp
from jax import lax
import numpy as np

D_MODEL = 1024
BATCH = 4
SEQ = 4096
DEPTH = 1
DEC_BATCH = 8
DEC_SEQ = 2048
PAST_LEN = 128

ATT_HEADS = 8
ATT_KV_HEADS = 2
ATT_HEAD_DIM = 64
ATT_WINDOW = 128
ATT_BLOCK = 128
ROPE_THETA = 500000.0
ROPE_DIM = ATT_HEAD_DIM // 4
DN_HEADS = 4
DN_HEAD_DIM = 128
DN_CONV = 5
DN_CHUNK = 64
X_HEADS = 4
X_HEAD_DIM = 128
MEM_LEN = 256
D_FF = 4 * D_MODEL
EPS = 1e-6

ATT_Q = ATT_HEADS * ATT_HEAD_DIM
ATT_KV = ATT_KV_HEADS * ATT_HEAD_DIM
DN_W = DN_HEADS * DN_HEAD_DIM
MIX_W = ATT_Q + DN_W
X_W = X_HEADS * X_HEAD_DIM
O_AQ = 0
O_AK = O_AQ + ATT_Q
O_AV = O_AK + ATT_KV
O_DQKV = O_AV + ATT_KV
O_DZ = O_DQKV + 3 * DN_W
O_BF = O_DZ + DN_W
O_BB = O_BF + DN_HEADS
O_AF = O_BB + DN_HEADS
O_AB = O_AF + DN_HEADS
IN_W = O_AB + DN_HEADS

kernel_name = "hymba_swa_gdn_sandwich_encoder"

F32 = jnp.float32


def _rms_norm(x, g):
    xf = x.astype(F32)
    y = xf * lax.rsqrt(jnp.mean(xf * xf, -1, keepdims=True) + EPS)
    return (y * g.astype(F32)).astype(x.dtype)


def _l2norm(t):
    return t * lax.rsqrt(jnp.sum(t * t, -1, keepdims=True) + EPS)


def _partial_rope(x, pos):
    half = ROPE_DIM // 2
    inv = ROPE_THETA ** (-(jnp.arange(half, dtype=F32) * 2.0 / ROPE_DIM))
    ang = pos.astype(F32)[:, None] * inv[None, :]
    cos = jnp.cos(ang)[None, :, None, :]
    sin = jnp.sin(ang)[None, :, None, :]
    xr = x[..., :ROPE_DIM].astype(F32)
    x1, x2 = xr[..., :half], xr[..., half:]
    rot = jnp.concatenate([x1 * cos - x2 * sin, x2 * cos + x1 * sin], -1).astype(x.dtype)
    return jnp.concatenate([rot, x[..., ROPE_DIM:]], -1)


def _window_attention(q, k, v, sink):
    B, T = q.shape[0], q.shape[1]
    nb = T // ATT_BLOCK
    G = ATT_HEADS // ATT_KV_HEADS
    qb = q.reshape(B, nb, ATT_BLOCK, ATT_KV_HEADS, G, ATT_HEAD_DIM)

    def band(t):
        tp = jnp.pad(t, ((0, 0), (ATT_BLOCK, ATT_BLOCK), (0, 0), (0, 0)))
        tb = tp.reshape(B, nb + 2, ATT_BLOCK, ATT_KV_HEADS, ATT_HEAD_DIM)
        return jnp.concatenate([tb[:, :-2], tb[:, 1:-1], tb[:, 2:]], axis=2)

    kb, vb = band(k), band(v)
    s = jnp.einsum('bnqhgd,bnkhd->bnhgqk', qb, kb).astype(F32) * (ATT_HEAD_DIM ** -0.5)
    qi = jnp.arange(ATT_BLOCK)[:, None]
    kj = jnp.arange(3 * ATT_BLOCK)[None, :]
    rel = kj - ATT_BLOCK - qi
    kpos = jnp.arange(nb)[:, None, None] * ATT_BLOCK - ATT_BLOCK + kj[None]
    valid = (jnp.abs(rel) <= ATT_WINDOW)[None] & (kpos >= 0) & (kpos < T)
    s = jnp.where(valid[None, :, None, None], s, -1e30)
    sk = sink.astype(F32).reshape(ATT_KV_HEADS, G)[None, None, :, :, None, None]
    m = jnp.maximum(jnp.max(s, -1, keepdims=True), sk)
    p = jnp.exp(s - m)
    p = p / (jnp.sum(p, -1, keepdims=True) + jnp.exp(sk - m))
    o = jnp.einsum('bnhgqk,bnkhd->bnqhgd', p.astype(v.dtype), vb)
    return o.reshape(B, T, ATT_Q)


def _centred_conv_silu(x, w):
    y = lax.conv_general_dilated(
        x, w[:, None, :].astype(x.dtype), window_strides=(1,),
        padding=[(DN_CONV // 2, DN_CONV // 2)],
        dimension_numbers=('NWC', 'WIO', 'NWC'), feature_group_count=x.shape[-1])
    return jax.nn.silu(y)


def _chunked_gated_delta(q, k, v, g, beta):
    B, T, H, Dk = q.shape
    Dv = v.shape[-1]
    C = DN_CHUNK
    N = T // C
    to_c = lambda t: jnp.moveaxis(t.reshape(B, N, C, H, -1), 3, 1)
    q = to_c(q) * (Dk ** -0.5)
    k = to_c(k)
    v = to_c(v)
    g = jnp.moveaxis(g.reshape(B, N, C, H), 3, 1)
    beta = jnp.moveaxis(beta.reshape(B, N, C, H), 3, 1)
    gc = jnp.cumsum(g, -1)
    tri = jnp.tril(jnp.ones((C, C), bool))
    strict = jnp.tril(jnp.ones((C, C), bool), -1)
    diff = gc[..., :, None] - gc[..., None, :]
    decay = jnp.where(tri, jnp.exp(jnp.where(tri, diff, 0.0)), 0.0)
    kb = k * beta[..., None]
    L = jnp.where(strict, jnp.einsum('bhncd,bhnsd->bhncs', kb, k) * decay, 0.0)
    A = L + jnp.eye(C, dtype=F32)
    rhs = jnp.concatenate([v * beta[..., None], kb * jnp.exp(gc)[..., None]], -1)
    sol = lax.linalg.triangular_solve(A, rhs, left_side=True, lower=True, unit_diagonal=True)
    u, w = sol[..., :Dv], sol[..., Dv:]
    attn = jnp.where(tri, jnp.einsum('bhncd,bhnsd->bhncs', q, k) * decay, 0.0)
    qg = q * jnp.exp(gc)[..., None]
    kd = k * jnp.exp(gc[..., -1:] - gc)[..., None]
    glast = jnp.exp(gc[..., -1])

    def step(S, inp):
        u_i, w_i, attn_i, qg_i, kd_i, gl_i = inp
        v_new = u_i - jnp.einsum('bhcd,bhde->bhce', w_i, S)
        o = jnp.einsum('bhcd,bhde->bhce', qg_i, S) + jnp.einsum('bhcs,bhse->bhce', attn_i, v_new)
        S = S * gl_i[..., None, None] + jnp.einsum('bhcd,bhce->bhde', kd_i, v_new)
        return S, o

    xs = tuple(jnp.moveaxis(t, 2, 0) for t in (u, w, attn, qg, kd, glast))
    S0 = jnp.zeros((B, H, Dk, Dv), F32)
    _, o = lax.scan(step, S0, xs)
    return o.transpose(1, 0, 3, 2, 4).reshape(B, T, H, Dv)


def _delta_direction(q, k, v, a, b, A_log, dt_bias, reverse):
    g = -jnp.exp(A_log.astype(F32)) * jax.nn.softplus(a.astype(F32) + dt_bias.astype(F32))
    beta = jax.nn.sigmoid(b.astype(F32))
    if reverse:
        q, k, v, g, beta = (jnp.flip(t, 1) for t in (q, k, v, g, beta))
    o = _chunked_gated_delta(q, k, v, g, beta)
    return jnp.flip(o, 1) if reverse else o


def _deltanet(qkv, z, bf, bb, af, ab, conv_w, A_log_f, A_log_b, dt_f, dt_b, norm_w):
    B, T = qkv.shape[0], qkv.shape[1]
    qkv = _centred_conv_silu(qkv, conv_w).astype(F32)
    q = _l2norm(qkv[..., :DN_W].reshape(B, T, DN_HEADS, DN_HEAD_DIM))
    k = _l2norm(qkv[..., DN_W:2 * DN_W].reshape(B, T, DN_HEADS, DN_HEAD_DIM))
    v = qkv[..., 2 * DN_W:].reshape(B, T, DN_HEADS, DN_HEAD_DIM)
    o = (_delta_direction(q, k, v, af, bf, A_log_f, dt_f, False)
         + _delta_direction(q, k, v, ab, bb, A_log_b, dt_b, True))
    o = o * lax.rsqrt(jnp.mean(o * o, -1, keepdims=True) + EPS) * norm_w.astype(F32)
    o = o * jax.nn.silu(z.astype(F32).reshape(B, T, DN_HEADS, DN_HEAD_DIM))
    return o.reshape(B, T, DN_W).astype(z.dtype)


def _cross_attention(h, mem, wq, wkv, wo, mem_g):
    B, T = h.shape[0], h.shape[1]
    mn = _rms_norm(mem, mem_g)
    q = (h @ wq).reshape(B, T, X_HEADS, X_HEAD_DIM)
    kv = mn @ wkv
    k = kv[..., :X_W].reshape(B, -1, X_HEADS, X_HEAD_DIM)
    v = kv[..., X_W:].reshape(B, -1, X_HEADS, X_HEAD_DIM)
    s = jnp.einsum('bqhd,bkhd->bhqk', q, k).astype(F32) * (X_HEAD_DIM ** -0.5)
    p = jax.nn.softmax(s, -1)
    o = jnp.einsum('bhqk,bkhd->bqhd', p.astype(v.dtype), v).reshape(B, T, X_W)
    return o @ wo


def _layer(x, mem, w_in, attn_sink, conv_w, A_log_f, A_log_b, dt_f, dt_b, dn_norm_w, w_out,
           xa_wq, xa_wkv, xa_wo, mem_g, w1, w2,
           g_pre_mix, g_post_mix, g_pre_xa, g_post_xa, g_pre_mlp, g_post_mlp):
    B, T = x.shape[0], x.shape[1]
    h = _rms_norm(x, g_pre_mix)
    proj = h @ w_in
    pos = jnp.arange(T)
    aq = _partial_rope(proj[..., O_AQ:O_AK].reshape(B, T, ATT_HEADS, ATT_HEAD_DIM), pos)
    ak = _partial_rope(proj[..., O_AK:O_AV].reshape(B, T, ATT_KV_HEADS, ATT_HEAD_DIM), pos)
    av = proj[..., O_AV:O_DQKV].reshape(B, T, ATT_KV_HEADS, ATT_HEAD_DIM)
    att = _window_attention(aq, ak, av, attn_sink)
    dn = _deltanet(proj[..., O_DQKV:O_DZ], proj[..., O_DZ:O_BF],
                   proj[..., O_BF:O_BB], proj[..., O_BB:O_AF],
                   proj[..., O_AF:O_AB], proj[..., O_AB:IN_W],
                   conv_w, A_log_f, A_log_b, dt_f, dt_b, dn_norm_w)
    mix = jnp.concatenate([att, dn], -1) @ w_out
    x = x + _rms_norm(mix, g_post_mix)
    h = _rms_norm(x, g_pre_xa)
    x = x + _rms_norm(_cross_attention(h, mem, xa_wq, xa_wkv, xa_wo, mem_g), g_post_xa)
    h = _rms_norm(x, g_pre_mlp)
    x = x + _rms_norm(jnp.square(jax.nn.relu(h @ w1)) @ w2, g_post_mlp)
    return x


def _trunk(x, mem, w_in, attn_sink, dn_conv_w, dn_A_log_f, dn_A_log_b, dn_dt_bias_f, dn_dt_bias_b,
           dn_norm_w, w_out, xa_wq, xa_wkv, xa_wo, mem_norm_g, mlp_w1, mlp_w2,
           norm_pre_mix, norm_post_mix, norm_pre_xa, norm_post_xa, norm_pre_mlp, norm_post_mlp):
    for l in range(DEPTH):
        x = _layer(x, mem, w_in[l], attn_sink[l], dn_conv_w[l], dn_A_log_f[l], dn_A_log_b[l],
                   dn_dt_bias_f[l], dn_dt_bias_b[l], dn_norm_w[l], w_out[l],
                   xa_wq[l], xa_wkv[l], xa_wo[l], mem_norm_g[l], mlp_w1[l], mlp_w2[l],
                   norm_pre_mix[l], norm_post_mix[l], norm_pre_xa[l], norm_post_xa[l],
                   norm_pre_mlp[l], norm_post_mlp[l])
    return x


def setup_inputs(seed: int = 0) -> dict:
    key = jax.random.key(seed)
    ks = jax.random.split(key, 32)
    nrm = lambda k, shape, fan_in: jax.random.normal(k, shape, F32) * (fan_in ** -0.5)
    gain = lambda k, n: 1.0 + 0.02 * jax.random.normal(k, (DEPTH, n), F32)
    dt = jnp.exp(jax.random.uniform(ks[8], (2, DEPTH, DN_HEADS), F32, np.log(1e-3), np.log(1e-1)))
    dt_bias = dt + jnp.log(-jnp.expm1(-dt))
    A_log = jnp.log(jax.random.uniform(ks[9], (2, DEPTH, DN_HEADS), F32, 1.0, 16.0))
    return {
        "x_prompt": jax.random.normal(ks[0], (BATCH, SEQ, D_MODEL), F32),
        "x_sample": jax.random.normal(ks[1], (DEC_BATCH, DEC_SEQ, D_MODEL), F32),
        "mem_prompt": jax.random.normal(ks[2], (BATCH, MEM_LEN, D_MODEL), F32),
        "mem_sample": jax.random.normal(ks[3], (DEC_BATCH, MEM_LEN, D_MODEL), F32),
        "w_in": nrm(ks[4], (DEPTH, D_MODEL, IN_W), D_MODEL),
        "attn_sink": 0.5 * jax.random.normal(ks[5], (DEPTH, ATT_HEADS), F32),
        "dn_conv_w": nrm(ks[6], (DEPTH, DN_CONV, 3 * DN_W), DN_CONV),
        "dn_A_log_f": A_log[0],
        "dn_A_log_b": A_log[1],
        "dn_dt_bias_f": dt_bias[0],
        "dn_dt_bias_b": dt_bias[1],
        "dn_norm_w": gain(ks[10], DN_HEAD_DIM),
        "w_out": nrm(ks[11], (DEPTH, MIX_W, D_MODEL), MIX_W),
        "xa_wq": nrm(ks[12], (DEPTH, D_MODEL, X_W), D_MODEL),
        "xa_wkv": nrm(ks[13], (DEPTH, D_MODEL, 2 * X_W), D_MODEL),
        "xa_wo": nrm(ks[14], (DEPTH, X_W, D_MODEL), X_W),
        "mem_norm_g": gain(ks[15], D_MODEL),
        "mlp_w1": nrm(ks[16], (DEPTH, D_MODEL, D_FF), D_MODEL),
        "mlp_w2": nrm(ks[17], (DEPTH, D_FF, D_MODEL), D_FF),
        "norm_pre_mix": gain(ks[18], D_MODEL),
        "norm_post_mix": gain(ks[19], D_MODEL),
        "norm_pre_xa": gain(ks[20], D_MODEL),
        "norm_post_xa": gain(ks[21], D_MODEL),
        "norm_pre_mlp": gain(ks[22], D_MODEL),
        "norm_post_mlp": gain(ks[23], D_MODEL),
    }


def reference(x_prompt, x_sample, mem_prompt, mem_sample, w_in, attn_sink, dn_conv_w,
              dn_A_log_f, dn_A_log_b, dn_dt_bias_f, dn_dt_bias_b, dn_norm_w, w_out,
              xa_wq, xa_wkv, xa_wo, mem_norm_g, mlp_w1, mlp_w2,
              norm_pre_mix, norm_post_mix, norm_pre_xa, norm_post_xa, norm_pre_mlp, norm_post_mlp):
    y_prompt = _trunk(x_prompt, mem_prompt, w_in, attn_sink, dn_conv_w, dn_A_log_f, dn_A_log_b,
                      dn_dt_bias_f, dn_dt_bias_b, dn_norm_w, w_out, xa_wq, xa_wkv, xa_wo,
                      mem_norm_g, mlp_w1, mlp_w2, norm_pre_mix, norm_post_mix, norm_pre_xa,
                      norm_post_xa, norm_pre_mlp, norm_post_mlp)
    y_sample = _trunk(x_sample, mem_sample, w_in, attn_sink, dn_conv_w, dn_A_log_f, dn_A_log_b,
                      dn_dt_bias_f, dn_dt_bias_b, dn_norm_w, w_out, xa_wq, xa_wkv, xa_wo,
                      mem_norm_g, mlp_w1, mlp_w2, norm_pre_mix, norm_post_mix, norm_pre_xa,
                      norm_post_xa, norm_pre_mlp, norm_post_mlp)
    return (y_prompt, y_sample)
```

```python
import functools

import jax
import jax.numpy as jnp
from jax import lax
from jax.experimental import pallas as pl
from jax.experimental.pallas import tpu as pltpu

F32 = jnp.float32
BF16 = jnp.bfloat16

D_MODEL = 1024
ATT_HEADS = 8
ATT_KV_HEADS = 2
ATT_HEAD_DIM = 64
ATT_WINDOW = 128
ROPE_THETA = 500000.0
ROPE_DIM = ATT_HEAD_DIM // 4
DN_HEADS = 4
DN_HEAD_DIM = 128
DN_CONV = 5
X_HEADS = 4
X_HEAD_DIM = 128
D_FF = 4 * D_MODEL
EPS = 1e-6

ATT_Q = ATT_HEADS * ATT_HEAD_DIM
ATT_KV = ATT_KV_HEADS * ATT_HEAD_DIM
DN_W = DN_HEADS * DN_HEAD_DIM
X_W = X_HEADS * X_HEAD_DIM
O_AK = ATT_Q
O_AV = O_AK + ATT_KV
O_DQKV = O_AV + ATT_KV
O_DZ = O_DQKV + 3 * DN_W
O_GATES = O_DZ + DN_W
N_GATES = 4 * DN_HEADS

LANES = 128
TM = 512
TQ = 128
CHUNK = 128
FF_BLOCK = 1024
VMEM_LIMIT = 56 * 1024 * 1024

NT_DIMS = (((1,), (1,)), ((), ()))
TN_DIMS = (((0,), (0,)), ((), ()))


def _mm(a, b):
    return jnp.dot(a.astype(BF16), b.astype(BF16), preferred_element_type=F32)


def _mm_nt(a, b):
    return lax.dot_general(a.astype(BF16), b.astype(BF16), NT_DIMS, preferred_element_type=F32)


def _mm_tn(a, b):
    return lax.dot_general(a.astype(BF16), b.astype(BF16), TN_DIMS, preferred_element_type=F32)


def _rms(x, g):
    return x * lax.rsqrt(jnp.mean(x * x, axis=-1, keepdims=True) + EPS) * g


def _softplus(x):
    return jnp.maximum(x, 0.0) + jnp.log1p(jnp.exp(-jnp.abs(x)))


def _silu(x):
    return x * jax.nn.sigmoid(x)


def _iota(shape, dim):
    return lax.broadcasted_iota(jnp.int32, shape, dim)


def _params(*sem):
    return pltpu.CompilerParams(dimension_semantics=sem, vmem_limit_bytes=VMEM_LIMIT)


def _in_proj_kernel(x_ref, g_ref, wq_ref, wkv_ref, wd_ref, wz_ref, wgc_ref, wgr_ref,
                    cos_ref, sa_ref, sb_ref, pc_ref, pra_ref, prd_ref,
                    aq_ref, akv_ref, dqkv_ref, dz_ref, gcol_ref, grow_ref):
    hb = _rms(x_ref[...], g_ref[...]).astype(BF16)
    cos = cos_ref[...]
    sa = sa_ref[...]
    sb = sb_ref[...]

    def rope(t):
        return t * cos + pltpu.roll(t, LANES - ROPE_DIM // 2, 1) * sa + pltpu.roll(t, ROPE_DIM // 2, 1) * sb

    q = jnp.dot(hb, wq_ref[...], preferred_element_type=F32)
    for j in range(ATT_Q // LANES):
        sl = slice(j * LANES, (j + 1) * LANES)
        aq_ref[:, sl] = (rope(q[:, sl]) * (ATT_HEAD_DIM ** -0.5)).astype(BF16)
    kv = jnp.dot(hb, wkv_ref[...], preferred_element_type=F32)
    akv_ref[:, :ATT_KV] = rope(kv[:, :ATT_KV]).astype(BF16)
    akv_ref[:, ATT_KV:] = kv[:, ATT_KV:].astype(BF16)
    dqkv_ref[...] = jnp.dot(hb, wd_ref[...], preferred_element_type=F32)
    dz_ref[...] = jnp.dot(hb, wz_ref[...], preferred_element_type=F32)

    raw = jnp.dot(hb, wgc_ref[...], preferred_element_type=F32)
    lane = _iota(raw.shape, 1)
    gval = -jnp.exp(pc_ref[0:1, :]) * _softplus(raw + pc_ref[1:2, :])
    gates = jnp.where(lane < 2 * DN_HEADS, jax.nn.sigmoid(raw), gval)
    pos = _iota(raw.shape, 0) & (CHUNK - 1)
    cf = gates
    cb = gates
    s = 1
    while s < CHUNK:
        cf = cf + jnp.where(pos >= s, pltpu.roll(cf, s, 0), 0.0)
        cb = cb + jnp.where(pos < CHUNK - s, pltpu.roll(cb, TM - s, 0), 0.0)
        s *= 2
    gcol_ref[...] = jnp.where(lane < 2 * DN_HEADS, gates, jnp.where(lane < 3 * DN_HEADS, cf, cb))

    rawr = lax.dot_general(wgr_ref[...], hb, NT_DIMS, preferred_element_type=F32)
    rowi = _iota(rawr.shape, 0)
    gvalr = -jnp.exp(pra_ref[...]) * _softplus(rawr + prd_ref[...])
    gatesr = jnp.where(rowi < 2 * DN_HEADS, jax.nn.sigmoid(rawr), gvalr)
    posr = _iota(rawr.shape, 1) & (CHUNK - 1)
    rf = gatesr
    rb = gatesr
    s = 1
    while s < CHUNK:
        rf = rf + jnp.where(posr >= s, pltpu.roll(rf, s, 1), 0.0)
        rb = rb + jnp.where(posr < CHUNK - s, pltpu.roll(rb, TM - s, 1), 0.0)
        s *= 2
    grow_ref[...] = jnp.where(rowi < 2 * DN_HEADS, gatesr, jnp.where(rowi < 3 * DN_HEADS, rf, rb))


def _in_proj(x2d, T, g, wq, wkv, wd, wz, wgc, wgr, cos, sa, sb, pc, pra, prd):
    BT = x2d.shape[0]
    n_t = T // TM
    row = lambda i: (i, 0)
    const = lambda i: (0, 0)
    tab = lambda i: (i % n_t, 0)
    full = lambda a: pl.BlockSpec(a.shape, const)
    return pl.pallas_call(
        _in_proj_kernel,
        grid=(BT // TM,),
        in_specs=[pl.BlockSpec((TM, D_MODEL), row), full(g), full(wq), full(wkv), full(wd), full(wz),
                  full(wgc), full(wgr),
                  pl.BlockSpec((TM, LANES), tab), pl.BlockSpec((TM, LANES), tab), pl.BlockSpec((TM, LANES), tab),
                  full(pc), full(pra), full(prd)],
        out_specs=[pl.BlockSpec((TM, ATT_Q), row), pl.BlockSpec((TM, 2 * ATT_KV), row),
                   pl.BlockSpec((TM, 3 * DN_W), row), pl.BlockSpec((TM, DN_W), row),
                   pl.BlockSpec((TM, LANES), row), pl.BlockSpec((N_GATES, TM), lambda i: (0, i))],
        out_shape=[jax.ShapeDtypeStruct((BT, ATT_Q), BF16), jax.ShapeDtypeStruct((BT, 2 * ATT_KV), BF16),
                   jax.ShapeDtypeStruct((BT, 3 * DN_W), F32), jax.ShapeDtypeStruct((BT, DN_W), F32),
                   jax.ShapeDtypeStruct((BT, LANES), F32), jax.ShapeDtypeStruct((N_GATES, BT), F32)],
        compiler_params=_params("parallel"),
        name="in_proj",
    )(x2d, g, wq, wkv, wd, wz, wgc, wgr, cos, sa, sb, pc, pra, prd)


def _win_attn_kernel(sink_ref, q_ref, kv_ref, o_ref, *, T):
    W = TQ + 2 * ATT_WINDOW
    t0 = pl.program_id(1) * TQ
    start = pl.multiple_of(jnp.clip(t0 - ATT_WINDOW, 0, T - W), LANES)
    kvw = kv_ref[pl.ds(start, W), :]
    k = kvw[:, :ATT_KV]
    v = kvw[:, ATT_KV:]
    klane = _iota(k.shape, 1)
    zero = jnp.zeros_like(k)
    k_heads = (jnp.where(klane < ATT_HEAD_DIM, k, zero), jnp.where(klane >= ATT_HEAD_DIM, k, zero))
    qpos = t0 + _iota((TQ, W), 0)
    kpos = start + _iota((TQ, W), 1)
    valid = jnp.abs(qpos - kpos) <= ATT_WINDOW
    olane = _iota((TQ, LANES), 1)
    for j in range(ATT_Q // LANES):
        sl = slice(j * LANES, (j + 1) * LANES)
        qg = q_ref[:, sl]
        outs = []
        for kvh in range(ATT_KV_HEADS):
            s = lax.dot_general(qg, k_heads[kvh], NT_DIMS, preferred_element_type=F32)
            s = jnp.where(valid, s, -1e30)
            sk = sink_ref[kvh * (ATT_HEADS // ATT_KV_HEADS) + j]
            m = jnp.maximum(jnp.max(s, axis=-1, keepdims=True), sk)
            p = jnp.exp(s - m)
            den = jnp.sum(p, axis=-1, keepdims=True) + jnp.exp(sk - m)
            outs.append(jnp.dot(p.astype(BF16), v, preferred_element_type=F32) / den)
        o_ref[:, sl] = jnp.where(olane < ATT_HEAD_DIM, outs[0], outs[1]).astype(BF16)


def _win_attn(aq, akv, sink, B, T):
    n_q = T // TQ
    return pl.pallas_call(
        functools.partial(_win_attn_kernel, T=T),
        grid=(B, n_q),
        in_specs=[pl.BlockSpec(memory_space=pltpu.SMEM),
                  pl.BlockSpec((TQ, ATT_Q), lambda b, i: (b * n_q + i, 0)),
                  pl.BlockSpec((T, 2 * ATT_KV), lambda b, i: (b, 0))],
        out_specs=pl.BlockSpec((TQ, ATT_Q), lambda b, i: (b * n_q + i, 0)),
        out_shape=jax.ShapeDtypeStruct((B * T, ATT_Q), BF16),
        compiler_params=_params("parallel", "arbitrary"),
        name="win_attn",
    )(sink, aq, akv)


def _pair_blockdiag(y):
    z = jnp.zeros((CHUNK, CHUNK), y.dtype)
    top = jnp.concatenate([y[:, :CHUNK], z], axis=1)
    bot = jnp.concatenate([z, y[:, CHUNK:]], axis=1)
    return jnp.concatenate([top, bot], axis=0)


def _mm_pair(x, y):
    return jnp.dot(x.astype(BF16), _pair_blockdiag(y.astype(BF16)), preferred_element_type=F32)


def _unit_tri_inverse_pair(l2):
    r = _iota((CHUNK, 2 * CHUNK), 0)
    c = _iota((CHUNK, 2 * CHUNK), 1) & (CHUNK - 1)
    base = 16
    d = jnp.where((r // base) == (c // base), l2, 0.0)
    d2 = _mm_pair(d, d)
    d4 = _mm_pair(d2, d2)
    d8 = _mm_pair(d4, d4)
    p = jnp.where(r == c, 1.0, 0.0) - d
    p = p + _mm_pair(p, d2)
    p = p + _mm_pair(p, d4)
    p = p + _mm_pair(p, d8)
    s = base
    while s < CHUNK:
        m = jnp.where(((r // (2 * s)) == (c // (2 * s))) & ((r // s) != (c // s)), l2, 0.0)
        p = p - _mm_pair(p, _mm_pair(m, p))
        s *= 2
    return p


def _deltanet_kernel(q_ref, k_ref, v_ref, z_ref, gcol_ref, grow_ref, cwq_ref, cwk_ref, cwv_ref, nw_ref,
                     o_ref, u_scr, w_scr, qg_scr, kd_scr, at_scr, gl_scr, os_scr, *, T):
    C = CHUNK
    N = T // C
    h = pl.program_id(1)
    ri = _iota((C, C), 0)
    ci = _iota((C, C), 1)
    tril = ri >= ci
    triu = ri <= ci
    lane = _iota((C, LANES), 1)

    def conv_silu(ref, cw_ref, n, c0):
        prev = ref[pl.ds(pl.multiple_of(jnp.maximum(c0 - 8, 0), 8), 8), :] * jnp.where(n > 0, 1.0, 0.0)
        nxt = ref[pl.ds(pl.multiple_of(jnp.minimum(c0 + C, T - 8), 8), 8), :] * jnp.where(n < N - 1, 1.0, 0.0)
        ext = jnp.concatenate([prev, ref[pl.ds(c0, C), :], nxt], axis=0)
        cw = cw_ref[...]
        off = 8 - DN_CONV // 2
        acc = ext[off:off + C] * cw[0:1]
        for j in range(1, DN_CONV):
            acc = acc + ext[off + j:off + j + C] * cw[j:j + 1]
        return _silu(acc)

    def l2norm(t):
        return t * lax.rsqrt(jnp.sum(t * t, axis=-1, keepdims=True) + EPS)

    def prepare(n, carry):
        c0 = pl.multiple_of(n * C, C)
        q = l2norm(conv_silu(q_ref, cwq_ref, n, c0)) * (DN_HEAD_DIM ** -0.5)
        k = l2norm(conv_silu(k_ref, cwk_ref, n, c0))
        v = conv_silu(v_ref, cwv_ref, n, c0)
        gc = gcol_ref[pl.ds(c0, C), :]

        def col(idx):
            return jnp.sum(jnp.where(lane == idx, gc, 0.0), axis=1, keepdims=True)

        beta = (col(h), col(DN_HEADS + h))
        gcc = (col(2 * DN_HEADS + h), col(3 * DN_HEADS + h))
        gr = grow_ref[2 * DN_HEADS:, pl.ds(c0, C)]
        sub = _iota(gr.shape, 0)

        def row(idx):
            return jnp.sum(jnp.where(sub == idx, gr, 0.0), axis=0, keepdims=True)

        gcr = (row(h), row(DN_HEADS + h))
        glast = (gcr[0][:, C - 1:C], gcr[1][:, 0:1])
        incl = (tril, triu)
        gram = _mm_nt(jnp.concatenate([q, k], axis=0), k)
        qk = gram[:C]
        kk = gram[C:]
        decay = [jnp.where(incl[d], jnp.exp(jnp.where(incl[d], gcc[d] - gcr[d], 0.0)), 0.0) for d in range(2)]
        strict = (ri > ci, ri < ci)
        l2 = jnp.concatenate([jnp.where(strict[d], beta[d] * kk * decay[d], 0.0) for d in range(2)], axis=1)
        ainv = _unit_tri_inverse_pair(l2)
        for d in range(2):
            eg = jnp.exp(gcc[d])
            rhs = jnp.concatenate([v * beta[d], (k * beta[d]) * eg], axis=1)
            uw = _mm(ainv[:, d * C:(d + 1) * C], rhs)
            u_scr[d, pl.ds(c0, C), :] = uw[:, :DN_HEAD_DIM]
            w_scr[d, pl.ds(c0, C), :] = uw[:, DN_HEAD_DIM:].astype(BF16)
            qg_scr[d, pl.ds(c0, C), :] = (q * eg).astype(BF16)
            kd_scr[d, pl.ds(c0, C), :] = (k * jnp.exp(glast[d] - gcc[d])).astype(BF16)
            at_scr[d, pl.ds(c0, C), :] = (qk * decay[d]).astype(BF16)
            gl_scr[d, n] = jnp.broadcast_to(jnp.exp(glast[d]), (8, LANES))
        return carry

    lax.fori_loop(0, N, prepare, 0)

    def scan_step(d, n, state):
        rows = pl.ds(pl.multiple_of(n * C, C), C)
        sb = state.astype(BF16)
        ws_qs = jnp.dot(jnp.concatenate([w_scr[d, rows, :], qg_scr[d, rows, :]], axis=0), sb,
                        preferred_element_type=F32)
        v_new = (u_scr[d, rows, :] - ws_qs[:C]).astype(BF16)
        os_scr[d, rows, :] = ws_qs[C:] + jnp.dot(at_scr[d, rows, :], v_new, preferred_element_type=F32)
        return state * gl_scr[d, n][0:1, :] + lax.dot_general(kd_scr[d, rows, :], v_new, TN_DIMS,
                                                               preferred_element_type=F32)

    def scan(i, states):
        return scan_step(0, i, states[0]), scan_step(1, N - 1 - i, states[1])

    s0 = jnp.zeros((DN_HEAD_DIM, DN_HEAD_DIM), F32)
    lax.fori_loop(0, N, scan, (s0, s0))

    def finish(n, carry):
        rows = pl.ds(pl.multiple_of(n * C, C), C)
        o = os_scr[0, rows, :] + os_scr[1, rows, :]
        o = o * lax.rsqrt(jnp.mean(o * o, axis=-1, keepdims=True) + EPS) * nw_ref[...]
        o_ref[rows, :] = (o * _silu(z_ref[rows, :])).astype(BF16)
        return carry

    lax.fori_loop(0, N, finish, 0)


def _deltanet(dqkv, dz, gcol, grow, conv_w, norm_w, B, T):
    H = DN_HEADS
    N = T // CHUNK
    seq = lambda off: pl.BlockSpec((T, DN_HEAD_DIM), lambda b, h: (b, off + h))
    cw = lambda off: pl.BlockSpec((DN_CONV, DN_HEAD_DIM), lambda b, h: (0, off + h))
    return pl.pallas_call(
        functools.partial(_deltanet_kernel, T=T),
        grid=(B, H),
        in_specs=[seq(0), seq(H), seq(2 * H), seq(0),
                  pl.BlockSpec((T, LANES), lambda b, h: (b, 0)),
                  pl.BlockSpec((N_GATES, T), lambda b, h: (0, b)),
                  cw(0), cw(H), cw(2 * H),
                  pl.BlockSpec((1, DN_HEAD_DIM), lambda b, h: (0, 0))],
        out_specs=seq(0),
        out_shape=jax.ShapeDtypeStruct((B * T, DN_W), BF16),
        scratch_shapes=[pltpu.VMEM((2, T, DN_HEAD_DIM), F32),
                        pltpu.VMEM((2, T, DN_HEAD_DIM), BF16),
                        pltpu.VMEM((2, T, DN_HEAD_DIM), BF16),
                        pltpu.VMEM((2, T, DN_HEAD_DIM), BF16),
                        pltpu.VMEM((2, T, CHUNK), BF16),
                        pltpu.VMEM((2, N, 8, LANES), F32),
                        pltpu.VMEM((2, T, DN_HEAD_DIM), F32)],
        compiler_params=_params("parallel", "arbitrary"),
        name="deltanet",
    )(dqkv, dqkv, dqkv, dz, gcol, grow, conv_w, conv_w, conv_w, norm_w)


def _out_proj_kernel(att_ref, dn_ref, x_ref, woa_ref, wod_ref, g1_ref, g2_ref, wq_ref, x1_ref, qx_ref):
    mix = (jnp.dot(att_ref[...], woa_ref[...], preferred_element_type=F32)
           + jnp.dot(dn_ref[...], wod_ref[...], preferred_element_type=F32))
    x1 = x_ref[...] + _rms(mix, g1_ref[...])
    x1_ref[...] = x1
    qx_ref[...] = _mm(_rms(x1, g2_ref[...]), wq_ref[...]).astype(BF16)


def _out_proj(att, dn, x2d, woa, wod, g1, g2, wq):
    BT = x2d.shape[0]
    row = lambda i: (i, 0)
    full = lambda a: pl.BlockSpec(a.shape, lambda i: (0, 0))
    return pl.pallas_call(
        _out_proj_kernel,
        grid=(BT // TM,),
        in_specs=[pl.BlockSpec((TM, ATT_Q), row), pl.BlockSpec((TM, DN_W), row), pl.BlockSpec((TM, D_MODEL), row),
                  full(woa), full(wod), full(g1), full(g2), full(wq)],
        out_specs=[pl.BlockSpec((TM, D_MODEL), row), pl.BlockSpec((TM, X_W), row)],
        out_shape=[jax.ShapeDtypeStruct((BT, D_MODEL), F32), jax.ShapeDtypeStruct((BT, X_W), BF16)],
        compiler_params=_params("parallel"),
        name="out_proj",
    )(att, dn, x2d, woa, wod, g1, g2, wq)


def _mem_kv_kernel(mem_ref, g_ref, w_ref, kv_ref):
    kv_ref[...] = _mm(_rms(mem_ref[...], g_ref[...]), w_ref[...]).astype(BF16)


def _mem_kv(mem2d, g, wkv):
    M = mem2d.shape[0]
    mem_len = 256
    return pl.pallas_call(
        _mem_kv_kernel,
        grid=(M // mem_len,),
        in_specs=[pl.BlockSpec((mem_len, D_MODEL), lambda i: (i, 0)),
                  pl.BlockSpec(g.shape, lambda i: (0, 0)), pl.BlockSpec(wkv.shape, lambda i: (0, 0))],
        out_specs=pl.BlockSpec((mem_len, 2 * X_W), lambda i: (i, 0)),
        out_shape=jax.ShapeDtypeStruct((M, 2 * X_W), BF16),
        compiler_params=_params("parallel"),
        name="mem_kv",
    )(mem2d, g, wkv)


def _xattn_kernel(qx_ref, kv_ref, x1_ref, wo_ref, g_ref, x2_ref):
    outs = []
    for hd in range(X_HEADS):
        sl = slice(hd * X_HEAD_DIM, (hd + 1) * X_HEAD_DIM)
        s = lax.dot_general(qx_ref[:, sl], kv_ref[:, sl], NT_DIMS, preferred_element_type=F32)
        s = s * (X_HEAD_DIM ** -0.5)
        p = jnp.exp(s - jnp.max(s, axis=-1, keepdims=True))
        den = jnp.sum(p, axis=-1, keepdims=True)
        vsl = slice(X_W + hd * X_HEAD_DIM, X_W + (hd + 1) * X_HEAD_DIM)
        outs.append(jnp.dot(p.astype(BF16), kv_ref[:, vsl], preferred_element_type=F32) / den)
    o = jnp.concatenate(outs, axis=1)
    x2_ref[...] = x1_ref[...] + _rms(_mm(o, wo_ref[...]), g_ref[...])


def _xattn(qx, kv, x1, wo, g, B, T):
    n_t = T // TM
    mem_len = kv.shape[0] // B
    row = lambda b, i: (b * n_t + i, 0)
    return pl.pallas_call(
        _xattn_kernel,
        grid=(B, n_t),
        in_specs=[pl.BlockSpec((TM, X_W), row), pl.BlockSpec((mem_len, 2 * X_W), lambda b, i: (b, 0)),
                  pl.BlockSpec((TM, D_MODEL), row),
                  pl.BlockSpec(wo.shape, lambda b, i: (0, 0)), pl.BlockSpec(g.shape, lambda b, i: (0, 0))],
        out_specs=pl.BlockSpec((TM, D_MODEL), row),
        out_shape=jax.ShapeDtypeStruct((B * T, D_MODEL), F32),
        compiler_params=_params("parallel", "arbitrary"),
        name="xattn",
    )(qx, kv, x1, wo, g)


def _mlp_kernel(x_ref, g1_ref, w1_ref, w2_ref, g2_ref, y_ref):
    x = x_ref[...]
    hb = _rms(x, g1_ref[...]).astype(BF16)
    acc = jnp.zeros((TM, D_MODEL), F32)
    for c in range(D_FF // FF_BLOCK):
        sl = slice(c * FF_BLOCK, (c + 1) * FF_BLOCK)
        a = jnp.maximum(jnp.dot(hb, w1_ref[:, sl], preferred_element_type=F32), 0.0)
        acc = acc + jnp.dot((a * a).astype(BF16), w2_ref[sl, :], preferred_element_type=F32)
    y_ref[...] = x + _rms(acc, g2_ref[...])


def _mlp(x2, g1, w1, w2, g2):
    BT = x2.shape[0]
    row = lambda i: (i, 0)
    full = lambda a: pl.BlockSpec(a.shape, lambda i: (0, 0))
    return pl.pallas_call(
        _mlp_kernel,
        grid=(BT // TM,),
        in_specs=[pl.BlockSpec((TM, D_MODEL), row), full(g1), full(w1), full(w2), full(g2)],
        out_specs=pl.BlockSpec((TM, D_MODEL), row),
        out_shape=jax.ShapeDtypeStruct((BT, D_MODEL), F32),
        compiler_params=_params("parallel"),
        name="mlp",
    )(x2, g1, w1, w2, g2)


def _rope_tables(T):
    half = ROPE_DIM // 2
    inv = ROPE_THETA ** (-(jnp.arange(half, dtype=F32) * 2.0 / ROPE_DIM))
    ang = jnp.arange(T).astype(F32)[:, None] * inv[None, :]
    cos, sin = jnp.cos(ang), jnp.sin(ang)
    pad = ATT_HEAD_DIM - ROPE_DIM
    one_head = lambda a, b, fill: jnp.concatenate([a, b, jnp.full((T, pad), fill, F32)], axis=1)
    zeros = jnp.zeros((T, half), F32)
    two_heads = lambda t: jnp.concatenate([t, t], axis=1)
    return (two_heads(one_head(cos, cos, 1.0)), two_heads(one_head(-sin, zeros, 0.0)),
            two_heads(one_head(zeros, sin, 0.0)))


def _gate_params(a_log_f, a_log_b, dt_f, dt_b):
    zeros = jnp.zeros((2 * DN_HEADS,), F32)
    a_vec = jnp.concatenate([zeros, a_log_f.astype(F32), a_log_b.astype(F32)])
    d_vec = jnp.concatenate([zeros, dt_f.astype(F32), dt_b.astype(F32)])
    pad = jnp.zeros((LANES - N_GATES,), F32)
    pc = jnp.stack([jnp.concatenate([a_vec, pad]), jnp.concatenate([d_vec, pad])])
    pra = jnp.broadcast_to(a_vec[:, None], (N_GATES, TM))
    prd = jnp.broadcast_to(d_vec[:, None], (N_GATES, TM))
    return pc, pra, prd


def _layer(x, mem, w_in, attn_sink, conv_w, a_log_f, a_log_b, dt_f, dt_b, dn_norm_w, w_out,
           xa_wq, xa_wkv, xa_wo, mem_g, w1, w2, g_pre_mix, g_post_mix, g_pre_xa, g_post_xa,
           g_pre_mlp, g_post_mlp):
    B, T, _ = x.shape
    assert T % TM == 0 and T >= TQ + 2 * ATT_WINDOW
    x2d = x.reshape(B * T, D_MODEL)
    vec = lambda g: g.astype(F32).reshape(1, -1)
    hpg = ATT_HEADS // ATT_KV_HEADS
    head_order = [kvh * hpg + j for j in range(hpg) for kvh in range(ATT_KV_HEADS)]
    q_cols = jnp.concatenate([jnp.arange(hd * ATT_HEAD_DIM, (hd + 1) * ATT_HEAD_DIM) for hd in head_order])
    wq = w_in[:, q_cols].astype(BF16)
    wkv = w_in[:, O_AK:O_DQKV].astype(BF16)
    wd = w_in[:, O_DQKV:O_DZ].astype(BF16)
    wz = w_in[:, O_DZ:O_GATES].astype(BF16)
    wg = w_in[:, O_GATES:O_GATES + N_GATES]
    wgc = jnp.pad(wg, ((0, 0), (0, LANES - N_GATES))).astype(BF16)
    wgr = wg.T.astype(BF16)
    cos, sa, sb = _rope_tables(T)
    pc, pra, prd = _gate_params(a_log_f, a_log_b, dt_f, dt_b)

    aq, akv, dqkv, dz, gcol, grow = _in_proj(x2d, T, vec(g_pre_mix), wq, wkv, wd, wz, wgc, wgr,
                                             cos, sa, sb, pc, pra, prd)
    att = _win_attn(aq, akv, attn_sink.astype(F32), B, T)
    dn = _deltanet(dqkv, dz, gcol, grow, conv_w.astype(F32), vec(dn_norm_w), B, T)
    x1, qx = _out_proj(att, dn, x2d, w_out[q_cols].astype(BF16), w_out[ATT_Q:].astype(BF16),
                       vec(g_post_mix), vec(g_pre_xa), xa_wq.astype(BF16))
    kv = _mem_kv(mem.reshape(-1, D_MODEL), vec(mem_g), xa_wkv.astype(BF16))
    x2 = _xattn(qx, kv, x1, xa_wo.astype(BF16), vec(g_post_xa), B, T)
    y = _mlp(x2, vec(g_pre_mlp), w1.astype(BF16), w2.astype(BF16), vec(g_post_mlp))
    return y.reshape(B, T, D_MODEL)


def kernel(x_prompt, x_sample, mem_prompt, mem_sample, w_in, attn_sink, dn_conv_w, dn_A_log_f, dn_A_log_b,
           dn_dt_bias_f, dn_dt_bias_b, dn_norm_w, w_out, xa_wq, xa_wkv, xa_wo, mem_norm_g, mlp_w1, mlp_w2,
           norm_pre_mix, norm_post_mix, norm_pre_xa, norm_post_xa, norm_pre_mlp, norm_post_mlp):
    weights = (w_in, attn_sink, dn_conv_w, dn_A_log_f, dn_A_log_b, dn_dt_bias_f, dn_dt_bias_b, dn_norm_w,
               w_out, xa_wq, xa_wkv, xa_wo, mem_norm_g, mlp_w1, mlp_w2, norm_pre_mix, norm_post_mix,
               norm_pre_xa, norm_post_xa, norm_pre_mlp, norm_post_mlp)
    outs = []
    for x, mem in ((x_prompt, mem_prompt), (x_sample, mem_sample)):
        for l in range(w_in.shape[0]):
            x = _layer(x, mem, *(w[l] for w in weights))
        outs.append(x)
    return tuple(outs)
```

```python
import functools

import jax
import jax.numpy as jnp
from jax import lax
from jax.experimental import pallas as pl
from jax.experimental.pallas import tpu as pltpu

F32 = jnp.float32
BF16 = jnp.bfloat16

D_MODEL = 1024
ATT_HEADS = 8
ATT_KV_HEADS = 2
ATT_HEAD_DIM = 64
ATT_WINDOW = 128
ROPE_THETA = 500000.0
ROPE_DIM = ATT_HEAD_DIM // 4
DN_HEADS = 4
DN_HEAD_DIM = 128
DN_CONV = 5
X_HEADS = 4
X_HEAD_DIM = 128
D_FF = 4 * D_MODEL
EPS = 1e-6

ATT_Q = ATT_HEADS * ATT_HEAD_DIM
ATT_KV = ATT_KV_HEADS * ATT_HEAD_DIM
DN_W = DN_HEADS * DN_HEAD_DIM
X_W = X_HEADS * X_HEAD_DIM
O_AK = ATT_Q
O_AV = O_AK + ATT_KV
O_DQKV = O_AV + ATT_KV
O_DZ = O_DQKV + 3 * DN_W
O_GATES = O_DZ + DN_W
N_GATES = 4 * DN_HEADS

LANES = 128
TM = 512
TQ = 128
CHUNK = 128
PREP_UNROLL = 8
FF_BLOCK = 1024
VMEM_LIMIT = 56 * 1024 * 1024

NT_DIMS = (((1,), (1,)), ((), ()))
TN_DIMS = (((0,), (0,)), ((), ()))


def _mm(a, b):
    return jnp.dot(a.astype(BF16), b.astype(BF16), preferred_element_type=F32)


def _mm_nt(a, b):
    return lax.dot_general(a.astype(BF16), b.astype(BF16), NT_DIMS, preferred_element_type=F32)


def _mm_tn(a, b):
    return lax.dot_general(a.astype(BF16), b.astype(BF16), TN_DIMS, preferred_element_type=F32)


def _rms(x, g):
    return x * lax.rsqrt(jnp.mean(x * x, axis=-1, keepdims=True) + EPS) * g


def _softplus(x):
    return jnp.maximum(x, 0.0) + jnp.log1p(jnp.exp(-jnp.abs(x)))


def _silu(x):
    return x * jax.nn.sigmoid(x)


def _iota(shape, dim):
    return lax.broadcasted_iota(jnp.int32, shape, dim)


def _params(*sem):
    return pltpu.CompilerParams(dimension_semantics=sem, vmem_limit_bytes=VMEM_LIMIT)


def _in_proj_kernel(x_ref, g_ref, wq_ref, wkv_ref, wd_ref, wz_ref, wgc_ref, wgr_ref,
                    cos_ref, sa_ref, sb_ref, pc_ref, pra_ref, prd_ref,
                    aq_ref, akv_ref, dqkv_ref, dz_ref, gcol_ref, grow_ref):
    hb = _rms(x_ref[...], g_ref[...]).astype(BF16)
    cos = cos_ref[...]
    sa = sa_ref[...]
    sb = sb_ref[...]

    def rope(t):
        return t * cos + pltpu.roll(t, LANES - ROPE_DIM // 2, 1) * sa + pltpu.roll(t, ROPE_DIM // 2, 1) * sb

    q = jnp.dot(hb, wq_ref[...], preferred_element_type=F32)
    for j in range(ATT_Q // LANES):
        sl = slice(j * LANES, (j + 1) * LANES)
        aq_ref[:, sl] = (rope(q[:, sl]) * (ATT_HEAD_DIM ** -0.5)).astype(BF16)
    kv = jnp.dot(hb, wkv_ref[...], preferred_element_type=F32)
    akv_ref[:, :ATT_KV] = rope(kv[:, :ATT_KV]).astype(BF16)
    akv_ref[:, ATT_KV:] = kv[:, ATT_KV:].astype(BF16)
    dqkv_ref[...] = jnp.dot(hb, wd_ref[...], preferred_element_type=F32)
    dz_ref[...] = jnp.dot(hb, wz_ref[...], preferred_element_type=F32)

    raw = jnp.dot(hb, wgc_ref[...], preferred_element_type=F32)
    lane = _iota(raw.shape, 1)
    gval = -jnp.exp(pc_ref[0:1, :]) * _softplus(raw + pc_ref[1:2, :])
    gates = jnp.where(lane < 2 * DN_HEADS, jax.nn.sigmoid(raw), gval)
    pos = _iota(raw.shape, 0) & (CHUNK - 1)
    cf = gates
    cb = gates
    s = 1
    while s < CHUNK:
        cf = cf + jnp.where(pos >= s, pltpu.roll(cf, s, 0), 0.0)
        cb = cb + jnp.where(pos < CHUNK - s, pltpu.roll(cb, TM - s, 0), 0.0)
        s *= 2
    gcol_ref[...] = jnp.where(lane < 2 * DN_HEADS, gates, jnp.where(lane < 3 * DN_HEADS, cf, cb))

    rawr = lax.dot_general(wgr_ref[...], hb, NT_DIMS, preferred_element_type=F32)
    rowi = _iota(rawr.shape, 0)
    gvalr = -jnp.exp(pra_ref[...]) * _softplus(rawr + prd_ref[...])
    gatesr = jnp.where(rowi < 2 * DN_HEADS, jax.nn.sigmoid(rawr), gvalr)
    posr = _iota(rawr.shape, 1) & (CHUNK - 1)
    rf = gatesr
    rb = gatesr
    s = 1
    while s < CHUNK:
        rf = rf + jnp.where(posr >= s, pltpu.roll(rf, s, 1), 0.0)
        rb = rb + jnp.where(posr < CHUNK - s, pltpu.roll(rb, TM - s, 1), 0.0)
        s *= 2
    grow_ref[...] = jnp.where(rowi < 2 * DN_HEADS, gatesr, jnp.where(rowi < 3 * DN_HEADS, rf, rb))


def _in_proj(x2d, T, g, wq, wkv, wd, wz, wgc, wgr, cos, sa, sb, pc, pra, prd):
    BT = x2d.shape[0]
    n_t = T // TM
    row = lambda i: (i, 0)
    const = lambda i: (0, 0)
    tab = lambda i: (i % n_t, 0)
    full = lambda a: pl.BlockSpec(a.shape, const)
    return pl.pallas_call(
        _in_proj_kernel,
        grid=(BT // TM,),
        in_specs=[pl.BlockSpec((TM, D_MODEL), row), full(g), full(wq), full(wkv), full(wd), full(wz),
                  full(wgc), full(wgr),
                  pl.BlockSpec((TM, LANES), tab), pl.BlockSpec((TM, LANES), tab), pl.BlockSpec((TM, LANES), tab),
                  full(pc), full(pra), full(prd)],
        out_specs=[pl.BlockSpec((TM, ATT_Q), row), pl.BlockSpec((TM, 2 * ATT_KV), row),
                   pl.BlockSpec((TM, 3 * DN_W), row), pl.BlockSpec((TM, DN_W), row),
                   pl.BlockSpec((TM, LANES), row), pl.BlockSpec((N_GATES, TM), lambda i: (0, i))],
        out_shape=[jax.ShapeDtypeStruct((BT, ATT_Q), BF16), jax.ShapeDtypeStruct((BT, 2 * ATT_KV), BF16),
                   jax.ShapeDtypeStruct((BT, 3 * DN_W), F32), jax.ShapeDtypeStruct((BT, DN_W), F32),
                   jax.ShapeDtypeStruct((BT, LANES), F32), jax.ShapeDtypeStruct((N_GATES, BT), F32)],
        compiler_params=_params("parallel"),
        name="in_proj",
    )(x2d, g, wq, wkv, wd, wz, wgc, wgr, cos, sa, sb, pc, pra, prd)


def _win_attn_kernel(sink_ref, q_ref, kv_ref, o_ref, *, T):
    W = TQ + 2 * ATT_WINDOW
    t0 = pl.program_id(1) * TQ
    start = pl.multiple_of(jnp.clip(t0 - ATT_WINDOW, 0, T - W), LANES)
    kvw = kv_ref[pl.ds(start, W), :]
    k = kvw[:, :ATT_KV]
    v = kvw[:, ATT_KV:]
    klane = _iota(k.shape, 1)
    zero = jnp.zeros_like(k)
    k_heads = (jnp.where(klane < ATT_HEAD_DIM, k, zero), jnp.where(klane >= ATT_HEAD_DIM, k, zero))
    qpos = t0 + _iota((TQ, W), 0)
    kpos = start + _iota((TQ, W), 1)
    valid = jnp.abs(qpos - kpos) <= ATT_WINDOW
    olane = _iota((TQ, LANES), 1)
    for j in range(ATT_Q // LANES):
        sl = slice(j * LANES, (j + 1) * LANES)
        qg = q_ref[:, sl]
        outs = []
        for kvh in range(ATT_KV_HEADS):
            s = lax.dot_general(qg, k_heads[kvh], NT_DIMS, preferred_element_type=F32)
            s = jnp.where(valid, s, -1e30)
            sk = sink_ref[kvh * (ATT_HEADS // ATT_KV_HEADS) + j]
            m = jnp.maximum(jnp.max(s, axis=-1, keepdims=True), sk)
            p = jnp.exp(s - m)
            den = jnp.sum(p, axis=-1, keepdims=True) + jnp.exp(sk - m)
            outs.append(jnp.dot(p.astype(BF16), v, preferred_element_type=F32) / den)
        o_ref[:, sl] = jnp.where(olane < ATT_HEAD_DIM, outs[0], outs[1]).astype(BF16)


def _win_attn(aq, akv, sink, B, T):
    n_q = T // TQ
    return pl.pallas_call(
        functools.partial(_win_attn_kernel, T=T),
        grid=(B, n_q),
        in_specs=[pl.BlockSpec(memory_space=pltpu.SMEM),
                  pl.BlockSpec((TQ, ATT_Q), lambda b, i: (b * n_q + i, 0)),
                  pl.BlockSpec((T, 2 * ATT_KV), lambda b, i: (b, 0))],
        out_specs=pl.BlockSpec((TQ, ATT_Q), lambda b, i: (b * n_q + i, 0)),
        out_shape=jax.ShapeDtypeStruct((B * T, ATT_Q), BF16),
        compiler_params=_params("parallel", "arbitrary"),
        name="win_attn",
    )(sink, aq, akv)


def _pair_blockdiag(y):
    z = jnp.zeros((CHUNK, CHUNK), y.dtype)
    top = jnp.concatenate([y[:, :CHUNK], z], axis=1)
    bot = jnp.concatenate([z, y[:, CHUNK:]], axis=1)
    return jnp.concatenate([top, bot], axis=0)


def _mm_pair(xb, yb):
    return jnp.dot(xb, _pair_blockdiag(yb), preferred_element_type=F32)


def _unit_tri_inverse_pairs(l2s):
    bf = lambda xs: [x.astype(BF16) for x in xs]
    r = _iota((CHUNK, 2 * CHUNK), 0)
    c = _iota((CHUNK, 2 * CHUNK), 1) & (CHUNK - 1)
    base = 16
    in_base = (r // base) == (c // base)
    ds = [jnp.where(in_base, l2, 0.0) for l2 in l2s]
    ps = [jnp.where(r == c, 1.0, 0.0) - d for d in ds]
    powb = bf(ds)
    for _ in range(3):
        powb = bf([_mm_pair(x, x) for x in powb])
        ps = [p + _mm_pair(pb, x) for p, pb, x in zip(ps, bf(ps), powb)]
    s = base
    while s < CHUNK:
        off_diag = ((r // (2 * s)) == (c // (2 * s))) & ((r // s) != (c // s))
        mbs = bf([jnp.where(off_diag, l2, 0.0) for l2 in l2s])
        pbs = bf(ps)
        tbs = bf([_mm_pair(mb, pb) for mb, pb in zip(mbs, pbs)])
        ps = [p - _mm_pair(pb, tb) for p, pb, tb in zip(ps, pbs, tbs)]
        s *= 2
    return ps


def _deltanet_kernel(q_ref, k_ref, v_ref, z_ref, gcol_ref, grow_ref, cwq_ref, cwk_ref, cwv_ref, nw_ref,
                     o_ref, u_scr, w_scr, qg_scr, kd_scr, at_scr, gl_scr, os_scr, *, T):
    C = CHUNK
    N = T // C
    U = PREP_UNROLL
    h = pl.program_id(1)
    ri = _iota((C, C), 0)
    ci = _iota((C, C), 1)
    incl = (ri >= ci, ri <= ci)
    strict = (ri > ci, ri < ci)
    lane = _iota((C, LANES), 1)

    def conv_silu(ref, cw_ref, n, c0):
        prev = ref[pl.ds(pl.multiple_of(jnp.maximum(c0 - 8, 0), 8), 8), :] * jnp.where(n > 0, 1.0, 0.0)
        nxt = ref[pl.ds(pl.multiple_of(jnp.minimum(c0 + C, T - 8), 8), 8), :] * jnp.where(n < N - 1, 1.0, 0.0)
        ext = jnp.concatenate([prev, ref[pl.ds(c0, C), :], nxt], axis=0)
        cw = cw_ref[...]
        off = 8 - DN_CONV // 2
        acc = ext[off:off + C] * cw[0:1]
        for j in range(1, DN_CONV):
            acc = acc + ext[off + j:off + j + C] * cw[j:j + 1]
        return _silu(acc)

    def l2norm(t):
        return t * lax.rsqrt(jnp.sum(t * t, axis=-1, keepdims=True) + EPS)

    def prepare_group(g, carry):
        ns = [g * U + u for u in range(U)]
        c0s = [pl.multiple_of(n * C, C) for n in ns]
        qs = [l2norm(conv_silu(q_ref, cwq_ref, n, c0)) * (DN_HEAD_DIM ** -0.5) for n, c0 in zip(ns, c0s)]
        ks = [l2norm(conv_silu(k_ref, cwk_ref, n, c0)) for n, c0 in zip(ns, c0s)]
        vs = [conv_silu(v_ref, cwv_ref, n, c0) for n, c0 in zip(ns, c0s)]
        kbs = [k.astype(BF16) for k in ks]
        grams = [lax.dot_general(jnp.concatenate([q.astype(BF16), kb], axis=0), kb, NT_DIMS,
                                 preferred_element_type=F32) for q, kb in zip(qs, kbs)]
        betas, gccs, glasts, decays, l2s = [], [], [], [], []
        for u in range(U):
            gc = gcol_ref[pl.ds(c0s[u], C), :]
            col = lambda idx: jnp.sum(jnp.where(lane == idx, gc, 0.0), axis=1, keepdims=True)
            beta = (col(h), col(DN_HEADS + h))
            gcc = (col(2 * DN_HEADS + h), col(3 * DN_HEADS + h))
            gr = grow_ref[2 * DN_HEADS:, pl.ds(c0s[u], C)]
            sub = _iota(gr.shape, 0)
            row = lambda idx: jnp.sum(jnp.where(sub == idx, gr, 0.0), axis=0, keepdims=True)
            gcr = (row(h), row(DN_HEADS + h))
            kk = grams[u][C:]
            decay = [jnp.where(incl[d], jnp.exp(jnp.where(incl[d], gcc[d] - gcr[d], 0.0)), 0.0) for d in range(2)]
            l2s.append(jnp.concatenate([jnp.where(strict[d], beta[d] * kk * decay[d], 0.0) for d in range(2)],
                                       axis=1))
            betas.append(beta)
            gccs.append(gcc)
            glasts.append((gcr[0][:, C - 1:C], gcr[1][:, 0:1]))
            decays.append(decay)
        ainvs = _unit_tri_inverse_pairs(l2s)
        for u in range(U):
            rows = pl.ds(c0s[u], C)
            q, k, v, qk = qs[u], ks[u], vs[u], grams[u][:C]
            for d in range(2):
                eg = jnp.exp(gccs[u][d])
                rhs = jnp.concatenate([v * betas[u][d], (k * betas[u][d]) * eg], axis=1)
                uw = _mm(ainvs[u][:, d * C:(d + 1) * C], rhs)
                u_scr[d, rows, :] = uw[:, :DN_HEAD_DIM]
                w_scr[d, rows, :] = uw[:, DN_HEAD_DIM:].astype(BF16)
                qg_scr[d, rows, :] = (q * eg).astype(BF16)
                kd_scr[d, rows, :] = (k * jnp.exp(glasts[u][d] - gccs[u][d])).astype(BF16)
                at_scr[d, rows, :] = (qk * decays[u][d]).astype(BF16)
                gl_scr[d, ns[u]] = jnp.broadcast_to(jnp.exp(glasts[u][d]), (8, LANES))
        return carry

    lax.fori_loop(0, N // U, prepare_group, 0)

    def scan_step(d, n, state):
        rows = pl.ds(pl.multiple_of(n * C, C), C)
        sb = state.astype(BF16)
        ws_qs = jnp.dot(jnp.concatenate([w_scr[d, rows, :], qg_scr[d, rows, :]], axis=0), sb,
                        preferred_element_type=F32)
        v_new = (u_scr[d, rows, :] - ws_qs[:C]).astype(BF16)
        os_scr[d, rows, :] = ws_qs[C:] + jnp.dot(at_scr[d, rows, :], v_new, preferred_element_type=F32)
        return state * gl_scr[d, n][0:1, :] + lax.dot_general(kd_scr[d, rows, :], v_new, TN_DIMS,
                                                               preferred_element_type=F32)

    def scan(i, states):
        return scan_step(0, i, states[0]), scan_step(1, N - 1 - i, states[1])

    s0 = jnp.zeros((DN_HEAD_DIM, DN_HEAD_DIM), F32)
    lax.fori_loop(0, N, scan, (s0, s0))

    def finish(n, carry):
        rows = pl.ds(pl.multiple_of(n * C, C), C)
        o = os_scr[0, rows, :] + os_scr[1, rows, :]
        o = o * lax.rsqrt(jnp.mean(o * o, axis=-1, keepdims=True) + EPS) * nw_ref[...]
        o_ref[rows, :] = (o * _silu(z_ref[rows, :])).astype(BF16)
        return carry

    lax.fori_loop(0, N, finish, 0)


def _deltanet(dqkv, dz, gcol, grow, conv_w, norm_w, B, T):
    H = DN_HEADS
    N = T // CHUNK
    seq = lambda off: pl.BlockSpec((T, DN_HEAD_DIM), lambda b, h: (b, off + h))
    cw = lambda off: pl.BlockSpec((DN_CONV, DN_HEAD_DIM), lambda b, h: (0, off + h))
    return pl.pallas_call(
        functools.partial(_deltanet_kernel, T=T),
        grid=(B, H),
        in_specs=[seq(0), seq(H), seq(2 * H), seq(0),
                  pl.BlockSpec((T, LANES), lambda b, h: (b, 0)),
                  pl.BlockSpec((N_GATES, T), lambda b, h: (0, b)),
                  cw(0), cw(H), cw(2 * H),
                  pl.BlockSpec((1, DN_HEAD_DIM), lambda b, h: (0, 0))],
        out_specs=seq(0),
        out_shape=jax.ShapeDtypeStruct((B * T, DN_W), BF16),
        scratch_shapes=[pltpu.VMEM((2, T, DN_HEAD_DIM), F32),
                        pltpu.VMEM((2, T, DN_HEAD_DIM), BF16),
                        pltpu.VMEM((2, T, DN_HEAD_DIM), BF16),
                        pltpu.VMEM((2, T, DN_HEAD_DIM), BF16),
                        pltpu.VMEM((2, T, CHUNK), BF16),
                        pltpu.VMEM((2, N, 8, LANES), F32),
                        pltpu.VMEM((2, T, DN_HEAD_DIM), F32)],
        compiler_params=_params("parallel", "arbitrary"),
        name="deltanet",
    )(dqkv, dqkv, dqkv, dz, gcol, grow, conv_w, conv_w, conv_w, norm_w)


def _out_proj_kernel(att_ref, dn_ref, x_ref, woa_ref, wod_ref, g1_ref, g2_ref, wq_ref, x1_ref, qx_ref):
    mix = (jnp.dot(att_ref[...], woa_ref[...], preferred_element_type=F32)
           + jnp.dot(dn_ref[...], wod_ref[...], preferred_element_type=F32))
    x1 = x_ref[...] + _rms(mix, g1_ref[...])
    x1_ref[...] = x1
    qx_ref[...] = _mm(_rms(x1, g2_ref[...]), wq_ref[...]).astype(BF16)


def _out_proj(att, dn, x2d, woa, wod, g1, g2, wq):
    BT = x2d.shape[0]
    row = lambda i: (i, 0)
    full = lambda a: pl.BlockSpec(a.shape, lambda i: (0, 0))
    return pl.pallas_call(
        _out_proj_kernel,
        grid=(BT // TM,),
        in_specs=[pl.BlockSpec((TM, ATT_Q), row), pl.BlockSpec((TM, DN_W), row), pl.BlockSpec((TM, D_MODEL), row),
                  full(woa), full(wod), full(g1), full(g2), full(wq)],
        out_specs=[pl.BlockSpec((TM, D_MODEL), row), pl.BlockSpec((TM, X_W), row)],
        out_shape=[jax.ShapeDtypeStruct((BT, D_MODEL), F32), jax.ShapeDtypeStruct((BT, X_W), BF16)],
        compiler_params=_params("parallel"),
        name="out_proj",
    )(att, dn, x2d, woa, wod, g1, g2, wq)


def _mem_kv_kernel(mem_ref, g_ref, w_ref, kv_ref):
    kv_ref[...] = _mm(_rms(mem_ref[...], g_ref[...]), w_ref[...]).astype(BF16)


def _mem_kv(mem2d, g, wkv):
    M = mem2d.shape[0]
    mem_len = 256
    return pl.pallas_call(
        _mem_kv_kernel,
        grid=(M // mem_len,),
        in_specs=[pl.BlockSpec((mem_len, D_MODEL), lambda i: (i, 0)),
                  pl.BlockSpec(g.shape, lambda i: (0, 0)), pl.BlockSpec(wkv.shape, lambda i: (0, 0))],
        out_specs=pl.BlockSpec((mem_len, 2 * X_W), lambda i: (i, 0)),
        out_shape=jax.ShapeDtypeStruct((M, 2 * X_W), BF16),
        compiler_params=_params("parallel"),
        name="mem_kv",
    )(mem2d, g, wkv)


def _xattn_kernel(qx_ref, kv_ref, x1_ref, wo_ref, g_ref, x2_ref):
    outs = []
    for hd in range(X_HEADS):
        sl = slice(hd * X_HEAD_DIM, (hd + 1) * X_HEAD_DIM)
        s = lax.dot_general(qx_ref[:, sl], kv_ref[:, sl], NT_DIMS, preferred_element_type=F32)
        s = s * (X_HEAD_DIM ** -0.5)
        p = jnp.exp(s - jnp.max(s, axis=-1, keepdims=True))
        den = jnp.sum(p, axis=-1, keepdims=True)
        vsl = slice(X_W + hd * X_HEAD_DIM, X_W + (hd + 1) * X_HEAD_DIM)
        outs.append(jnp.dot(p.astype(BF16), kv_ref[:, vsl], preferred_element_type=F32) / den)
    o = jnp.concatenate(outs, axis=1)
    x2_ref[...] = x1_ref[...] + _rms(_mm(o, wo_ref[...]), g_ref[...])


def _xattn(qx, kv, x1, wo, g, B, T):
    n_t = T // TM
    mem_len = kv.shape[0] // B
    row = lambda b, i: (b * n_t + i, 0)
    return pl.pallas_call(
        _xattn_kernel,
        grid=(B, n_t),
        in_specs=[pl.BlockSpec((TM, X_W), row), pl.BlockSpec((mem_len, 2 * X_W), lambda b, i: (b, 0)),
                  pl.BlockSpec((TM, D_MODEL), row),
                  pl.BlockSpec(wo.shape, lambda b, i: (0, 0)), pl.BlockSpec(g.shape, lambda b, i: (0, 0))],
        out_specs=pl.BlockSpec((TM, D_MODEL), row),
        out_shape=jax.ShapeDtypeStruct((B * T, D_MODEL), F32),
        compiler_params=_params("parallel", "arbitrary"),
        name="xattn",
    )(qx, kv, x1, wo, g)


def _mlp_kernel(x_ref, g1_ref, w1_ref, w2_ref, g2_ref, y_ref):
    x = x_ref[...]
    hb = _rms(x, g1_ref[...]).astype(BF16)
    acc = jnp.zeros((TM, D_MODEL), F32)
    for c in range(D_FF // FF_BLOCK):
        sl = slice(c * FF_BLOCK, (c + 1) * FF_BLOCK)
        a = jnp.maximum(jnp.dot(hb, w1_ref[:, sl], preferred_element_type=F32), 0.0)
        acc = acc + jnp.dot((a * a).astype(BF16), w2_ref[sl, :], preferred_element_type=F32)
    y_ref[...] = x + _rms(acc, g2_ref[...])


def _mlp(x2, g1, w1, w2, g2):
    BT = x2.shape[0]
    row = lambda i: (i, 0)
    full = lambda a: pl.BlockSpec(a.shape, lambda i: (0, 0))
    return pl.pallas_call(
        _mlp_kernel,
        grid=(BT // TM,),
        in_specs=[pl.BlockSpec((TM, D_MODEL), row), full(g1), full(w1), full(w2), full(g2)],
        out_specs=pl.BlockSpec((TM, D_MODEL), row),
        out_shape=jax.ShapeDtypeStruct((BT, D_MODEL), F32),
        compiler_params=_params("parallel"),
        name="mlp",
    )(x2, g1, w1, w2, g2)


def _rope_tables(T):
    half = ROPE_DIM // 2
    inv = ROPE_THETA ** (-(jnp.arange(half, dtype=F32) * 2.0 / ROPE_DIM))
    ang = jnp.arange(T).astype(F32)[:, None] * inv[None, :]
    cos, sin = jnp.cos(ang), jnp.sin(ang)
    pad = ATT_HEAD_DIM - ROPE_DIM
    one_head = lambda a, b, fill: jnp.concatenate([a, b, jnp.full((T, pad), fill, F32)], axis=1)
    zeros = jnp.zeros((T, half), F32)
    two_heads = lambda t: jnp.concatenate([t, t], axis=1)
    return (two_heads(one_head(cos, cos, 1.0)), two_heads(one_head(-sin, zeros, 0.0)),
            two_heads(one_head(zeros, sin, 0.0)))


def _gate_params(a_log_f, a_log_b, dt_f, dt_b):
    zeros = jnp.zeros((2 * DN_HEADS,), F32)
    a_vec = jnp.concatenate([zeros, a_log_f.astype(F32), a_log_b.astype(F32)])
    d_vec = jnp.concatenate([zeros, dt_f.astype(F32), dt_b.astype(F32)])
    pad = jnp.zeros((LANES - N_GATES,), F32)
    pc = jnp.stack([jnp.concatenate([a_vec, pad]), jnp.concatenate([d_vec, pad])])
    pra = jnp.broadcast_to(a_vec[:, None], (N_GATES, TM))
    prd = jnp.broadcast_to(d_vec[:, None], (N_GATES, TM))
    return pc, pra, prd


def _layer(x, mem, w_in, attn_sink, conv_w, a_log_f, a_log_b, dt_f, dt_b, dn_norm_w, w_out,
           xa_wq, xa_wkv, xa_wo, mem_g, w1, w2, g_pre_mix, g_post_mix, g_pre_xa, g_post_xa,
           g_pre_mlp, g_post_mlp):
    B, T, _ = x.shape
    assert T % TM == 0 and T >= TQ + 2 * ATT_WINDOW
    x2d = x.reshape(B * T, D_MODEL)
    vec = lambda g: g.astype(F32).reshape(1, -1)
    hpg = ATT_HEADS // ATT_KV_HEADS
    head_order = [kvh * hpg + j for j in range(hpg) for kvh in range(ATT_KV_HEADS)]
    q_cols = jnp.concatenate([jnp.arange(hd * ATT_HEAD_DIM, (hd + 1) * ATT_HEAD_DIM) for hd in head_order])
    wq = w_in[:, q_cols].astype(BF16)
    wkv = w_in[:, O_AK:O_DQKV].astype(BF16)
    wd = w_in[:, O_DQKV:O_DZ].astype(BF16)
    wz = w_in[:, O_DZ:O_GATES].astype(BF16)
    wg = w_in[:, O_GATES:O_GATES + N_GATES]
    wgc = jnp.pad(wg, ((0, 0), (0, LANES - N_GATES))).astype(BF16)
    wgr = wg.T.astype(BF16)
    cos, sa, sb = _rope_tables(T)
    pc, pra, prd = _gate_params(a_log_f, a_log_b, dt_f, dt_b)

    aq, akv, dqkv, dz, gcol, grow = _in_proj(x2d, T, vec(g_pre_mix), wq, wkv, wd, wz, wgc, wgr,
                                             cos, sa, sb, pc, pra, prd)
    att = _win_attn(aq, akv, attn_sink.astype(F32), B, T)
    dn = _deltanet(dqkv, dz, gcol, grow, conv_w.astype(F32), vec(dn_norm_w), B, T)
    x1, qx = _out_proj(att, dn, x2d, w_out[q_cols].astype(BF16), w_out[ATT_Q:].astype(BF16),
                       vec(g_post_mix), vec(g_pre_xa), xa_wq.astype(BF16))
    kv = _mem_kv(mem.reshape(-1, D_MODEL), vec(mem_g), xa_wkv.astype(BF16))
    x2 = _xattn(qx, kv, x1, xa_wo.astype(BF16), vec(g_post_xa), B, T)
    y = _mlp(x2, vec(g_pre_mlp), w1.astype(BF16), w2.astype(BF16), vec(g_post_mlp))
    return y.reshape(B, T, D_MODEL)


def kernel(x_prompt, x_sample, mem_prompt, mem_sample, w_in, attn_sink, dn_conv_w, dn_A_log_f, dn_A_log_b,
           dn_dt_bias_f, dn_dt_bias_b, dn_norm_w, w_out, xa_wq, xa_wkv, xa_wo, mem_norm_g, mlp_w1, mlp_w2,
           norm_pre_mix, norm_post_mix, norm_pre_xa, norm_post_xa, norm_pre_mlp, norm_post_mlp):
    weights = (w_in, attn_sink, dn_conv_w, dn_A_log_f, dn_A_log_b, dn_dt_bias_f, dn_dt_bias_b, dn_norm_w,
               w_out, xa_wq, xa_wkv, xa_wo, mem_norm_g, mlp_w1, mlp_w2, norm_pre_mix, norm_post_mix,
               norm_pre_xa, norm_post_xa, norm_pre_mlp, norm_post_mlp)
    outs = []
    for x, mem in ((x_prompt, mem_prompt), (x_sample, mem_sample)):
        for l in range(w_in.shape[0]):
            x = _layer(x, mem, *(w[l] for w in weights))
        outs.append(x)
    return tuple(outs)
```

```python
import functools

import jax
import jax.numpy as jnp
from jax import lax
from jax.experimental import pallas as pl
from jax.experimental.pallas import tpu as pltpu

F32 = jnp.float32
BF16 = jnp.bfloat16

D_MODEL = 1024
ATT_HEADS = 8
ATT_KV_HEADS = 2
ATT_HEAD_DIM = 64
ATT_WINDOW = 128
ROPE_THETA = 500000.0
ROPE_DIM = ATT_HEAD_DIM // 4
DN_HEADS = 4
DN_HEAD_DIM = 128
DN_CONV = 5
X_HEADS = 4
X_HEAD_DIM = 128
D_FF = 4 * D_MODEL
EPS = 1e-6

ATT_Q = ATT_HEADS * ATT_HEAD_DIM
ATT_KV = ATT_KV_HEADS * ATT_HEAD_DIM
DN_W = DN_HEADS * DN_HEAD_DIM
X_W = X_HEADS * X_HEAD_DIM
O_AK = ATT_Q
O_AV = O_AK + ATT_KV
O_DQKV = O_AV + ATT_KV
O_DZ = O_DQKV + 3 * DN_W
O_GATES = O_DZ + DN_W
N_GATES = 4 * DN_HEADS

LANES = 128
TM = 512
TQ = 128
TQ_STEP = 256
CHUNK = 128
PREP_UNROLL = 8
FF_BLOCK = 1024
VMEM_LIMIT = 56 * 1024 * 1024

NT_DIMS = (((1,), (1,)), ((), ()))
TN_DIMS = (((0,), (0,)), ((), ()))


def _mm(a, b):
    return jnp.dot(a.astype(BF16), b.astype(BF16), preferred_element_type=F32)


def _mm_nt(a, b):
    return lax.dot_general(a.astype(BF16), b.astype(BF16), NT_DIMS, preferred_element_type=F32)


def _mm_tn(a, b):
    return lax.dot_general(a.astype(BF16), b.astype(BF16), TN_DIMS, preferred_element_type=F32)


def _rms(x, g):
    return x * lax.rsqrt(jnp.mean(x * x, axis=-1, keepdims=True) + EPS) * g


def _softplus(x):
    return jnp.maximum(x, 0.0) + jnp.log1p(jnp.exp(-jnp.abs(x)))


def _silu(x):
    return x * jax.nn.sigmoid(x)


def _iota(shape, dim):
    return lax.broadcasted_iota(jnp.int32, shape, dim)


def _params(*sem):
    return pltpu.CompilerParams(dimension_semantics=sem, vmem_limit_bytes=VMEM_LIMIT)


def _in_proj_kernel(x_ref, g_ref, wq_ref, wk_ref, wv_ref, wd_ref, wz_ref, wgc_ref,
                    cos_ref, sa_ref, sb_ref, pc_ref,
                    aq_ref, ak_ref, avt_ref, dqkv_ref, dz_ref, gcol_ref, grow_ref):
    hb = _rms(x_ref[...], g_ref[...]).astype(BF16)
    cos = cos_ref[...]
    sa = sa_ref[...]
    sb = sb_ref[...]

    def rope(t):
        return t * cos + pltpu.roll(t, LANES - ROPE_DIM // 2, 1) * sa + pltpu.roll(t, ROPE_DIM // 2, 1) * sb

    q = jnp.dot(hb, wq_ref[...], preferred_element_type=F32)
    for j in range(ATT_Q // LANES):
        sl = slice(j * LANES, (j + 1) * LANES)
        aq_ref[:, sl] = (rope(q[:, sl]) * (ATT_HEAD_DIM ** -0.5)).astype(BF16)
    ak_ref[...] = rope(jnp.dot(hb, wk_ref[...], preferred_element_type=F32)).astype(BF16)
    avt_ref[...] = jnp.dot(hb, wv_ref[...], preferred_element_type=F32).astype(BF16).T
    dqkv_ref[...] = jnp.dot(hb, wd_ref[...], preferred_element_type=F32)
    dz_ref[...] = jnp.dot(hb, wz_ref[...], preferred_element_type=F32)

    raw = jnp.dot(hb, wgc_ref[...], preferred_element_type=F32)
    lane = _iota(raw.shape, 1)
    gval = -jnp.exp(pc_ref[0:1, :]) * _softplus(raw + pc_ref[1:2, :])
    gates = jnp.where(lane < 2 * DN_HEADS, jax.nn.sigmoid(raw), gval)
    pos = _iota(raw.shape, 0) & (CHUNK - 1)
    cf = gates
    cb = gates
    s = 1
    while s < CHUNK:
        cf = cf + jnp.where(pos >= s, pltpu.roll(cf, s, 0), 0.0)
        cb = cb + jnp.where(pos < CHUNK - s, pltpu.roll(cb, TM - s, 0), 0.0)
        s *= 2
    gcol = jnp.where(lane < 2 * DN_HEADS, gates, jnp.where(lane < 3 * DN_HEADS, cf, cb))
    gcol_ref[...] = gcol
    grow_ref[...] = gcol.T[:N_GATES]


def _in_proj(x2d, T, g, wq, wk, wv, wd, wz, wgc, cos, sa, sb, pc):
    BT = x2d.shape[0]
    n_t = T // TM
    row = lambda i: (i, 0)
    const = lambda i: (0, 0)
    tab = lambda i: (i % n_t, 0)
    full = lambda a: pl.BlockSpec(a.shape, const)
    return pl.pallas_call(
        _in_proj_kernel,
        grid=(BT // TM,),
        in_specs=[pl.BlockSpec((TM, D_MODEL), row), full(g), full(wq), full(wk), full(wv), full(wd), full(wz),
                  full(wgc),
                  pl.BlockSpec((TM, LANES), tab), pl.BlockSpec((TM, LANES), tab), pl.BlockSpec((TM, LANES), tab),
                  full(pc)],
        out_specs=[pl.BlockSpec((TM, ATT_Q), row), pl.BlockSpec((TM, ATT_KV), row),
                   pl.BlockSpec((ATT_KV, TM), lambda i: (0, i)),
                   pl.BlockSpec((TM, 3 * DN_W), row), pl.BlockSpec((TM, DN_W), row),
                   pl.BlockSpec((TM, LANES), row), pl.BlockSpec((N_GATES, TM), lambda i: (0, i))],
        out_shape=[jax.ShapeDtypeStruct((BT, ATT_Q), BF16), jax.ShapeDtypeStruct((BT, ATT_KV), BF16),
                   jax.ShapeDtypeStruct((ATT_KV, BT), BF16),
                   jax.ShapeDtypeStruct((BT, 3 * DN_W), F32), jax.ShapeDtypeStruct((BT, DN_W), F32),
                   jax.ShapeDtypeStruct((BT, LANES), F32), jax.ShapeDtypeStruct((N_GATES, BT), F32)],
        compiler_params=_params("parallel"),
        name="in_proj",
    )(x2d, g, wq, wk, wv, wd, wz, wgc, cos, sa, sb, pc)


def _win_attn_kernel(sink_ref, q_ref, k_ref, vt_ref, o_ref, *, T):
    W = TQ + 2 * ATT_WINDOW
    hpg = ATT_HEADS // ATT_KV_HEADS
    items = []
    for sub in range(TQ_STEP // TQ):
        t0 = pl.program_id(1) * TQ_STEP + sub * TQ
        start = pl.multiple_of(jnp.clip(t0 - ATT_WINDOW, 0, T - W), LANES)
        k = k_ref[pl.ds(start, W), :]
        vt = vt_ref[:, pl.ds(start, W)]
        k_lo = _iota(k.shape, 1) < ATT_HEAD_DIM
        v_lo = _iota(vt.shape, 0) < ATT_HEAD_DIM
        kk = jnp.concatenate([jnp.where(k_lo, k, jnp.zeros_like(k)), jnp.where(k_lo, jnp.zeros_like(k), k)], axis=0)
        vvt = jnp.concatenate([jnp.where(v_lo, vt, jnp.zeros_like(vt)), jnp.where(v_lo, jnp.zeros_like(vt), vt)],
                              axis=1)
        kpos = start + _iota((W, 2 * TQ), 0)
        qpos = t0 + (_iota((W, 2 * TQ), 1) & (TQ - 1))
        cap = jnp.where(jnp.abs(qpos - kpos) <= ATT_WINDOW, jnp.inf, -1e30)
        rows = slice(sub * TQ, (sub + 1) * TQ)
        for pair in range(ATT_Q // LANES // 2):
            groups = (2 * pair, 2 * pair + 1)
            qp = jnp.concatenate([q_ref[rows, g * LANES:(g + 1) * LANES] for g in groups], axis=0)
            items.append(dict(kk=kk, vvt=vvt, cap=cap, qp=qp, rows=rows, groups=groups))
    for it in items:
        it["st"] = lax.dot_general(it["kk"], it["qp"], NT_DIMS, preferred_element_type=F32)
    for it in items:
        ps, dens = [], []
        for kvh in range(ATT_KV_HEADS):
            sh = jnp.minimum(it["st"][kvh * W:(kvh + 1) * W], it["cap"])
            sk = jnp.concatenate([jnp.full((1, TQ), sink_ref[kvh * hpg + g], F32) for g in it["groups"]], axis=1)
            m = jnp.maximum(jnp.max(sh, axis=0, keepdims=True), sk)
            p = jnp.exp(sh - m)
            dens.append(jnp.sum(p, axis=0, keepdims=True) + jnp.exp(sk - m))
            ps.append(p.astype(BF16))
        it["p"] = jnp.concatenate(ps, axis=0)
        it["dens"] = dens
    for it in items:
        ot = jnp.dot(it["vvt"], it["p"], preferred_element_type=F32)
        ot = ot / jnp.where(_iota(ot.shape, 0) < ATT_HEAD_DIM, it["dens"][0], it["dens"][1])
        for i, g in enumerate(it["groups"]):
            o_ref[g * LANES:(g + 1) * LANES, it["rows"]] = ot[:, i * TQ:(i + 1) * TQ].astype(BF16)


def _win_attn(aq, ak, avt, sink, B, T):
    n_q = T // TQ_STEP
    return pl.pallas_call(
        functools.partial(_win_attn_kernel, T=T),
        grid=(B, n_q),
        in_specs=[pl.BlockSpec(memory_space=pltpu.SMEM),
                  pl.BlockSpec((TQ_STEP, ATT_Q), lambda b, i: (b * n_q + i, 0)),
                  pl.BlockSpec((T, ATT_KV), lambda b, i: (b, 0)),
                  pl.BlockSpec((ATT_KV, T), lambda b, i: (0, b))],
        out_specs=pl.BlockSpec((ATT_Q, TQ_STEP), lambda b, i: (0, b * n_q + i)),
        out_shape=jax.ShapeDtypeStruct((ATT_Q, B * T), BF16),
        compiler_params=_params("parallel", "arbitrary"),
        name="win_attn",
    )(sink, aq, ak, avt)


def _pair_blockdiag(y):
    z = jnp.zeros((CHUNK, CHUNK), y.dtype)
    top = jnp.concatenate([y[:, :CHUNK], z], axis=1)
    bot = jnp.concatenate([z, y[:, CHUNK:]], axis=1)
    return jnp.concatenate([top, bot], axis=0)


def _mm_pair(xb, yb):
    return jnp.dot(xb, _pair_blockdiag(yb), preferred_element_type=F32)


def _unit_tri_inverse_pairs(l2s):
    bf = lambda xs: [x.astype(BF16) for x in xs]
    r = _iota((CHUNK, 2 * CHUNK), 0)
    c = _iota((CHUNK, 2 * CHUNK), 1) & (CHUNK - 1)
    base = 16
    in_base = (r // base) == (c // base)
    ds = [jnp.where(in_base, l2, 0.0) for l2 in l2s]
    ps = [jnp.where(r == c, 1.0, 0.0) - d for d in ds]
    powb = bf(ds)
    for _ in range(3):
        powb = bf([_mm_pair(x, x) for x in powb])
        ps = [p + _mm_pair(pb, x) for p, pb, x in zip(ps, bf(ps), powb)]
    s = base
    while s < CHUNK:
        off_diag = ((r // (2 * s)) == (c // (2 * s))) & ((r // s) != (c // s))
        mbs = bf([jnp.where(off_diag, l2, 0.0) for l2 in l2s])
        pbs = bf(ps)
        tbs = bf([_mm_pair(mb, pb) for mb, pb in zip(mbs, pbs)])
        ps = [p - _mm_pair(pb, tb) for p, pb, tb in zip(ps, pbs, tbs)]
        s *= 2
    return ps


def _deltanet_kernel(q_ref, k_ref, v_ref, z_ref, gcol_ref, grow_ref, cwq_ref, cwk_ref, cwv_ref, nw_ref,
                     o_ref, u_scr, w_scr, qg_scr, kd_scr, at_scr, gl_scr, os_scr, *, T):
    C = CHUNK
    N = T // C
    U = PREP_UNROLL
    h = pl.program_id(1)
    ri = _iota((C, C), 0)
    ci = _iota((C, C), 1)
    incl = (ri >= ci, ri <= ci)
    strict = (ri > ci, ri < ci)
    lane = _iota((C, LANES), 1)

    def conv_silu(ref, cw_ref, n, c0):
        prev = ref[pl.ds(pl.multiple_of(jnp.maximum(c0 - 8, 0), 8), 8), :] * jnp.where(n > 0, 1.0, 0.0)
        nxt = ref[pl.ds(pl.multiple_of(jnp.minimum(c0 + C, T - 8), 8), 8), :] * jnp.where(n < N - 1, 1.0, 0.0)
        ext = jnp.concatenate([prev, ref[pl.ds(c0, C), :], nxt], axis=0)
        cw = cw_ref[...]
        off = 8 - DN_CONV // 2
        acc = ext[off:off + C] * cw[0:1]
        for j in range(1, DN_CONV):
            acc = acc + ext[off + j:off + j + C] * cw[j:j + 1]
        return _silu(acc)

    def l2norm(t):
        return t * lax.rsqrt(jnp.sum(t * t, axis=-1, keepdims=True) + EPS)

    def prepare_group(g, carry):
        ns = [g * U + u for u in range(U)]
        c0s = [pl.multiple_of(n * C, C) for n in ns]
        qs = [l2norm(conv_silu(q_ref, cwq_ref, n, c0)) * (DN_HEAD_DIM ** -0.5) for n, c0 in zip(ns, c0s)]
        ks = [l2norm(conv_silu(k_ref, cwk_ref, n, c0)) for n, c0 in zip(ns, c0s)]
        vs = [conv_silu(v_ref, cwv_ref, n, c0) for n, c0 in zip(ns, c0s)]
        kbs = [k.astype(BF16) for k in ks]
        grams = [lax.dot_general(jnp.concatenate([q.astype(BF16), kb], axis=0), kb, NT_DIMS,
                                 preferred_element_type=F32) for q, kb in zip(qs, kbs)]
        betas, gccs, glasts, decays, l2s = [], [], [], [], []
        for u in range(U):
            gc = gcol_ref[pl.ds(c0s[u], C), :]
            col = lambda idx: jnp.sum(jnp.where(lane == idx, gc, 0.0), axis=1, keepdims=True)
            beta = (col(h), col(DN_HEADS + h))
            gcc = (col(2 * DN_HEADS + h), col(3 * DN_HEADS + h))
            gr = grow_ref[2 * DN_HEADS:, pl.ds(c0s[u], C)]
            sub = _iota(gr.shape, 0)
            row = lambda idx: jnp.sum(jnp.where(sub == idx, gr, 0.0), axis=0, keepdims=True)
            gcr = (row(h), row(DN_HEADS + h))
            kk = grams[u][C:]
            decay = [jnp.where(incl[d], jnp.exp(jnp.where(incl[d], gcc[d] - gcr[d], 0.0)), 0.0) for d in range(2)]
            l2s.append(jnp.concatenate([jnp.where(strict[d], beta[d] * kk * decay[d], 0.0) for d in range(2)],
                                       axis=1))
            betas.append(beta)
            gccs.append(gcc)
            glasts.append((gcr[0][:, C - 1:C], gcr[1][:, 0:1]))
            decays.append(decay)
        ainvs = _unit_tri_inverse_pairs(l2s)
        for u in range(U):
            rows = pl.ds(c0s[u], C)
            q, k, v, qk = qs[u], ks[u], vs[u], grams[u][:C]
            for d in range(2):
                eg = jnp.exp(gccs[u][d])
                rhs = jnp.concatenate([v * betas[u][d], (k * betas[u][d]) * eg], axis=1)
                uw = _mm(ainvs[u][:, d * C:(d + 1) * C], rhs)
                u_scr[d, rows, :] = uw[:, :DN_HEAD_DIM]
                w_scr[d, rows, :] = uw[:, DN_HEAD_DIM:].astype(BF16)
                qg_scr[d, rows, :] = (q * eg).astype(BF16)
                kd_scr[d, rows, :] = (k * jnp.exp(glasts[u][d] - gccs[u][d])).astype(BF16)
                at_scr[d, rows, :] = (qk * decays[u][d]).astype(BF16)
                gl_scr[d, ns[u]] = jnp.broadcast_to(jnp.exp(glasts[u][d]), (8, LANES))
        return carry

    lax.fori_loop(0, N // U, prepare_group, 0)

    def scan_step(d, n, state):
        rows = pl.ds(pl.multiple_of(n * C, C), C)
        sb = state.astype(BF16)
        ws_qs = jnp.dot(jnp.concatenate([w_scr[d, rows, :], qg_scr[d, rows, :]], axis=0), sb,
                        preferred_element_type=F32)
        v_new = (u_scr[d, rows, :] - ws_qs[:C]).astype(BF16)
        os_scr[d, rows, :] = ws_qs[C:] + jnp.dot(at_scr[d, rows, :], v_new, preferred_element_type=F32)
        return state * gl_scr[d, n][0:1, :] + lax.dot_general(kd_scr[d, rows, :], v_new, TN_DIMS,
                                                               preferred_element_type=F32)

    def scan(i, states):
        return scan_step(0, i, states[0]), scan_step(1, N - 1 - i, states[1])

    s0 = jnp.zeros((DN_HEAD_DIM, DN_HEAD_DIM), F32)
    lax.fori_loop(0, N, scan, (s0, s0))

    def finish(n, carry):
        rows = pl.ds(pl.multiple_of(n * C, C), C)
        o = os_scr[0, rows, :] + os_scr[1, rows, :]
        o = o * lax.rsqrt(jnp.mean(o * o, axis=-1, keepdims=True) + EPS) * nw_ref[...]
        o_ref[rows, :] = (o * _silu(z_ref[rows, :])).astype(BF16)
        return carry

    lax.fori_loop(0, N, finish, 0)


def _deltanet(dqkv, dz, gcol, grow, conv_w, norm_w, B, T):
    H = DN_HEADS
    N = T // CHUNK
    seq = lambda off: pl.BlockSpec((T, DN_HEAD_DIM), lambda b, h: (b, off + h))
    cw = lambda off: pl.BlockSpec((DN_CONV, DN_HEAD_DIM), lambda b, h: (0, off + h))
    return pl.pallas_call(
        functools.partial(_deltanet_kernel, T=T),
        grid=(B, H),
        in_specs=[seq(0), seq(H), seq(2 * H), seq(0),
                  pl.BlockSpec((T, LANES), lambda b, h: (b, 0)),
                  pl.BlockSpec((N_GATES, T), lambda b, h: (0, b)),
                  cw(0), cw(H), cw(2 * H),
                  pl.BlockSpec((1, DN_HEAD_DIM), lambda b, h: (0, 0))],
        out_specs=seq(0),
        out_shape=jax.ShapeDtypeStruct((B * T, DN_W), BF16),
        scratch_shapes=[pltpu.VMEM((2, T, DN_HEAD_DIM), F32),
                        pltpu.VMEM((2, T, DN_HEAD_DIM), BF16),
                        pltpu.VMEM((2, T, DN_HEAD_DIM), BF16),
                        pltpu.VMEM((2, T, DN_HEAD_DIM), BF16),
                        pltpu.VMEM((2, T, CHUNK), BF16),
                        pltpu.VMEM((2, N, 8, LANES), F32),
                        pltpu.VMEM((2, T, DN_HEAD_DIM), F32)],
        compiler_params=_params("parallel", "arbitrary"),
        name="deltanet",
    )(dqkv, dqkv, dqkv, dz, gcol, grow, conv_w, conv_w, conv_w, norm_w)


def _out_proj_kernel(att_t_ref, dn_ref, x_ref, woa_ref, wod_ref, g1_ref, g2_ref, wq_ref, x1_ref, qx_ref):
    mix = (lax.dot_general(att_t_ref[...], woa_ref[...], TN_DIMS, preferred_element_type=F32)
           + jnp.dot(dn_ref[...], wod_ref[...], preferred_element_type=F32))
    x1 = x_ref[...] + _rms(mix, g1_ref[...])
    x1_ref[...] = x1
    qx_ref[...] = _mm(_rms(x1, g2_ref[...]), wq_ref[...]).astype(BF16)


def _out_proj(att_t, dn, x2d, woa, wod, g1, g2, wq):
    BT = x2d.shape[0]
    row = lambda i: (i, 0)
    full = lambda a: pl.BlockSpec(a.shape, lambda i: (0, 0))
    return pl.pallas_call(
        _out_proj_kernel,
        grid=(BT // TM,),
        in_specs=[pl.BlockSpec((ATT_Q, TM), lambda i: (0, i)), pl.BlockSpec((TM, DN_W), row),
                  pl.BlockSpec((TM, D_MODEL), row),
                  full(woa), full(wod), full(g1), full(g2), full(wq)],
        out_specs=[pl.BlockSpec((TM, D_MODEL), row), pl.BlockSpec((TM, X_W), row)],
        out_shape=[jax.ShapeDtypeStruct((BT, D_MODEL), F32), jax.ShapeDtypeStruct((BT, X_W), BF16)],
        compiler_params=_params("parallel"),
        name="out_proj",
    )(att_t, dn, x2d, woa, wod, g1, g2, wq)


def _mem_kv_kernel(mem_ref, g_ref, w_ref, kv_ref):
    kv_ref[...] = _mm(_rms(mem_ref[...], g_ref[...]), w_ref[...]).astype(BF16)


def _mem_kv(mem2d, g, wkv):
    M = mem2d.shape[0]
    mem_len = 256
    return pl.pallas_call(
        _mem_kv_kernel,
        grid=(M // mem_len,),
        in_specs=[pl.BlockSpec((mem_len, D_MODEL), lambda i: (i, 0)),
                  pl.BlockSpec(g.shape, lambda i: (0, 0)), pl.BlockSpec(wkv.shape, lambda i: (0, 0))],
        out_specs=pl.BlockSpec((mem_len, 2 * X_W), lambda i: (i, 0)),
        out_shape=jax.ShapeDtypeStruct((M, 2 * X_W), BF16),
        compiler_params=_params("parallel"),
        name="mem_kv",
    )(mem2d, g, wkv)


def _xattn_kernel(qx_ref, kv_ref, x1_ref, wo_ref, g_ref, x2_ref):
    outs = []
    for hd in range(X_HEADS):
        sl = slice(hd * X_HEAD_DIM, (hd + 1) * X_HEAD_DIM)
        s = lax.dot_general(qx_ref[:, sl], kv_ref[:, sl], NT_DIMS, preferred_element_type=F32)
        s = s * (X_HEAD_DIM ** -0.5)
        p = jnp.exp(s - jnp.max(s, axis=-1, keepdims=True))
        den = jnp.sum(p, axis=-1, keepdims=True)
        vsl = slice(X_W + hd * X_HEAD_DIM, X_W + (hd + 1) * X_HEAD_DIM)
        outs.append(jnp.dot(p.astype(BF16), kv_ref[:, vsl], preferred_element_type=F32) / den)
    o = jnp.concatenate(outs, axis=1)
    x2_ref[...] = x1_ref[...] + _rms(_mm(o, wo_ref[...]), g_ref[...])


def _xattn(qx, kv, x1, wo, g, B, T):
    n_t = T // TM
    mem_len = kv.shape[0] // B
    row = lambda b, i: (b * n_t + i, 0)
    return pl.pallas_call(
        _xattn_kernel,
        grid=(B, n_t),
        in_specs=[pl.BlockSpec((TM, X_W), row), pl.BlockSpec((mem_len, 2 * X_W), lambda b, i: (b, 0)),
                  pl.BlockSpec((TM, D_MODEL), row),
                  pl.BlockSpec(wo.shape, lambda b, i: (0, 0)), pl.BlockSpec(g.shape, lambda b, i: (0, 0))],
        out_specs=pl.BlockSpec((TM, D_MODEL), row),
        out_shape=jax.ShapeDtypeStruct((B * T, D_MODEL), F32),
        compiler_params=_params("parallel", "arbitrary"),
        name="xattn",
    )(qx, kv, x1, wo, g)


def _mlp_kernel(x_ref, g1_ref, w1_ref, w2_ref, g2_ref, y_ref):
    x = x_ref[...]
    hb = _rms(x, g1_ref[...]).astype(BF16)
    acc = jnp.zeros((TM, D_MODEL), F32)
    for c in range(D_FF // FF_BLOCK):
        sl = slice(c * FF_BLOCK, (c + 1) * FF_BLOCK)
        a = jnp.maximum(jnp.dot(hb, w1_ref[:, sl], preferred_element_type=F32), 0.0)
        acc = acc + jnp.dot((a * a).astype(BF16), w2_ref[sl, :], preferred_element_type=F32)
    y_ref[...] = x + _rms(acc, g2_ref[...])


def _mlp(x2, g1, w1, w2, g2):
    BT = x2.shape[0]
    row = lambda i: (i, 0)
    full = lambda a: pl.BlockSpec(a.shape, lambda i: (0, 0))
    return pl.pallas_call(
        _mlp_kernel,
        grid=(BT // TM,),
        in_specs=[pl.BlockSpec((TM, D_MODEL), row), full(g1), full(w1), full(w2), full(g2)],
        out_specs=pl.BlockSpec((TM, D_MODEL), row),
        out_shape=jax.ShapeDtypeStruct((BT, D_MODEL), F32),
        compiler_params=_params("parallel"),
        name="mlp",
    )(x2, g1, w1, w2, g2)


def _rope_tables(T):
    half = ROPE_DIM // 2
    inv = ROPE_THETA ** (-(jnp.arange(half, dtype=F32) * 2.0 / ROPE_DIM))
    ang = jnp.arange(T).astype(F32)[:, None] * inv[None, :]
    cos, sin = jnp.cos(ang), jnp.sin(ang)
    pad = ATT_HEAD_DIM - ROPE_DIM
    one_head = lambda a, b, fill: jnp.concatenate([a, b, jnp.full((T, pad), fill, F32)], axis=1)
    zeros = jnp.zeros((T, half), F32)
    two_heads = lambda t: jnp.concatenate([t, t], axis=1)
    return (two_heads(one_head(cos, cos, 1.0)), two_heads(one_head(-sin, zeros, 0.0)),
            two_heads(one_head(zeros, sin, 0.0)))


def _gate_params(a_log_f, a_log_b, dt_f, dt_b):
    zeros = jnp.zeros((2 * DN_HEADS,), F32)
    a_vec = jnp.concatenate([zeros, a_log_f.astype(F32), a_log_b.astype(F32)])
    d_vec = jnp.concatenate([zeros, dt_f.astype(F32), dt_b.astype(F32)])
    pad = jnp.zeros((LANES - N_GATES,), F32)
    return jnp.stack([jnp.concatenate([a_vec, pad]), jnp.concatenate([d_vec, pad])])


def _layer(x, mem, w_in, attn_sink, conv_w, a_log_f, a_log_b, dt_f, dt_b, dn_norm_w, w_out,
           xa_wq, xa_wkv, xa_wo, mem_g, w1, w2, g_pre_mix, g_post_mix, g_pre_xa, g_post_xa,
           g_pre_mlp, g_post_mlp):
    B, T, _ = x.shape
    assert T % TM == 0 and T >= TQ + 2 * ATT_WINDOW
    x2d = x.reshape(B * T, D_MODEL)
    vec = lambda g: g.astype(F32).reshape(1, -1)
    hpg = ATT_HEADS // ATT_KV_HEADS
    head_order = [kvh * hpg + j for j in range(hpg) for kvh in range(ATT_KV_HEADS)]
    q_cols = jnp.concatenate([jnp.arange(hd * ATT_HEAD_DIM, (hd + 1) * ATT_HEAD_DIM) for hd in head_order])
    wq = w_in[:, q_cols].astype(BF16)
    wk = w_in[:, O_AK:O_AV].astype(BF16)
    wv = w_in[:, O_AV:O_DQKV].astype(BF16)
    wd = w_in[:, O_DQKV:O_DZ].astype(BF16)
    wz = w_in[:, O_DZ:O_GATES].astype(BF16)
    wg = w_in[:, O_GATES:O_GATES + N_GATES]
    wgc = jnp.pad(wg, ((0, 0), (0, LANES - N_GATES))).astype(BF16)
    cos, sa, sb = _rope_tables(T)
    pc = _gate_params(a_log_f, a_log_b, dt_f, dt_b)

    aq, ak, avt, dqkv, dz, gcol, grow = _in_proj(x2d, T, vec(g_pre_mix), wq, wk, wv, wd, wz, wgc,
                                                 cos, sa, sb, pc)
    att_t = _win_attn(aq, ak, avt, attn_sink.astype(F32), B, T)
    dn = _deltanet(dqkv, dz, gcol, grow, conv_w.astype(F32), vec(dn_norm_w), B, T)
    x1, qx = _out_proj(att_t, dn, x2d, w_out[q_cols].astype(BF16), w_out[ATT_Q:].astype(BF16),
                       vec(g_post_mix), vec(g_pre_xa), xa_wq.astype(BF16))
    kv = _mem_kv(mem.reshape(-1, D_MODEL), vec(mem_g), xa_wkv.astype(BF16))
    x2 = _xattn(qx, kv, x1, xa_wo.astype(BF16), vec(g_post_xa), B, T)
    y = _mlp(x2, vec(g_pre_mlp), w1.astype(BF16), w2.astype(BF16), vec(g_post_mlp))
    return y.reshape(B, T, D_MODEL)


def kernel(x_prompt, x_sample, mem_prompt, mem_sample, w_in, attn_sink, dn_conv_w, dn_A_log_f, dn_A_log_b,
           dn_dt_bias_f, dn_dt_bias_b, dn_norm_w, w_out, xa_wq, xa_wkv, xa_wo, mem_norm_g, mlp_w1, mlp_w2,
           norm_pre_mix, norm_post_mix, norm_pre_xa, norm_post_xa, norm_pre_mlp, norm_post_mlp):
    weights = (w_in, attn_sink, dn_conv_w, dn_A_log_f, dn_A_log_b, dn_dt_bias_f, dn_dt_bias_b, dn_norm_w,
               w_out, xa_wq, xa_wkv, xa_wo, mem_norm_g, mlp_w1, mlp_w2, norm_pre_mix, norm_post_mix,
               norm_pre_xa, norm_post_xa, norm_pre_mlp, norm_post_mlp)
    outs = []
    for x, mem in ((x_prompt, mem_prompt), (x_sample, mem_sample)):
        for l in range(w_in.shape[0]):
            x = _layer(x, mem, *(w[l] for w in weights))
        outs.append(x)
    return tuple(outs)
```

```python
import functools

import jax
import jax.numpy as jnp
from jax import lax
from jax.experimental import pallas as pl
from jax.experimental.pallas import tpu as pltpu

F32 = jnp.float32
BF16 = jnp.bfloat16

D_MODEL = 1024
ATT_HEADS = 8
ATT_KV_HEADS = 2
ATT_HEAD_DIM = 64
ATT_WINDOW = 128
ROPE_THETA = 500000.0
ROPE_DIM = ATT_HEAD_DIM // 4
DN_HEADS = 4
DN_HEAD_DIM = 128
DN_CONV = 5
X_HEADS = 4
X_HEAD_DIM = 128
D_FF = 4 * D_MODEL
EPS = 1e-6

ATT_Q = ATT_HEADS * ATT_HEAD_DIM
ATT_KV = ATT_KV_HEADS * ATT_HEAD_DIM
DN_W = DN_HEADS * DN_HEAD_DIM
X_W = X_HEADS * X_HEAD_DIM
O_AK = ATT_Q
O_AV = O_AK + ATT_KV
O_DQKV = O_AV + ATT_KV
O_DZ = O_DQKV + 3 * DN_W
O_GATES = O_DZ + DN_W
N_GATES = 4 * DN_HEADS

LANES = 128
TM = 512
TQ = 128
TQ_STEP = 256
CHUNK = 128
PREP_UNROLL = 8
PAD = 8
COPY_ROWS = 512
SUB_TILES = 2
TS = SUB_TILES * TM
FF_BLOCK = 1024
VMEM_LIMIT = 56 * 1024 * 1024

NT_DIMS = (((1,), (1,)), ((), ()))
TN_DIMS = (((0,), (0,)), ((), ()))


def _mm(a, b):
    return jnp.dot(a.astype(BF16), b.astype(BF16), preferred_element_type=F32)


def _mm_nt(a, b):
    return lax.dot_general(a.astype(BF16), b.astype(BF16), NT_DIMS, preferred_element_type=F32)


def _mm_tn(a, b):
    return lax.dot_general(a.astype(BF16), b.astype(BF16), TN_DIMS, preferred_element_type=F32)


def _rms(x, g):
    return x * lax.rsqrt(jnp.mean(x * x, axis=-1, keepdims=True) + EPS) * g


def _softplus(x):
    return jnp.maximum(x, 0.0) + jnp.log1p(jnp.exp(-jnp.abs(x)))


def _silu(x):
    return x * jax.nn.sigmoid(x)


def _iota(shape, dim):
    return lax.broadcasted_iota(jnp.int32, shape, dim)


def _params(*sem):
    return pltpu.CompilerParams(dimension_semantics=sem, vmem_limit_bytes=VMEM_LIMIT)


def _in_proj_kernel(x_ref, g_ref, wq_ref, wk_ref, wv_ref, wd_ref, wz_ref, wgc_ref,
                    cos_ref, sa_ref, sb_ref, pc_ref,
                    aq_ref, ak_ref, avt_ref, dqkv_ref, dz_ref, gcol_ref, grow_ref):
    hb = _rms(x_ref[...], g_ref[...]).astype(BF16)
    cos = cos_ref[...]
    sa = sa_ref[...]
    sb = sb_ref[...]

    def rope(t):
        return t * cos + pltpu.roll(t, LANES - ROPE_DIM // 2, 1) * sa + pltpu.roll(t, ROPE_DIM // 2, 1) * sb

    q = jnp.dot(hb, wq_ref[...], preferred_element_type=F32)
    for j in range(ATT_Q // LANES):
        sl = slice(j * LANES, (j + 1) * LANES)
        aq_ref[:, sl] = (rope(q[:, sl]) * (ATT_HEAD_DIM ** -0.5)).astype(BF16)
    ak_ref[...] = rope(jnp.dot(hb, wk_ref[...], preferred_element_type=F32)).astype(BF16)
    avt_ref[...] = jnp.dot(hb, wv_ref[...], preferred_element_type=F32).astype(BF16).T
    dqkv_ref[...] = jnp.dot(hb, wd_ref[...], preferred_element_type=F32)
    dz_ref[...] = jnp.dot(hb, wz_ref[...], preferred_element_type=F32)

    raw = jnp.dot(hb, wgc_ref[...], preferred_element_type=F32)
    lane = _iota(raw.shape, 1)
    gval = -jnp.exp(pc_ref[0:1, :]) * _softplus(raw + pc_ref[1:2, :])
    gates = jnp.where(lane < 2 * DN_HEADS, jax.nn.sigmoid(raw), gval)
    pos = _iota(raw.shape, 0) & (CHUNK - 1)
    cf = gates
    cb = gates
    s = 1
    while s < CHUNK:
        cf = cf + jnp.where(pos >= s, pltpu.roll(cf, s, 0), 0.0)
        cb = cb + jnp.where(pos < CHUNK - s, pltpu.roll(cb, TM - s, 0), 0.0)
        s *= 2
    gcol = jnp.where(lane < 2 * DN_HEADS, gates, jnp.where(lane < 3 * DN_HEADS, cf, cb))
    gcol_ref[...] = gcol
    grow_ref[...] = gcol.T[:N_GATES]


def _in_proj(x2d, T, g, wq, wk, wv, wd, wz, wgc, cos, sa, sb, pc):
    BT = x2d.shape[0]
    n_t = T // TM
    row = lambda i: (i, 0)
    const = lambda i: (0, 0)
    tab = lambda i: (i % n_t, 0)
    full = lambda a: pl.BlockSpec(a.shape, const)
    return pl.pallas_call(
        _in_proj_kernel,
        grid=(BT // TM,),
        in_specs=[pl.BlockSpec((TM, D_MODEL), row), full(g), full(wq), full(wk), full(wv), full(wd), full(wz),
                  full(wgc),
                  pl.BlockSpec((TM, LANES), tab), pl.BlockSpec((TM, LANES), tab), pl.BlockSpec((TM, LANES), tab),
                  full(pc)],
        out_specs=[pl.BlockSpec((TM, ATT_Q), row), pl.BlockSpec((TM, ATT_KV), row),
                   pl.BlockSpec((ATT_KV, TM), lambda i: (0, i)),
                   pl.BlockSpec((TM, 3 * DN_W), row), pl.BlockSpec((TM, DN_W), row),
                   pl.BlockSpec((TM, LANES), row), pl.BlockSpec((N_GATES, TM), lambda i: (0, i))],
        out_shape=[jax.ShapeDtypeStruct((BT, ATT_Q), BF16), jax.ShapeDtypeStruct((BT, ATT_KV), BF16),
                   jax.ShapeDtypeStruct((ATT_KV, BT), BF16),
                   jax.ShapeDtypeStruct((BT, 3 * DN_W), F32), jax.ShapeDtypeStruct((BT, DN_W), F32),
                   jax.ShapeDtypeStruct((BT, LANES), F32), jax.ShapeDtypeStruct((N_GATES, BT), F32)],
        compiler_params=_params("parallel"),
        name="in_proj",
    )(x2d, g, wq, wk, wv, wd, wz, wgc, cos, sa, sb, pc)


def _win_attn_kernel(sink_ref, q_ref, k_ref, vt_ref, o_ref, *, T):
    W = TQ + 2 * ATT_WINDOW
    hpg = ATT_HEADS // ATT_KV_HEADS
    items = []
    for sub in range(TQ_STEP // TQ):
        t0 = pl.program_id(1) * TQ_STEP + sub * TQ
        start = pl.multiple_of(jnp.clip(t0 - ATT_WINDOW, 0, T - W), LANES)
        k = k_ref[pl.ds(start, W), :]
        vt = vt_ref[:, pl.ds(start, W)]
        k_lo = _iota(k.shape, 1) < ATT_HEAD_DIM
        v_lo = _iota(vt.shape, 0) < ATT_HEAD_DIM
        kk = jnp.concatenate([jnp.where(k_lo, k, jnp.zeros_like(k)), jnp.where(k_lo, jnp.zeros_like(k), k)], axis=0)
        vvt = jnp.concatenate([jnp.where(v_lo, vt, jnp.zeros_like(vt)), jnp.where(v_lo, jnp.zeros_like(vt), vt)],
                              axis=1)
        kpos = start + _iota((W, 2 * TQ), 0)
        qpos = t0 + (_iota((W, 2 * TQ), 1) & (TQ - 1))
        cap = jnp.where(jnp.abs(qpos - kpos) <= ATT_WINDOW, jnp.inf, -1e30)
        rows = slice(sub * TQ, (sub + 1) * TQ)
        for pair in range(ATT_Q // LANES // 2):
            groups = (2 * pair, 2 * pair + 1)
            qp = jnp.concatenate([q_ref[rows, g * LANES:(g + 1) * LANES] for g in groups], axis=0)
            items.append(dict(kk=kk, vvt=vvt, cap=cap, qp=qp, rows=rows, groups=groups))
    for it in items:
        it["st"] = lax.dot_general(it["kk"], it["qp"], NT_DIMS, preferred_element_type=F32)
    for it in items:
        ps, dens = [], []
        for kvh in range(ATT_KV_HEADS):
            sh = jnp.minimum(it["st"][kvh * W:(kvh + 1) * W], it["cap"])
            sk = jnp.concatenate([jnp.full((1, TQ), sink_ref[kvh * hpg + g], F32) for g in it["groups"]], axis=1)
            m = jnp.maximum(jnp.max(sh, axis=0, keepdims=True), sk)
            p = jnp.exp(sh - m)
            dens.append(jnp.sum(p, axis=0, keepdims=True) + jnp.exp(sk - m))
            ps.append(p.astype(BF16))
        it["p"] = jnp.concatenate(ps, axis=0)
        it["dens"] = dens
    for it in items:
        ot = jnp.dot(it["vvt"], it["p"], preferred_element_type=F32)
        ot = ot / jnp.where(_iota(ot.shape, 0) < ATT_HEAD_DIM, it["dens"][0], it["dens"][1])
        for i, g in enumerate(it["groups"]):
            o_ref[g * LANES:(g + 1) * LANES, it["rows"]] = ot[:, i * TQ:(i + 1) * TQ].astype(BF16)


def _win_attn(aq, ak, avt, sink, B, T):
    n_q = T // TQ_STEP
    return pl.pallas_call(
        functools.partial(_win_attn_kernel, T=T),
        grid=(B, n_q),
        in_specs=[pl.BlockSpec(memory_space=pltpu.SMEM),
                  pl.BlockSpec((TQ_STEP, ATT_Q), lambda b, i: (b * n_q + i, 0)),
                  pl.BlockSpec((T, ATT_KV), lambda b, i: (b, 0)),
                  pl.BlockSpec((ATT_KV, T), lambda b, i: (0, b))],
        out_specs=pl.BlockSpec((ATT_Q, TQ_STEP), lambda b, i: (0, b * n_q + i)),
        out_shape=jax.ShapeDtypeStruct((ATT_Q, B * T), BF16),
        compiler_params=_params("parallel", "arbitrary"),
        name="win_attn",
    )(sink, aq, ak, avt)


def _pair_blockdiag(y):
    z = jnp.zeros((CHUNK, CHUNK), y.dtype)
    top = jnp.concatenate([y[:, :CHUNK], z], axis=1)
    bot = jnp.concatenate([z, y[:, CHUNK:]], axis=1)
    return jnp.concatenate([top, bot], axis=0)


def _mm_pair(xb, yb):
    return jnp.dot(xb, _pair_blockdiag(yb), preferred_element_type=F32)


def _unit_tri_inverse_pairs(l2s):
    bf = lambda xs: [x.astype(BF16) for x in xs]
    r = _iota((CHUNK, 2 * CHUNK), 0)
    c = _iota((CHUNK, 2 * CHUNK), 1) & (CHUNK - 1)
    base = 16
    in_base = (r // base) == (c // base)
    ds = [jnp.where(in_base, l2, 0.0) for l2 in l2s]
    ps = [jnp.where(r == c, 1.0, 0.0) - d for d in ds]
    powb = bf(ds)
    for _ in range(3):
        powb = bf([_mm_pair(x, x) for x in powb])
        ps = [p + _mm_pair(pb, x) for p, pb, x in zip(ps, bf(ps), powb)]
    s = base
    while s < CHUNK:
        off_diag = ((r // (2 * s)) == (c // (2 * s))) & ((r // s) != (c // s))
        mbs = bf([jnp.where(off_diag, l2, 0.0) for l2 in l2s])
        pbs = bf(ps)
        tbs = bf([_mm_pair(mb, pb) for mb, pb in zip(mbs, pbs)])
        ps = [p - _mm_pair(pb, tb) for p, pb, tb in zip(ps, pbs, tbs)]
        s *= 2
    return ps


def _deltanet_kernel(q_ref, k_ref, v_ref, z_ref, gcol_ref, grow_ref, cwq_ref, cwk_ref, cwv_ref, nw_ref,
                     o_ref, pad_scr, u_scr, w_scr, qg_scr, at_scr, gl_scr, ku_scr, kw_scr, sp_scr, *, T):
    C = CHUNK
    N = T // C
    U = PREP_UNROLL
    h = pl.program_id(1)
    ri = _iota((C, C), 0)
    ci = _iota((C, C), 1)
    incl = (ri >= ci, ri <= ci)
    strict = (ri > ci, ri < ci)
    lane = _iota((C, LANES), 1)

    zpad = jnp.zeros((PAD, DN_HEAD_DIM), F32)
    for a, ref in enumerate((q_ref, k_ref, v_ref)):
        pad_scr[a, 0:PAD, :] = zpad
        pad_scr[a, T + PAD:T + 2 * PAD, :] = zpad

        def copy_rows(i, carry, a=a, ref=ref):
            r0 = pl.multiple_of(i * COPY_ROWS, COPY_ROWS)
            pad_scr[a, pl.ds(pl.multiple_of(r0 + PAD, PAD), COPY_ROWS), :] = ref[pl.ds(r0, COPY_ROWS), :]
            return carry

        lax.fori_loop(0, T // COPY_ROWS, copy_rows, 0)

    def conv_silu(a, cw_ref, c0):
        win = pad_scr.at[a, pl.ds(c0, C + 2 * PAD), :]
        off = PAD - DN_CONV // 2
        acc = win[off:off + C, :] * cw_ref[0:1, :]
        for j in range(1, DN_CONV):
            acc = acc + win[off + j:off + j + C, :] * cw_ref[j:j + 1, :]
        return _silu(acc)

    def l2norm(t):
        return t * lax.rsqrt(jnp.sum(t * t, axis=-1, keepdims=True) + EPS)

    def prepare_group(g, carry):
        ns = [g * U + u for u in range(U)]
        c0s = [pl.multiple_of(n * C, C) for n in ns]
        qs = [l2norm(conv_silu(0, cwq_ref, c0)) * (DN_HEAD_DIM ** -0.5) for c0 in c0s]
        ks = [l2norm(conv_silu(1, cwk_ref, c0)) for c0 in c0s]
        vs = [conv_silu(2, cwv_ref, c0) for c0 in c0s]
        kbs = [k.astype(BF16) for k in ks]
        grams = [lax.dot_general(jnp.concatenate([q.astype(BF16), kb], axis=0), kb, NT_DIMS,
                                 preferred_element_type=F32) for q, kb in zip(qs, kbs)]
        l2s, rhss, kds = [], [], []
        for u in range(U):
            rows = pl.ds(c0s[u], C)
            q, k, v = qs[u], ks[u], vs[u]
            gc = gcol_ref[rows, :]
            col = lambda idx: jnp.sum(jnp.where(lane == idx, gc, 0.0), axis=1, keepdims=True)
            beta = (col(h), col(DN_HEADS + h))
            gcc = (col(2 * DN_HEADS + h), col(3 * DN_HEADS + h))
            gr = grow_ref[2 * DN_HEADS:, rows]
            sub = _iota(gr.shape, 0)
            row = lambda idx: jnp.sum(jnp.where(sub == idx, gr, 0.0), axis=0, keepdims=True)
            gcr = (row(h), row(DN_HEADS + h))
            glast = (gcr[0][:, C - 1:C], gcr[1][:, 0:1])
            qk = grams[u][:C]
            kk = grams[u][C:]
            l2 = []
            for d in range(2):
                decay = jnp.where(incl[d], jnp.exp(jnp.where(incl[d], gcc[d] - gcr[d], 0.0)), 0.0)
                l2.append(jnp.where(strict[d], beta[d] * kk * decay, 0.0))
                eg = jnp.exp(gcc[d])
                qg_scr[d, rows, :] = (q * eg).astype(BF16)
                kds.append((k * jnp.exp(glast[d] - gcc[d])).astype(BF16))
                at_scr[d, rows, :] = (qk * decay).astype(BF16)
                gl_scr[d, ns[u]] = jnp.broadcast_to(jnp.exp(glast[d]), (8, LANES))
                rhss.append(jnp.concatenate([v * beta[d], (k * beta[d]) * eg], axis=1).astype(BF16))
            l2s.append(jnp.concatenate(l2, axis=1))
        ainvs = _unit_tri_inverse_pairs(l2s)
        for u in range(U):
            rows = pl.ds(c0s[u], C)
            for d in range(2):
                uw = jnp.dot(ainvs[u][:, d * C:(d + 1) * C].astype(BF16), rhss[2 * u + d],
                             preferred_element_type=F32)
                uwb = uw.astype(BF16)
                u_scr[d, rows, :] = uw[:, :DN_HEAD_DIM]
                w_scr[d, rows, :] = uwb[:, DN_HEAD_DIM:]
                kuw = lax.dot_general(kds[2 * u + d], uwb, TN_DIMS, preferred_element_type=F32)
                ku_scr[d, rows, :] = kuw[:, :DN_HEAD_DIM]
                kw_scr[d, rows, :] = kuw[:, DN_HEAD_DIM:].astype(BF16)
        return carry

    lax.fori_loop(0, N // U, prepare_group, 0)

    def scan_step(d, n, state):
        rows = pl.ds(pl.multiple_of(n * C, C), C)
        sb = state.astype(BF16)
        sp_scr[d, rows, :] = sb
        return (state * gl_scr[d, n][0:1, :] + ku_scr[d, rows, :]
                - jnp.dot(kw_scr[d, rows, :], sb, preferred_element_type=F32))

    def scan(i, states):
        return scan_step(0, i, states[0]), scan_step(1, N - 1 - i, states[1])

    s0 = jnp.zeros((DN_HEAD_DIM, DN_HEAD_DIM), F32)
    lax.fori_loop(0, N, scan, (s0, s0))

    def outputs_group(g, carry):
        rows = [pl.ds(pl.multiple_of((g * U + u) * C, C), C) for u in range(U)]
        items = [(r, d) for r in rows for d in range(2)]
        sbs = [sp_scr[d, r, :] for r, d in items]
        ws_qs = [jnp.dot(jnp.concatenate([w_scr[d, r, :], qg_scr[d, r, :]], axis=0), sb,
                         preferred_element_type=F32) for (r, d), sb in zip(items, sbs)]
        v_news = [(u_scr[d, r, :] - x[:C]).astype(BF16) for (r, d), x in zip(items, ws_qs)]
        os_ = [x[C:] + jnp.dot(at_scr[d, r, :], vn, preferred_element_type=F32)
               for (r, d), x, vn in zip(items, ws_qs, v_news)]
        for u, r in enumerate(rows):
            o = os_[2 * u] + os_[2 * u + 1]
            o = o * lax.rsqrt(jnp.mean(o * o, axis=-1, keepdims=True) + EPS) * nw_ref[...]
            o_ref[r, :] = (o * _silu(z_ref[r, :])).astype(BF16)
        return carry

    lax.fori_loop(0, N // U, outputs_group, 0)


def _deltanet(dqkv, dz, gcol, grow, conv_w, norm_w, B, T):
    H = DN_HEADS
    N = T // CHUNK
    seq = lambda off: pl.BlockSpec((T, DN_HEAD_DIM), lambda b, h: (b, off + h))
    cw = lambda off: pl.BlockSpec((DN_CONV, DN_HEAD_DIM), lambda b, h: (0, off + h))
    return pl.pallas_call(
        functools.partial(_deltanet_kernel, T=T),
        grid=(B, H),
        in_specs=[seq(0), seq(H), seq(2 * H), seq(0),
                  pl.BlockSpec((T, LANES), lambda b, h: (b, 0)),
                  pl.BlockSpec((N_GATES, T), lambda b, h: (0, b)),
                  cw(0), cw(H), cw(2 * H),
                  pl.BlockSpec((1, DN_HEAD_DIM), lambda b, h: (0, 0))],
        out_specs=seq(0),
        out_shape=jax.ShapeDtypeStruct((B * T, DN_W), BF16),
        scratch_shapes=[pltpu.VMEM((3, T + 2 * PAD, DN_HEAD_DIM), F32),
                        pltpu.VMEM((2, T, DN_HEAD_DIM), F32),
                        pltpu.VMEM((2, T, DN_HEAD_DIM), BF16),
                        pltpu.VMEM((2, T, DN_HEAD_DIM), BF16),
                        pltpu.VMEM((2, T, CHUNK), BF16),
                        pltpu.VMEM((2, N, 8, LANES), F32),
                        pltpu.VMEM((2, T, DN_HEAD_DIM), F32),
                        pltpu.VMEM((2, T, DN_HEAD_DIM), BF16),
                        pltpu.VMEM((2, T, DN_HEAD_DIM), BF16)],
        compiler_params=_params("parallel", "arbitrary"),
        name="deltanet",
    )(dqkv, dqkv, dqkv, dz, gcol, grow, conv_w, conv_w, conv_w, norm_w)


def _out_proj_kernel(att_t_ref, dn_ref, x_ref, woa_ref, wod_ref, g1_ref, g2_ref, wq_ref, x1_ref, qx_ref):
    subs = [slice(i * TM, (i + 1) * TM) for i in range(SUB_TILES)]
    mixes = [lax.dot_general(att_t_ref[:, r], woa_ref[...], TN_DIMS, preferred_element_type=F32)
             + jnp.dot(dn_ref[r, :], wod_ref[...], preferred_element_type=F32) for r in subs]
    x1s = [x_ref[r, :] + _rms(mix, g1_ref[...]) for r, mix in zip(subs, mixes)]
    hs = [_rms(x1, g2_ref[...]).astype(BF16) for x1 in x1s]
    for r, x1, hb in zip(subs, x1s, hs):
        x1_ref[r, :] = x1
        qx_ref[r, :] = jnp.dot(hb, wq_ref[...], preferred_element_type=F32).astype(BF16)


def _out_proj(att_t, dn, x2d, woa, wod, g1, g2, wq):
    BT = x2d.shape[0]
    row = lambda i: (i, 0)
    full = lambda a: pl.BlockSpec(a.shape, lambda i: (0, 0))
    return pl.pallas_call(
        _out_proj_kernel,
        grid=(BT // TS,),
        in_specs=[pl.BlockSpec((ATT_Q, TS), lambda i: (0, i)), pl.BlockSpec((TS, DN_W), row),
                  pl.BlockSpec((TS, D_MODEL), row),
                  full(woa), full(wod), full(g1), full(g2), full(wq)],
        out_specs=[pl.BlockSpec((TS, D_MODEL), row), pl.BlockSpec((TS, X_W), row)],
        out_shape=[jax.ShapeDtypeStruct((BT, D_MODEL), F32), jax.ShapeDtypeStruct((BT, X_W), BF16)],
        compiler_params=_params("parallel"),
        name="out_proj",
    )(att_t, dn, x2d, woa, wod, g1, g2, wq)


def _mem_kv_kernel(mem_ref, g_ref, w_ref, kv_ref):
    kv_ref[...] = _mm(_rms(mem_ref[...], g_ref[...]), w_ref[...]).astype(BF16)


def _mem_kv(mem2d, g, wkv):
    M = mem2d.shape[0]
    mem_len = 256
    return pl.pallas_call(
        _mem_kv_kernel,
        grid=(M // mem_len,),
        in_specs=[pl.BlockSpec((mem_len, D_MODEL), lambda i: (i, 0)),
                  pl.BlockSpec(g.shape, lambda i: (0, 0)), pl.BlockSpec(wkv.shape, lambda i: (0, 0))],
        out_specs=pl.BlockSpec((mem_len, 2 * X_W), lambda i: (i, 0)),
        out_shape=jax.ShapeDtypeStruct((M, 2 * X_W), BF16),
        compiler_params=_params("parallel"),
        name="mem_kv",
    )(mem2d, g, wkv)


def _xattn_kernel(qx_ref, kv_ref, x1_ref, wo_ref, g_ref, x2_ref):
    subs = [slice(i * TM, (i + 1) * TM) for i in range(SUB_TILES)]
    items = [(r, hd) for r in subs for hd in range(X_HEADS)]
    ksl = lambda hd: slice(hd * X_HEAD_DIM, (hd + 1) * X_HEAD_DIM)
    vsl = lambda hd: slice(X_W + hd * X_HEAD_DIM, X_W + (hd + 1) * X_HEAD_DIM)
    ss = [lax.dot_general(qx_ref[r, ksl(hd)], kv_ref[:, ksl(hd)], NT_DIMS, preferred_element_type=F32)
          * (X_HEAD_DIM ** -0.5) for r, hd in items]
    ps = [jnp.exp(s - jnp.max(s, axis=-1, keepdims=True)) for s in ss]
    dens = [jnp.sum(p, axis=-1, keepdims=True) for p in ps]
    os_ = [jnp.dot(p.astype(BF16), kv_ref[:, vsl(hd)], preferred_element_type=F32) / den
           for (r, hd), p, den in zip(items, ps, dens)]
    for i, r in enumerate(subs):
        o = jnp.concatenate(os_[i * X_HEADS:(i + 1) * X_HEADS], axis=1)
        x2_ref[r, :] = x1_ref[r, :] + _rms(_mm(o, wo_ref[...]), g_ref[...])


def _xattn(qx, kv, x1, wo, g, B, T):
    n_t = T // TS
    mem_len = kv.shape[0] // B
    row = lambda b, i: (b * n_t + i, 0)
    return pl.pallas_call(
        _xattn_kernel,
        grid=(B, n_t),
        in_specs=[pl.BlockSpec((TS, X_W), row), pl.BlockSpec((mem_len, 2 * X_W), lambda b, i: (b, 0)),
                  pl.BlockSpec((TS, D_MODEL), row),
                  pl.BlockSpec(wo.shape, lambda b, i: (0, 0)), pl.BlockSpec(g.shape, lambda b, i: (0, 0))],
        out_specs=pl.BlockSpec((TS, D_MODEL), row),
        out_shape=jax.ShapeDtypeStruct((B * T, D_MODEL), F32),
        compiler_params=_params("parallel", "arbitrary"),
        name="xattn",
    )(qx, kv, x1, wo, g)


def _mlp_kernel(x_ref, g1_ref, w1_ref, w2_ref, g2_ref, y_ref):
    x = x_ref[...]
    hb = _rms(x, g1_ref[...]).astype(BF16)
    acc = jnp.zeros((TM, D_MODEL), F32)
    for c in range(D_FF // FF_BLOCK):
        sl = slice(c * FF_BLOCK, (c + 1) * FF_BLOCK)
        a = jnp.maximum(jnp.dot(hb, w1_ref[:, sl], preferred_element_type=F32), 0.0)
        acc = acc + jnp.dot((a * a).astype(BF16), w2_ref[sl, :], preferred_element_type=F32)
    y_ref[...] = x + _rms(acc, g2_ref[...])


def _mlp(x2, g1, w1, w2, g2):
    BT = x2.shape[0]
    row = lambda i: (i, 0)
    full = lambda a: pl.BlockSpec(a.shape, lambda i: (0, 0))
    return pl.pallas_call(
        _mlp_kernel,
        grid=(BT // TM,),
        in_specs=[pl.BlockSpec((TM, D_MODEL), row), full(g1), full(w1), full(w2), full(g2)],
        out_specs=pl.BlockSpec((TM, D_MODEL), row),
        out_shape=jax.ShapeDtypeStruct((BT, D_MODEL), F32),
        compiler_params=_params("parallel"),
        name="mlp",
    )(x2, g1, w1, w2, g2)


def _rope_tables(T):
    half = ROPE_DIM // 2
    inv = ROPE_THETA ** (-(jnp.arange(half, dtype=F32) * 2.0 / ROPE_DIM))
    ang = jnp.arange(T).astype(F32)[:, None] * inv[None, :]
    cos, sin = jnp.cos(ang), jnp.sin(ang)
    pad = ATT_HEAD_DIM - ROPE_DIM
    one_head = lambda a, b, fill: jnp.concatenate([a, b, jnp.full((T, pad), fill, F32)], axis=1)
    zeros = jnp.zeros((T, half), F32)
    two_heads = lambda t: jnp.concatenate([t, t], axis=1)
    return (two_heads(one_head(cos, cos, 1.0)), two_heads(one_head(-sin, zeros, 0.0)),
            two_heads(one_head(zeros, sin, 0.0)))


def _gate_params(a_log_f, a_log_b, dt_f, dt_b):
    zeros = jnp.zeros((2 * DN_HEADS,), F32)
    a_vec = jnp.concatenate([zeros, a_log_f.astype(F32), a_log_b.astype(F32)])
    d_vec = jnp.concatenate([zeros, dt_f.astype(F32), dt_b.astype(F32)])
    pad = jnp.zeros((LANES - N_GATES,), F32)
    return jnp.stack([jnp.concatenate([a_vec, pad]), jnp.concatenate([d_vec, pad])])


def _layer(x, mem, w_in, attn_sink, conv_w, a_log_f, a_log_b, dt_f, dt_b, dn_norm_w, w_out,
           xa_wq, xa_wkv, xa_wo, mem_g, w1, w2, g_pre_mix, g_post_mix, g_pre_xa, g_post_xa,
           g_pre_mlp, g_post_mlp):
    B, T, _ = x.shape
    assert T % TS == 0 and T % (PREP_UNROLL * CHUNK) == 0 and T >= TQ + 2 * ATT_WINDOW
    x2d = x.reshape(B * T, D_MODEL)
    vec = lambda g: g.astype(F32).reshape(1, -1)
    hpg = ATT_HEADS // ATT_KV_HEADS
    head_order = [kvh * hpg + j for j in range(hpg) for kvh in range(ATT_KV_HEADS)]
    q_cols = jnp.concatenate([jnp.arange(hd * ATT_HEAD_DIM, (hd + 1) * ATT_HEAD_DIM) for hd in head_order])
    wq = w_in[:, q_cols].astype(BF16)
    wk = w_in[:, O_AK:O_AV].astype(BF16)
    wv = w_in[:, O_AV:O_DQKV].astype(BF16)
    wd = w_in[:, O_DQKV:O_DZ].astype(BF16)
    wz = w_in[:, O_DZ:O_GATES].astype(BF16)
    wg = w_in[:, O_GATES:O_GATES + N_GATES]
    wgc = jnp.pad(wg, ((0, 0), (0, LANES - N_GATES))).astype(BF16)
    cos, sa, sb = _rope_tables(T)
    pc = _gate_params(a_log_f, a_log_b, dt_f, dt_b)

    aq, ak, avt, dqkv, dz, gcol, grow = _in_proj(x2d, T, vec(g_pre_mix), wq, wk, wv, wd, wz, wgc,
                                                 cos, sa, sb, pc)
    att_t = _win_attn(aq, ak, avt, attn_sink.astype(F32), B, T)
    dn = _deltanet(dqkv, dz, gcol, grow, conv_w.astype(F32), vec(dn_norm_w), B, T)
    x1, qx = _out_proj(att_t, dn, x2d, w_out[q_cols].astype(BF16), w_out[ATT_Q:].astype(BF16),
                       vec(g_post_mix), vec(g_pre_xa), xa_wq.astype(BF16))
    kv = _mem_kv(mem.reshape(-1, D_MODEL), vec(mem_g), xa_wkv.astype(BF16))
    x2 = _xattn(qx, kv, x1, xa_wo.astype(BF16), vec(g_post_xa), B, T)
    y = _mlp(x2, vec(g_pre_mlp), w1.astype(BF16), w2.astype(BF16), vec(g_post_mlp))
    return y.reshape(B, T, D_MODEL)


def kernel(x_prompt, x_sample, mem_prompt, mem_sample, w_in, attn_sink, dn_conv_w, dn_A_log_f, dn_A_log_b,
           dn_dt_bias_f, dn_dt_bias_b, dn_norm_w, w_out, xa_wq, xa_wkv, xa_wo, mem_norm_g, mlp_w1, mlp_w2,
           norm_pre_mix, norm_post_mix, norm_pre_xa, norm_post_xa, norm_pre_mlp, norm_post_mlp):
    weights = (w_in, attn_sink, dn_conv_w, dn_A_log_f, dn_A_log_b, dn_dt_bias_f, dn_dt_bias_b, dn_norm_w,
               w_out, xa_wq, xa_wkv, xa_wo, mem_norm_g, mlp_w1, mlp_w2, norm_pre_mix, norm_post_mix,
               norm_pre_xa, norm_post_xa, norm_pre_mlp, norm_post_mlp)
    outs = []
    for x, mem in ((x_prompt, mem_prompt), (x_sample, mem_sample)):
        for l in range(w_in.shape[0]):
            x = _layer(x, mem, *(w[l] for w in weights))
        outs.append(x)
    return tuple(outs)
```

```python
import functools

import jax
import jax.numpy as jnp
from jax import lax
from jax.experimental import pallas as pl
from jax.experimental.pallas import tpu as pltpu

F32 = jnp.float32
BF16 = jnp.bfloat16

D_MODEL = 1024
ATT_HEADS = 8
ATT_KV_HEADS = 2
ATT_HEAD_DIM = 64
ATT_WINDOW = 128
ROPE_THETA = 500000.0
ROPE_DIM = ATT_HEAD_DIM // 4
DN_HEADS = 4
DN_HEAD_DIM = 128
DN_CONV = 5
X_HEADS = 4
X_HEAD_DIM = 128
D_FF = 4 * D_MODEL
EPS = 1e-6

ATT_Q = ATT_HEADS * ATT_HEAD_DIM
ATT_KV = ATT_KV_HEADS * ATT_HEAD_DIM
DN_W = DN_HEADS * DN_HEAD_DIM
X_W = X_HEADS * X_HEAD_DIM
O_AK = ATT_Q
O_AV = O_AK + ATT_KV
O_DQKV = O_AV + ATT_KV
O_DZ = O_DQKV + 3 * DN_W
O_GATES = O_DZ + DN_W
N_GATES = 4 * DN_HEADS

LANES = 128
TM = 512
TQ = 128
TQ_STEP = 256
CHUNK = 128
PREP_UNROLL = 8
PAD = 8
COPY_ROWS = 512
SUB_TILES = 2
TS = SUB_TILES * TM
FF_BLOCK = 1024
VMEM_LIMIT = 56 * 1024 * 1024

NT_DIMS = (((1,), (1,)), ((), ()))
TN_DIMS = (((0,), (0,)), ((), ()))


def _mm(a, b):
    return jnp.dot(a.astype(BF16), b.astype(BF16), preferred_element_type=F32)


def _mm_nt(a, b):
    return lax.dot_general(a.astype(BF16), b.astype(BF16), NT_DIMS, preferred_element_type=F32)


def _mm_tn(a, b):
    return lax.dot_general(a.astype(BF16), b.astype(BF16), TN_DIMS, preferred_element_type=F32)


def _rms(x, g):
    return x * lax.rsqrt(jnp.mean(x * x, axis=-1, keepdims=True) + EPS) * g


def _softplus(x):
    return jnp.maximum(x, 0.0) + jnp.log1p(jnp.exp(-jnp.abs(x)))


def _silu(x):
    return x * jax.nn.sigmoid(x)


def _iota(shape, dim):
    return lax.broadcasted_iota(jnp.int32, shape, dim)


def _params(*sem):
    return pltpu.CompilerParams(dimension_semantics=sem, vmem_limit_bytes=VMEM_LIMIT)


def _in_proj_kernel(x_ref, g_ref, wq_ref, wk_ref, wv_ref, wd_ref, wz_ref, wgc_ref,
                    cos_ref, sa_ref, sb_ref, pc_ref,
                    aq_ref, ak_ref, avt_ref, dqkv_ref, dz_ref, gcol_ref, grow_ref):
    subs = [slice(i * TM, (i + 1) * TM) for i in range(SUB_TILES)]
    hbs = [_rms(x_ref[r, :], g_ref[...]).astype(BF16) for r in subs]

    def rope(t, r):
        return (t * cos_ref[r, :] + pltpu.roll(t, LANES - ROPE_DIM // 2, 1) * sa_ref[r, :]
                + pltpu.roll(t, ROPE_DIM // 2, 1) * sb_ref[r, :])

    for r, hb in zip(subs, hbs):
        q = jnp.dot(hb, wq_ref[...], preferred_element_type=F32)
        for j in range(ATT_Q // LANES):
            sl = slice(j * LANES, (j + 1) * LANES)
            aq_ref[r, sl] = (rope(q[:, sl], r) * (ATT_HEAD_DIM ** -0.5)).astype(BF16)
        ak_ref[r, :] = rope(jnp.dot(hb, wk_ref[...], preferred_element_type=F32), r).astype(BF16)
        avt_ref[:, r] = jnp.dot(hb, wv_ref[...], preferred_element_type=F32).astype(BF16).T
        dqkv_ref[r, :] = jnp.dot(hb, wd_ref[...], preferred_element_type=F32)
        dz_ref[r, :] = jnp.dot(hb, wz_ref[...], preferred_element_type=F32)

        raw = jnp.dot(hb, wgc_ref[...], preferred_element_type=F32)
        lane = _iota(raw.shape, 1)
        gval = -jnp.exp(pc_ref[0:1, :]) * _softplus(raw + pc_ref[1:2, :])
        gates = jnp.where(lane < 2 * DN_HEADS, jax.nn.sigmoid(raw), gval)
        pos = _iota(raw.shape, 0) & (CHUNK - 1)
        cf = gates
        cb = gates
        s = 1
        while s < CHUNK:
            cf = cf + jnp.where(pos >= s, pltpu.roll(cf, s, 0), 0.0)
            cb = cb + jnp.where(pos < CHUNK - s, pltpu.roll(cb, TM - s, 0), 0.0)
            s *= 2
        gcol = jnp.where(lane < 2 * DN_HEADS, gates, jnp.where(lane < 3 * DN_HEADS, cf, cb))
        gcol_ref[r, :] = gcol
        grow_ref[:, r] = gcol.T[:N_GATES]


def _in_proj(x2d, T, g, wq, wk, wv, wd, wz, wgc, cos, sa, sb, pc):
    BT = x2d.shape[0]
    n_t = T // TS
    row = lambda i: (i, 0)
    const = lambda i: (0, 0)
    tab = lambda i: (i % n_t, 0)
    full = lambda a: pl.BlockSpec(a.shape, const)
    return pl.pallas_call(
        _in_proj_kernel,
        grid=(BT // TS,),
        in_specs=[pl.BlockSpec((TS, D_MODEL), row), full(g), full(wq), full(wk), full(wv), full(wd), full(wz),
                  full(wgc),
                  pl.BlockSpec((TS, LANES), tab), pl.BlockSpec((TS, LANES), tab), pl.BlockSpec((TS, LANES), tab),
                  full(pc)],
        out_specs=[pl.BlockSpec((TS, ATT_Q), row), pl.BlockSpec((TS, ATT_KV), row),
                   pl.BlockSpec((ATT_KV, TS), lambda i: (0, i)),
                   pl.BlockSpec((TS, 3 * DN_W), row), pl.BlockSpec((TS, DN_W), row),
                   pl.BlockSpec((TS, LANES), row), pl.BlockSpec((N_GATES, TS), lambda i: (0, i))],
        out_shape=[jax.ShapeDtypeStruct((BT, ATT_Q), BF16), jax.ShapeDtypeStruct((BT, ATT_KV), BF16),
                   jax.ShapeDtypeStruct((ATT_KV, BT), BF16),
                   jax.ShapeDtypeStruct((BT, 3 * DN_W), F32), jax.ShapeDtypeStruct((BT, DN_W), F32),
                   jax.ShapeDtypeStruct((BT, LANES), F32), jax.ShapeDtypeStruct((N_GATES, BT), F32)],
        compiler_params=_params("parallel"),
        name="in_proj",
    )(x2d, g, wq, wk, wv, wd, wz, wgc, cos, sa, sb, pc)


def _win_attn_kernel(sink_ref, q_ref, k_ref, vt_ref, o_ref, *, T):
    W = TQ + 2 * ATT_WINDOW
    hpg = ATT_HEADS // ATT_KV_HEADS
    items = []
    for sub in range(TQ_STEP // TQ):
        t0 = pl.program_id(1) * TQ_STEP + sub * TQ
        start = pl.multiple_of(jnp.clip(t0 - ATT_WINDOW, 0, T - W), LANES)
        k = k_ref[pl.ds(start, W), :]
        vt = vt_ref[:, pl.ds(start, W)]
        k_lo = _iota(k.shape, 1) < ATT_HEAD_DIM
        v_lo = _iota(vt.shape, 0) < ATT_HEAD_DIM
        kk = jnp.concatenate([jnp.where(k_lo, k, jnp.zeros_like(k)), jnp.where(k_lo, jnp.zeros_like(k), k)], axis=0)
        vvt = jnp.concatenate([jnp.where(v_lo, vt, jnp.zeros_like(vt)), jnp.where(v_lo, jnp.zeros_like(vt), vt)],
                              axis=1)
        kpos = start + _iota((W, 2 * TQ), 0)
        qpos = t0 + (_iota((W, 2 * TQ), 1) & (TQ - 1))
        cap = jnp.where(jnp.abs(qpos - kpos) <= ATT_WINDOW, jnp.inf, -1e30)
        rows = slice(sub * TQ, (sub + 1) * TQ)
        for pair in range(ATT_Q // LANES // 2):
            groups = (2 * pair, 2 * pair + 1)
            qp = jnp.concatenate([q_ref[rows, g * LANES:(g + 1) * LANES] for g in groups], axis=0)
            items.append(dict(kk=kk, vvt=vvt, cap=cap, qp=qp, rows=rows, groups=groups))
    for it in items:
        it["st"] = lax.dot_general(it["kk"], it["qp"], NT_DIMS, preferred_element_type=F32)
    for it in items:
        ps, dens = [], []
        for kvh in range(ATT_KV_HEADS):
            sh = jnp.minimum(it["st"][kvh * W:(kvh + 1) * W], it["cap"])
            sk = jnp.concatenate([jnp.full((1, TQ), sink_ref[kvh * hpg + g], F32) for g in it["groups"]], axis=1)
            m = jnp.maximum(jnp.max(sh, axis=0, keepdims=True), sk)
            p = jnp.exp(sh - m)
            dens.append(jnp.sum(p, axis=0, keepdims=True) + jnp.exp(sk - m))
            ps.append(p.astype(BF16))
        it["p"] = jnp.concatenate(ps, axis=0)
        it["dens"] = dens
    for it in items:
        ot = jnp.dot(it["vvt"], it["p"], preferred_element_type=F32)
        ot = ot / jnp.where(_iota(ot.shape, 0) < ATT_HEAD_DIM, it["dens"][0], it["dens"][1])
        for i, g in enumerate(it["groups"]):
            o_ref[g * LANES:(g + 1) * LANES, it["rows"]] = ot[:, i * TQ:(i + 1) * TQ].astype(BF16)


def _win_attn(aq, ak, avt, sink, B, T):
    n_q = T // TQ_STEP
    return pl.pallas_call(
        functools.partial(_win_attn_kernel, T=T),
        grid=(B, n_q),
        in_specs=[pl.BlockSpec(memory_space=pltpu.SMEM),
                  pl.BlockSpec((TQ_STEP, ATT_Q), lambda b, i: (b * n_q + i, 0)),
                  pl.BlockSpec((T, ATT_KV), lambda b, i: (b, 0)),
                  pl.BlockSpec((ATT_KV, T), lambda b, i: (0, b))],
        out_specs=pl.BlockSpec((ATT_Q, TQ_STEP), lambda b, i: (0, b * n_q + i)),
        out_shape=jax.ShapeDtypeStruct((ATT_Q, B * T), BF16),
        compiler_params=_params("parallel", "arbitrary"),
        name="win_attn",
    )(sink, aq, ak, avt)


def _pair_blockdiag(y):
    z = jnp.zeros((CHUNK, CHUNK), y.dtype)
    top = jnp.concatenate([y[:, :CHUNK], z], axis=1)
    bot = jnp.concatenate([z, y[:, CHUNK:]], axis=1)
    return jnp.concatenate([top, bot], axis=0)


def _mm_pair(xb, yb):
    return jnp.dot(xb, _pair_blockdiag(yb), preferred_element_type=F32)


def _unit_tri_inverse_pairs(l2s):
    bf = lambda xs: [x.astype(BF16) for x in xs]
    r = _iota((CHUNK, 2 * CHUNK), 0)
    c = _iota((CHUNK, 2 * CHUNK), 1) & (CHUNK - 1)
    base = 16
    in_base = (r // base) == (c // base)
    ds = [jnp.where(in_base, l2, 0.0) for l2 in l2s]
    ps = [jnp.where(r == c, 1.0, 0.0) - d for d in ds]
    powb = bf(ds)
    for _ in range(3):
        powb = bf([_mm_pair(x, x) for x in powb])
        ps = [p + _mm_pair(pb, x) for p, pb, x in zip(ps, bf(ps), powb)]
    s = base
    while s < CHUNK:
        off_diag = ((r // (2 * s)) == (c // (2 * s))) & ((r // s) != (c // s))
        mbs = bf([jnp.where(off_diag, l2, 0.0) for l2 in l2s])
        pbs = bf(ps)
        tbs = bf([_mm_pair(mb, pb) for mb, pb in zip(mbs, pbs)])
        ps = [p - _mm_pair(pb, tb) for p, pb, tb in zip(ps, pbs, tbs)]
        s *= 2
    return ps


def _deltanet_kernel(q_ref, k_ref, v_ref, z_ref, gcol_ref, grow_ref, cwq_ref, cwk_ref, cwv_ref, nw_ref,
                     o_ref, pad_scr, u_scr, w_scr, qg_scr, at_scr, gl_scr, ku_scr, kw_scr, sp_scr, *, T):
    C = CHUNK
    N = T // C
    U = PREP_UNROLL
    h = pl.program_id(1)
    ri = _iota((C, C), 0)
    ci = _iota((C, C), 1)
    incl = (ri >= ci, ri <= ci)
    strict = (ri > ci, ri < ci)
    lane = _iota((C, LANES), 1)

    zpad = jnp.zeros((PAD, DN_HEAD_DIM), F32)
    for a, ref in enumerate((q_ref, k_ref, v_ref)):
        pad_scr[a, 0:PAD, :] = zpad
        pad_scr[a, T + PAD:T + 2 * PAD, :] = zpad

        def copy_rows(i, carry, a=a, ref=ref):
            r0 = pl.multiple_of(i * COPY_ROWS, COPY_ROWS)
            pad_scr[a, pl.ds(pl.multiple_of(r0 + PAD, PAD), COPY_ROWS), :] = ref[pl.ds(r0, COPY_ROWS), :]
            return carry

        lax.fori_loop(0, T // COPY_ROWS, copy_rows, 0)

    def conv_silu(a, cw_ref, c0):
        win = pad_scr.at[a, pl.ds(c0, C + 2 * PAD), :]
        off = PAD - DN_CONV // 2
        acc = win[off:off + C, :] * cw_ref[0:1, :]
        for j in range(1, DN_CONV):
            acc = acc + win[off + j:off + j + C, :] * cw_ref[j:j + 1, :]
        return _silu(acc)

    def l2norm(t):
        return t * lax.rsqrt(jnp.sum(t * t, axis=-1, keepdims=True) + EPS)

    def prepare_group(g, carry):
        ns = [g * U + u for u in range(U)]
        c0s = [pl.multiple_of(n * C, C) for n in ns]
        qs = [l2norm(conv_silu(0, cwq_ref, c0)) * (DN_HEAD_DIM ** -0.5) for c0 in c0s]
        ks = [l2norm(conv_silu(1, cwk_ref, c0)) for c0 in c0s]
        vs = [conv_silu(2, cwv_ref, c0) for c0 in c0s]
        kbs = [k.astype(BF16) for k in ks]
        grams = [lax.dot_general(jnp.concatenate([q.astype(BF16), kb], axis=0), kb, NT_DIMS,
                                 preferred_element_type=F32) for q, kb in zip(qs, kbs)]
        l2s, rhss, kds = [], [], []
        for u in range(U):
            rows = pl.ds(c0s[u], C)
            q, k, v = qs[u], ks[u], vs[u]
            gc = gcol_ref[rows, :]
            col = lambda idx: jnp.sum(jnp.where(lane == idx, gc, 0.0), axis=1, keepdims=True)
            beta = (col(h), col(DN_HEADS + h))
            gcc = (col(2 * DN_HEADS + h), col(3 * DN_HEADS + h))
            gr = grow_ref[2 * DN_HEADS:, rows]
            sub = _iota(gr.shape, 0)
            row = lambda idx: jnp.sum(jnp.where(sub == idx, gr, 0.0), axis=0, keepdims=True)
            gcr = (row(h), row(DN_HEADS + h))
            glast = (gcr[0][:, C - 1:C], gcr[1][:, 0:1])
            qk = grams[u][:C]
            kk = grams[u][C:]
            l2 = []
            for d in range(2):
                decay = jnp.where(incl[d], jnp.exp(jnp.where(incl[d], gcc[d] - gcr[d], 0.0)), 0.0)
                l2.append(jnp.where(strict[d], beta[d] * kk * decay, 0.0))
                eg = jnp.exp(gcc[d])
                qg_scr[d, rows, :] = (q * eg).astype(BF16)
                kds.append((k * jnp.exp(glast[d] - gcc[d])).astype(BF16))
                at_scr[d, rows, :] = (qk * decay).astype(BF16)
                gl_scr[d, ns[u]] = jnp.broadcast_to(jnp.exp(glast[d]), (8, LANES))
                rhss.append(jnp.concatenate([v * beta[d], (k * beta[d]) * eg], axis=1).astype(BF16))
            l2s.append(jnp.concatenate(l2, axis=1))
        ainvs = _unit_tri_inverse_pairs(l2s)
        for u in range(U):
            rows = pl.ds(c0s[u], C)
            for d in range(2):
                uw = jnp.dot(ainvs[u][:, d * C:(d + 1) * C].astype(BF16), rhss[2 * u + d],
                             preferred_element_type=F32)
                uwb = uw.astype(BF16)
                u_scr[d, rows, :] = uw[:, :DN_HEAD_DIM]
                w_scr[d, rows, :] = uwb[:, DN_HEAD_DIM:]
                kuw = lax.dot_general(kds[2 * u + d], uwb, TN_DIMS, preferred_element_type=F32)
                ku_scr[d, rows, :] = kuw[:, :DN_HEAD_DIM]
                kw_scr[d, rows, :] = kuw[:, DN_HEAD_DIM:].astype(BF16)
        return carry

    lax.fori_loop(0, N // U, prepare_group, 0)

    def scan_step(d, n, state):
        rows = pl.ds(pl.multiple_of(n * C, C), C)
        sb = state.astype(BF16)
        sp_scr[d, rows, :] = sb
        return (state * gl_scr[d, n][0:1, :] + ku_scr[d, rows, :]
                - jnp.dot(kw_scr[d, rows, :], sb, preferred_element_type=F32))

    def scan(i, states):
        return scan_step(0, i, states[0]), scan_step(1, N - 1 - i, states[1])

    s0 = jnp.zeros((DN_HEAD_DIM, DN_HEAD_DIM), F32)
    lax.fori_loop(0, N, scan, (s0, s0))

    def outputs_group(g, carry):
        rows = [pl.ds(pl.multiple_of((g * U + u) * C, C), C) for u in range(U)]
        items = [(r, d) for r in rows for d in range(2)]
        sbs = [sp_scr[d, r, :] for r, d in items]
        ws_qs = [jnp.dot(jnp.concatenate([w_scr[d, r, :], qg_scr[d, r, :]], axis=0), sb,
                         preferred_element_type=F32) for (r, d), sb in zip(items, sbs)]
        v_news = [(u_scr[d, r, :] - x[:C]).astype(BF16) for (r, d), x in zip(items, ws_qs)]
        os_ = [x[C:] + jnp.dot(at_scr[d, r, :], vn, preferred_element_type=F32)
               for (r, d), x, vn in zip(items, ws_qs, v_news)]
        for u, r in enumerate(rows):
            o = os_[2 * u] + os_[2 * u + 1]
            o = o * lax.rsqrt(jnp.mean(o * o, axis=-1, keepdims=True) + EPS) * nw_ref[...]
            o_ref[r, :] = (o * _silu(z_ref[r, :])).astype(BF16)
        return carry

    lax.fori_loop(0, N // U, outputs_group, 0)


def _deltanet(dqkv, dz, gcol, grow, conv_w, norm_w, B, T):
    H = DN_HEADS
    N = T // CHUNK
    seq = lambda off: pl.BlockSpec((T, DN_HEAD_DIM), lambda b, h: (b, off + h))
    cw = lambda off: pl.BlockSpec((DN_CONV, DN_HEAD_DIM), lambda b, h: (0, off + h))
    return pl.pallas_call(
        functools.partial(_deltanet_kernel, T=T),
        grid=(B, H),
        in_specs=[seq(0), seq(H), seq(2 * H), seq(0),
                  pl.BlockSpec((T, LANES), lambda b, h: (b, 0)),
                  pl.BlockSpec((N_GATES, T), lambda b, h: (0, b)),
                  cw(0), cw(H), cw(2 * H),
                  pl.BlockSpec((1, DN_HEAD_DIM), lambda b, h: (0, 0))],
        out_specs=seq(0),
        out_shape=jax.ShapeDtypeStruct((B * T, DN_W), BF16),
        scratch_shapes=[pltpu.VMEM((3, T + 2 * PAD, DN_HEAD_DIM), F32),
                        pltpu.VMEM((2, T, DN_HEAD_DIM), F32),
                        pltpu.VMEM((2, T, DN_HEAD_DIM), BF16),
                        pltpu.VMEM((2, T, DN_HEAD_DIM), BF16),
                        pltpu.VMEM((2, T, CHUNK), BF16),
                        pltpu.VMEM((2, N, 8, LANES), F32),
                        pltpu.VMEM((2, T, DN_HEAD_DIM), F32),
                        pltpu.VMEM((2, T, DN_HEAD_DIM), BF16),
                        pltpu.VMEM((2, T, DN_HEAD_DIM), BF16)],
        compiler_params=_params("parallel", "arbitrary"),
        name="deltanet",
    )(dqkv, dqkv, dqkv, dz, gcol, grow, conv_w, conv_w, conv_w, norm_w)


def _out_proj_kernel(att_t_ref, dn_ref, x_ref, woa_ref, wod_ref, g1_ref, g2_ref, wq_ref, x1_ref, qx_ref):
    subs = [slice(i * TM, (i + 1) * TM) for i in range(SUB_TILES)]
    mixes = [lax.dot_general(att_t_ref[:, r], woa_ref[...], TN_DIMS, preferred_element_type=F32)
             + jnp.dot(dn_ref[r, :], wod_ref[...], preferred_element_type=F32) for r in subs]
    x1s = [x_ref[r, :] + _rms(mix, g1_ref[...]) for r, mix in zip(subs, mixes)]
    hs = [_rms(x1, g2_ref[...]).astype(BF16) for x1 in x1s]
    for r, x1, hb in zip(subs, x1s, hs):
        x1_ref[r, :] = x1
        qx_ref[r, :] = jnp.dot(hb, wq_ref[...], preferred_element_type=F32).astype(BF16)


def _out_proj(att_t, dn, x2d, woa, wod, g1, g2, wq):
    BT = x2d.shape[0]
    row = lambda i: (i, 0)
    full = lambda a: pl.BlockSpec(a.shape, lambda i: (0, 0))
    return pl.pallas_call(
        _out_proj_kernel,
        grid=(BT // TS,),
        in_specs=[pl.BlockSpec((ATT_Q, TS), lambda i: (0, i)), pl.BlockSpec((TS, DN_W), row),
                  pl.BlockSpec((TS, D_MODEL), row),
                  full(woa), full(wod), full(g1), full(g2), full(wq)],
        out_specs=[pl.BlockSpec((TS, D_MODEL), row), pl.BlockSpec((TS, X_W), row)],
        out_shape=[jax.ShapeDtypeStruct((BT, D_MODEL), F32), jax.ShapeDtypeStruct((BT, X_W), BF16)],
        compiler_params=_params("parallel"),
        name="out_proj",
    )(att_t, dn, x2d, woa, wod, g1, g2, wq)


def _mem_kv_kernel(mem_ref, g_ref, w_ref, kv_ref):
    kv_ref[...] = _mm(_rms(mem_ref[...], g_ref[...]), w_ref[...]).astype(BF16)


def _mem_kv(mem2d, g, wkv):
    M = mem2d.shape[0]
    mem_len = 256
    return pl.pallas_call(
        _mem_kv_kernel,
        grid=(M // mem_len,),
        in_specs=[pl.BlockSpec((mem_len, D_MODEL), lambda i: (i, 0)),
                  pl.BlockSpec(g.shape, lambda i: (0, 0)), pl.BlockSpec(wkv.shape, lambda i: (0, 0))],
        out_specs=pl.BlockSpec((mem_len, 2 * X_W), lambda i: (i, 0)),
        out_shape=jax.ShapeDtypeStruct((M, 2 * X_W), BF16),
        compiler_params=_params("parallel"),
        name="mem_kv",
    )(mem2d, g, wkv)


def _xattn_kernel(qx_ref, kv_ref, x1_ref, wo_ref, g_ref, x2_ref):
    subs = [slice(i * TM, (i + 1) * TM) for i in range(SUB_TILES)]
    items = [(r, hd) for r in subs for hd in range(X_HEADS)]
    ksl = lambda hd: slice(hd * X_HEAD_DIM, (hd + 1) * X_HEAD_DIM)
    vsl = lambda hd: slice(X_W + hd * X_HEAD_DIM, X_W + (hd + 1) * X_HEAD_DIM)
    ss = [lax.dot_general(qx_ref[r, ksl(hd)], kv_ref[:, ksl(hd)], NT_DIMS, preferred_element_type=F32)
          * (X_HEAD_DIM ** -0.5) for r, hd in items]
    ps = [jnp.exp(s - jnp.max(s, axis=-1, keepdims=True)) for s in ss]
    dens = [jnp.sum(p, axis=-1, keepdims=True) for p in ps]
    os_ = [jnp.dot(p.astype(BF16), kv_ref[:, vsl(hd)], preferred_element_type=F32) / den
           for (r, hd), p, den in zip(items, ps, dens)]
    for i, r in enumerate(subs):
        o = jnp.concatenate(os_[i * X_HEADS:(i + 1) * X_HEADS], axis=1)
        x2_ref[r, :] = x1_ref[r, :] + _rms(_mm(o, wo_ref[...]), g_ref[...])


def _xattn(qx, kv, x1, wo, g, B, T):
    n_t = T // TS
    mem_len = kv.shape[0] // B
    row = lambda b, i: (b * n_t + i, 0)
    return pl.pallas_call(
        _xattn_kernel,
        grid=(B, n_t),
        in_specs=[pl.BlockSpec((TS, X_W), row), pl.BlockSpec((mem_len, 2 * X_W), lambda b, i: (b, 0)),
                  pl.BlockSpec((TS, D_MODEL), row),
                  pl.BlockSpec(wo.shape, lambda b, i: (0, 0)), pl.BlockSpec(g.shape, lambda b, i: (0, 0))],
        out_specs=pl.BlockSpec((TS, D_MODEL), row),
        out_shape=jax.ShapeDtypeStruct((B * T, D_MODEL), F32),
        compiler_params=_params("parallel", "arbitrary"),
        name="xattn",
    )(qx, kv, x1, wo, g)


def _mlp_kernel(x_ref, g1_ref, w1_ref, w2_ref, g2_ref, y_ref):
    x = x_ref[...]
    hb = _rms(x, g1_ref[...]).astype(BF16)
    acc = jnp.zeros((TM, D_MODEL), F32)
    for c in range(D_FF // FF_BLOCK):
        sl = slice(c * FF_BLOCK, (c + 1) * FF_BLOCK)
        a = jnp.maximum(jnp.dot(hb, w1_ref[:, sl], preferred_element_type=F32), 0.0)
        acc = acc + jnp.dot((a * a).astype(BF16), w2_ref[sl, :], preferred_element_type=F32)
    y_ref[...] = x + _rms(acc, g2_ref[...])


def _mlp(x2, g1, w1, w2, g2):
    BT = x2.shape[0]
    row = lambda i: (i, 0)
    full = lambda a: pl.BlockSpec(a.shape, lambda i: (0, 0))
    return pl.pallas_call(
        _mlp_kernel,
        grid=(BT // TM,),
        in_specs=[pl.BlockSpec((TM, D_MODEL), row), full(g1), full(w1), full(w2), full(g2)],
        out_specs=pl.BlockSpec((TM, D_MODEL), row),
        out_shape=jax.ShapeDtypeStruct((BT, D_MODEL), F32),
        compiler_params=_params("parallel"),
        name="mlp",
    )(x2, g1, w1, w2, g2)


def _rope_tables(T):
    half = ROPE_DIM // 2
    inv = ROPE_THETA ** (-(jnp.arange(half, dtype=F32) * 2.0 / ROPE_DIM))
    d = jnp.arange(LANES) % ATT_HEAD_DIM
    ang = jnp.arange(T).astype(F32)[:, None] * inv[d % half][None, :]
    cos, sin = jnp.cos(ang), jnp.sin(ang)
    lo, hi = (d < half)[None, :], ((d >= half) & (d < ROPE_DIM))[None, :]
    return jnp.where(lo | hi, cos, 1.0), jnp.where(lo, -sin, 0.0), jnp.where(hi, sin, 0.0)


def _gate_params(a_log_f, a_log_b, dt_f, dt_b):
    zeros = jnp.zeros((2 * DN_HEADS,), F32)
    a_vec = jnp.concatenate([zeros, a_log_f.astype(F32), a_log_b.astype(F32)])
    d_vec = jnp.concatenate([zeros, dt_f.astype(F32), dt_b.astype(F32)])
    pad = jnp.zeros((LANES - N_GATES,), F32)
    return jnp.stack([jnp.concatenate([a_vec, pad]), jnp.concatenate([d_vec, pad])])


def _layer(x, mem, rope_tables, w_in, attn_sink, conv_w, a_log_f, a_log_b, dt_f, dt_b, dn_norm_w, w_out,
           xa_wq, xa_wkv, xa_wo, mem_g, w1, w2, g_pre_mix, g_post_mix, g_pre_xa, g_post_xa,
           g_pre_mlp, g_post_mlp):
    B, T, _ = x.shape
    assert T % TS == 0 and T % (PREP_UNROLL * CHUNK) == 0 and T >= TQ + 2 * ATT_WINDOW
    x2d = x.reshape(B * T, D_MODEL)
    vec = lambda g: g.astype(F32).reshape(1, -1)
    hpg = ATT_HEADS // ATT_KV_HEADS

    def pair_heads(w, axis):
        shape = w.shape
        w = w.reshape(shape[:axis] + (ATT_KV_HEADS, hpg, ATT_HEAD_DIM) + shape[axis + 1:])
        return jnp.swapaxes(w, axis, axis + 1).reshape(shape)

    wq = pair_heads(w_in[:, :ATT_Q], 1).astype(BF16)
    wk = w_in[:, O_AK:O_AV].astype(BF16)
    wv = w_in[:, O_AV:O_DQKV].astype(BF16)
    wd = w_in[:, O_DQKV:O_DZ].astype(BF16)
    wz = w_in[:, O_DZ:O_GATES].astype(BF16)
    wg = w_in[:, O_GATES:O_GATES + N_GATES]
    wgc = jnp.pad(wg, ((0, 0), (0, LANES - N_GATES))).astype(BF16)
    cos, sa, sb = rope_tables
    pc = _gate_params(a_log_f, a_log_b, dt_f, dt_b)

    aq, ak, avt, dqkv, dz, gcol, grow = _in_proj(x2d, T, vec(g_pre_mix), wq, wk, wv, wd, wz, wgc,
                                                 cos, sa, sb, pc)
    att_t = _win_attn(aq, ak, avt, attn_sink.astype(F32), B, T)
    dn = _deltanet(dqkv, dz, gcol, grow, conv_w.astype(F32), vec(dn_norm_w), B, T)
    x1, qx = _out_proj(att_t, dn, x2d, pair_heads(w_out[:ATT_Q], 0).astype(BF16), w_out[ATT_Q:].astype(BF16),
                       vec(g_post_mix), vec(g_pre_xa), xa_wq.astype(BF16))
    kv = _mem_kv(mem.reshape(-1, D_MODEL), vec(mem_g), xa_wkv.astype(BF16))
    x2 = _xattn(qx, kv, x1, xa_wo.astype(BF16), vec(g_post_xa), B, T)
    y = _mlp(x2, vec(g_pre_mlp), w1.astype(BF16), w2.astype(BF16), vec(g_post_mlp))
    return y.reshape(B, T, D_MODEL)


def kernel(x_prompt, x_sample, mem_prompt, mem_sample, w_in, attn_sink, dn_conv_w, dn_A_log_f, dn_A_log_b,
           dn_dt_bias_f, dn_dt_bias_b, dn_norm_w, w_out, xa_wq, xa_wkv, xa_wo, mem_norm_g, mlp_w1, mlp_w2,
           norm_pre_mix, norm_post_mix, norm_pre_xa, norm_post_xa, norm_pre_mlp, norm_post_mlp):
    weights = (w_in, attn_sink, dn_conv_w, dn_A_log_f, dn_A_log_b, dn_dt_bias_f, dn_dt_bias_b, dn_norm_w,
               w_out, xa_wq, xa_wkv, xa_wo, mem_norm_g, mlp_w1, mlp_w2, norm_pre_mix, norm_post_mix,
               norm_pre_xa, norm_post_xa, norm_pre_mlp, norm_post_mlp)
    rope_tables = _rope_tables(max(x_prompt.shape[1], x_sample.shape[1]))
    outs = []
    for x, mem in ((x_prompt, mem_prompt), (x_sample, mem_sample)):
        for l in range(w_in.shape[0]):
            x = _layer(x, mem, rope_tables, *(w[l] for w in weights))
        outs.append(x)
    return tuple(outs)
```

```python
import functools

import jax
import jax.numpy as jnp
from jax import lax
from jax.experimental import pallas as pl
from jax.experimental.pallas import tpu as pltpu

F32 = jnp.float32
BF16 = jnp.bfloat16

D_MODEL = 1024
ATT_HEADS = 8
ATT_KV_HEADS = 2
ATT_HEAD_DIM = 64
ATT_WINDOW = 128
ROPE_THETA = 500000.0
ROPE_DIM = ATT_HEAD_DIM // 4
DN_HEADS = 4
DN_HEAD_DIM = 128
DN_CONV = 5
X_HEADS = 4
X_HEAD_DIM = 128
D_FF = 4 * D_MODEL
EPS = 1e-6

ATT_Q = ATT_HEADS * ATT_HEAD_DIM
ATT_KV = ATT_KV_HEADS * ATT_HEAD_DIM
DN_W = DN_HEADS * DN_HEAD_DIM
X_W = X_HEADS * X_HEAD_DIM
O_AK = ATT_Q
O_AV = O_AK + ATT_KV
O_DQKV = O_AV + ATT_KV
O_DZ = O_DQKV + 3 * DN_W
O_GATES = O_DZ + DN_W
N_GATES = 4 * DN_HEADS

LANES = 128
TM = 512
TQ = 128
TQ_STEP = 256
CHUNK = 128
PREP_UNROLL = 8
PAD = 8
COPY_ROWS = 512
SUB_TILES = 2
TS = SUB_TILES * TM
FF_BLOCK = 1024
VMEM_LIMIT = 56 * 1024 * 1024

NT_DIMS = (((1,), (1,)), ((), ()))
TN_DIMS = (((0,), (0,)), ((), ()))


def _mm(a, b):
    return jnp.dot(a.astype(BF16), b.astype(BF16), preferred_element_type=F32)


def _mm_nt(a, b):
    return lax.dot_general(a.astype(BF16), b.astype(BF16), NT_DIMS, preferred_element_type=F32)


def _mm_tn(a, b):
    return lax.dot_general(a.astype(BF16), b.astype(BF16), TN_DIMS, preferred_element_type=F32)


def _rms(x, g):
    return x * lax.rsqrt(jnp.mean(x * x, axis=-1, keepdims=True) + EPS) * g


def _softplus(x):
    return jnp.maximum(x, 0.0) + jnp.log1p(jnp.exp(-jnp.abs(x)))


def _silu(x):
    return x * jax.nn.sigmoid(x)


def _iota(shape, dim):
    return lax.broadcasted_iota(jnp.int32, shape, dim)


def _params(*sem):
    return pltpu.CompilerParams(dimension_semantics=sem, vmem_limit_bytes=VMEM_LIMIT)


def _in_proj_kernel(x_ref, g_ref, wq_ref, wk_ref, wv_ref, wd_ref, wz_ref, wgc_ref,
                    cos_ref, sa_ref, sb_ref, pc_ref,
                    aq_ref, ak_ref, avt_ref, dqkv_ref, dz_ref, grow_ref):
    subs = [slice(i * TM, (i + 1) * TM) for i in range(SUB_TILES)]
    hbs = [_rms(x_ref[r, :], g_ref[...]).astype(BF16) for r in subs]

    def rope(t, r):
        return (t * cos_ref[r, :] + pltpu.roll(t, LANES - ROPE_DIM // 2, 1) * sa_ref[r, :]
                + pltpu.roll(t, ROPE_DIM // 2, 1) * sb_ref[r, :])

    for r, hb in zip(subs, hbs):
        q = jnp.dot(hb, wq_ref[...], preferred_element_type=F32)
        for j in range(ATT_Q // LANES):
            sl = slice(j * LANES, (j + 1) * LANES)
            aq_ref[r, sl] = (rope(q[:, sl], r) * (ATT_HEAD_DIM ** -0.5)).astype(BF16)
        ak_ref[r, :] = rope(jnp.dot(hb, wk_ref[...], preferred_element_type=F32), r).astype(BF16)
        avt_ref[:, r] = jnp.dot(hb, wv_ref[...], preferred_element_type=F32).astype(BF16).T
        dqkv_ref[r, :] = jnp.dot(hb, wd_ref[...], preferred_element_type=F32)
        dz_ref[r, :] = jnp.dot(hb, wz_ref[...], preferred_element_type=F32)

        raw = jnp.dot(hb, wgc_ref[...], preferred_element_type=F32)
        lane = _iota(raw.shape, 1)
        gval = -jnp.exp(pc_ref[0:1, :]) * _softplus(raw + pc_ref[1:2, :])
        gates = jnp.where(lane < 2 * DN_HEADS, jax.nn.sigmoid(raw), gval)
        pos = _iota(raw.shape, 0) & (CHUNK - 1)
        cf = gates
        cb = gates
        s = 1
        while s < CHUNK:
            cf = cf + jnp.where(pos >= s, pltpu.roll(cf, s, 0), 0.0)
            cb = cb + jnp.where(pos < CHUNK - s, pltpu.roll(cb, TM - s, 0), 0.0)
            s *= 2
        gcol = jnp.where(lane < 2 * DN_HEADS, gates, jnp.where(lane < 3 * DN_HEADS, cf, cb))
        grow_ref[:, r] = gcol.T[:N_GATES]


def _in_proj(x2d, T, g, wq, wk, wv, wd, wz, wgc, cos, sa, sb, pc):
    BT = x2d.shape[0]
    n_t = T // TS
    row = lambda i: (i, 0)
    const = lambda i: (0, 0)
    tab = lambda i: (i % n_t, 0)
    full = lambda a: pl.BlockSpec(a.shape, const)
    return pl.pallas_call(
        _in_proj_kernel,
        grid=(BT // TS,),
        in_specs=[pl.BlockSpec((TS, D_MODEL), row), full(g), full(wq), full(wk), full(wv), full(wd), full(wz),
                  full(wgc),
                  pl.BlockSpec((TS, LANES), tab), pl.BlockSpec((TS, LANES), tab), pl.BlockSpec((TS, LANES), tab),
                  full(pc)],
        out_specs=[pl.BlockSpec((TS, ATT_Q), row), pl.BlockSpec((TS, ATT_KV), row),
                   pl.BlockSpec((ATT_KV, TS), lambda i: (0, i)),
                   pl.BlockSpec((TS, 3 * DN_W), row), pl.BlockSpec((TS, DN_W), row),
                   pl.BlockSpec((N_GATES, TS), lambda i: (0, i))],
        out_shape=[jax.ShapeDtypeStruct((BT, ATT_Q), BF16), jax.ShapeDtypeStruct((BT, ATT_KV), BF16),
                   jax.ShapeDtypeStruct((ATT_KV, BT), BF16),
                   jax.ShapeDtypeStruct((BT, 3 * DN_W), F32), jax.ShapeDtypeStruct((BT, DN_W), F32),
                   jax.ShapeDtypeStruct((N_GATES, BT), F32)],
        compiler_params=_params("parallel"),
        name="in_proj",
    )(x2d, g, wq, wk, wv, wd, wz, wgc, cos, sa, sb, pc)


def _win_attn_kernel(sink_ref, q_ref, k_ref, vt_ref, o_ref, *, T):
    W = TQ + 2 * ATT_WINDOW
    hpg = ATT_HEADS // ATT_KV_HEADS
    items = []
    for sub in range(TQ_STEP // TQ):
        t0 = pl.program_id(1) * TQ_STEP + sub * TQ
        start = pl.multiple_of(jnp.clip(t0 - ATT_WINDOW, 0, T - W), LANES)
        k = k_ref[pl.ds(start, W), :]
        vt = vt_ref[:, pl.ds(start, W)]
        k_lo = _iota(k.shape, 1) < ATT_HEAD_DIM
        v_lo = _iota(vt.shape, 0) < ATT_HEAD_DIM
        kk = jnp.concatenate([jnp.where(k_lo, k, jnp.zeros_like(k)), jnp.where(k_lo, jnp.zeros_like(k), k)], axis=0)
        vvt = jnp.concatenate([jnp.where(v_lo, vt, jnp.zeros_like(vt)), jnp.where(v_lo, jnp.zeros_like(vt), vt)],
                              axis=1)
        kpos = start + _iota((W, 2 * TQ), 0)
        qpos = t0 + (_iota((W, 2 * TQ), 1) & (TQ - 1))
        cap = jnp.where(jnp.abs(qpos - kpos) <= ATT_WINDOW, jnp.inf, -1e30)
        rows = slice(sub * TQ, (sub + 1) * TQ)
        for pair in range(ATT_Q // LANES // 2):
            groups = (2 * pair, 2 * pair + 1)
            qp = jnp.concatenate([q_ref[rows, g * LANES:(g + 1) * LANES] for g in groups], axis=0)
            items.append(dict(kk=kk, vvt=vvt, cap=cap, qp=qp, rows=rows, groups=groups))
    for it in items:
        it["st"] = lax.dot_general(it["kk"], it["qp"], NT_DIMS, preferred_element_type=F32)
    for it in items:
        ps, dens = [], []
        for kvh in range(ATT_KV_HEADS):
            sh = jnp.minimum(it["st"][kvh * W:(kvh + 1) * W], it["cap"])
            sk = jnp.concatenate([jnp.full((1, TQ), sink_ref[kvh * hpg + g], F32) for g in it["groups"]], axis=1)
            m = jnp.maximum(jnp.max(sh, axis=0, keepdims=True), sk)
            p = jnp.exp(sh - m)
            dens.append(jnp.sum(p, axis=0, keepdims=True) + jnp.exp(sk - m))
            ps.append(p.astype(BF16))
        it["p"] = jnp.concatenate(ps, axis=0)
        it["dens"] = dens
    for it in items:
        ot = jnp.dot(it["vvt"], it["p"], preferred_element_type=F32)
        ot = ot / jnp.where(_iota(ot.shape, 0) < ATT_HEAD_DIM, it["dens"][0], it["dens"][1])
        for i, g in enumerate(it["groups"]):
            o_ref[g * LANES:(g + 1) * LANES, it["rows"]] = ot[:, i * TQ:(i + 1) * TQ].astype(BF16)


def _win_attn(aq, ak, avt, sink, B, T):
    n_q = T // TQ_STEP
    return pl.pallas_call(
        functools.partial(_win_attn_kernel, T=T),
        grid=(B, n_q),
        in_specs=[pl.BlockSpec(memory_space=pltpu.SMEM),
                  pl.BlockSpec((TQ_STEP, ATT_Q), lambda b, i: (b * n_q + i, 0)),
                  pl.BlockSpec((T, ATT_KV), lambda b, i: (b, 0)),
                  pl.BlockSpec((ATT_KV, T), lambda b, i: (0, b))],
        out_specs=pl.BlockSpec((ATT_Q, TQ_STEP), lambda b, i: (0, b * n_q + i)),
        out_shape=jax.ShapeDtypeStruct((ATT_Q, B * T), BF16),
        compiler_params=_params("parallel", "arbitrary"),
        name="win_attn",
    )(sink, aq, ak, avt)


def _pair_blockdiag(y):
    z = jnp.zeros((CHUNK, CHUNK), y.dtype)
    top = jnp.concatenate([y[:, :CHUNK], z], axis=1)
    bot = jnp.concatenate([z, y[:, CHUNK:]], axis=1)
    return jnp.concatenate([top, bot], axis=0)


def _mm_pair(xb, yb):
    return jnp.dot(xb, _pair_blockdiag(yb), preferred_element_type=F32)


def _unit_tri_inverse_pairs(l2s):
    bf = lambda xs: [x.astype(BF16) for x in xs]
    r = _iota((CHUNK, 2 * CHUNK), 0)
    c = _iota((CHUNK, 2 * CHUNK), 1) & (CHUNK - 1)
    base = 16
    in_base = (r // base) == (c // base)
    ds = [jnp.where(in_base, l2, 0.0) for l2 in l2s]
    ps = [jnp.where(r == c, 1.0, 0.0) - d for d in ds]
    powb = bf(ds)
    for _ in range(3):
        powb = bf([_mm_pair(x, x) for x in powb])
        ps = [p + _mm_pair(pb, x) for p, pb, x in zip(ps, bf(ps), powb)]
    s = base
    while s < CHUNK:
        off_diag = ((r // (2 * s)) == (c // (2 * s))) & ((r // s) != (c // s))
        mbs = bf([jnp.where(off_diag, l2, 0.0) for l2 in l2s])
        pbs = bf(ps)
        tbs = bf([_mm_pair(mb, pb) for mb, pb in zip(mbs, pbs)])
        ps = [p - _mm_pair(pb, tb) for p, pb, tb in zip(ps, pbs, tbs)]
        s *= 2
    return ps


def _deltanet_kernel(q_ref, k_ref, v_ref, z_ref, grow_ref, cwq_ref, cwk_ref, cwv_ref, nw_ref,
                     o_ref, pad_scr, u_scr, w_scr, qg_scr, at_scr, gl_scr, ku_scr, kw_scr, sp_scr, *, T):
    C = CHUNK
    N = T // C
    U = PREP_UNROLL
    h = pl.program_id(1)
    ri = _iota((C, C), 0)
    ci = _iota((C, C), 1)
    incl = (ri >= ci, ri <= ci)
    strict = (ri > ci, ri < ci)
    lane = _iota((C, LANES), 1)

    zpad = jnp.zeros((PAD, DN_HEAD_DIM), F32)
    for a, ref in enumerate((q_ref, k_ref, v_ref)):
        pad_scr[a, 0:PAD, :] = zpad
        pad_scr[a, T + PAD:T + 2 * PAD, :] = zpad

        def copy_rows(i, carry, a=a, ref=ref):
            r0 = pl.multiple_of(i * COPY_ROWS, COPY_ROWS)
            pad_scr[a, pl.ds(pl.multiple_of(r0 + PAD, PAD), COPY_ROWS), :] = ref[pl.ds(r0, COPY_ROWS), :]
            return carry

        lax.fori_loop(0, T // COPY_ROWS, copy_rows, 0)

    def conv_silu(a, cw_ref, c0):
        win = pad_scr.at[a, pl.ds(c0, C + 2 * PAD), :]
        off = PAD - DN_CONV // 2
        acc = win[off:off + C, :] * cw_ref[0:1, :]
        for j in range(1, DN_CONV):
            acc = acc + win[off + j:off + j + C, :] * cw_ref[j:j + 1, :]
        return _silu(acc)

    def l2norm(t):
        return t * lax.rsqrt(jnp.sum(t * t, axis=-1, keepdims=True) + EPS)

    def chunk_start(n):
        return n * C if isinstance(n, int) else pl.multiple_of(n * C, C)

    def prepare_chunks(ns):
        c0s = [chunk_start(n) for n in ns]
        qs = [l2norm(conv_silu(0, cwq_ref, c0)) * (DN_HEAD_DIM ** -0.5) for c0 in c0s]
        ks = [l2norm(conv_silu(1, cwk_ref, c0)) for c0 in c0s]
        vs = [conv_silu(2, cwv_ref, c0) for c0 in c0s]
        kbs = [k.astype(BF16) for k in ks]
        grams = [lax.dot_general(jnp.concatenate([q.astype(BF16), kb], axis=0), kb, NT_DIMS,
                                 preferred_element_type=F32) for q, kb in zip(qs, kbs)]
        l2s, rhss, kds = [], [], []
        for u in range(U):
            rows = pl.ds(c0s[u], C)
            q, k, v = qs[u], ks[u], vs[u]
            gr = grow_ref[:, rows]
            gt = jnp.concatenate([gr, jnp.zeros((LANES - N_GATES, C), F32)], axis=0).T
            col = lambda idx: jnp.sum(jnp.where(lane == idx, gt, 0.0), axis=1, keepdims=True)
            beta = (col(h), col(DN_HEADS + h))
            gcc = (col(2 * DN_HEADS + h), col(3 * DN_HEADS + h))
            sub = _iota(gr.shape, 0)
            row = lambda idx: jnp.sum(jnp.where(sub == idx, gr, 0.0), axis=0, keepdims=True)
            gcr = (row(2 * DN_HEADS + h), row(3 * DN_HEADS + h))
            glast = (gcr[0][:, C - 1:C], gcr[1][:, 0:1])
            qk = grams[u][:C]
            kk = grams[u][C:]
            l2 = []
            for d in range(2):
                decay = jnp.where(incl[d], jnp.exp(jnp.where(incl[d], gcc[d] - gcr[d], 0.0)), 0.0)
                l2.append(jnp.where(strict[d], beta[d] * kk * decay, 0.0))
                eg = jnp.exp(gcc[d])
                qg_scr[d, rows, :] = (q * eg).astype(BF16)
                kds.append((k * jnp.exp(glast[d] - gcc[d])).astype(BF16))
                at_scr[d, rows, :] = (qk * decay).astype(BF16)
                gl_scr[d, ns[u]] = jnp.broadcast_to(jnp.exp(glast[d]), (8, LANES))
                rhss.append(jnp.concatenate([v * beta[d], (k * beta[d]) * eg], axis=1).astype(BF16))
            l2s.append(jnp.concatenate(l2, axis=1))
        ainvs = _unit_tri_inverse_pairs(l2s)
        for u in range(U):
            rows = pl.ds(c0s[u], C)
            for d in range(2):
                uw = jnp.dot(ainvs[u][:, d * C:(d + 1) * C].astype(BF16), rhss[2 * u + d],
                             preferred_element_type=F32)
                uwb = uw.astype(BF16)
                u_scr[d, rows, :] = uw[:, :DN_HEAD_DIM]
                w_scr[d, rows, :] = uwb[:, DN_HEAD_DIM:]
                kuw = lax.dot_general(kds[2 * u + d], uwb, TN_DIMS, preferred_element_type=F32)
                ku_scr[d, rows, :] = kuw[:, :DN_HEAD_DIM]
                kw_scr[d, rows, :] = kuw[:, DN_HEAD_DIM:].astype(BF16)

    def scan_step(d, n, state):
        rows = pl.ds(chunk_start(n), C)
        sb = state.astype(BF16)
        sp_scr[d, rows, :] = sb
        return (state * gl_scr[d, n][0:1, :] + ku_scr[d, rows, :]
                - jnp.dot(kw_scr[d, rows, :], sb, preferred_element_type=F32))

    def scan_steps(i0, states):
        for j in range(half):
            i = i0 + j
            states = (scan_step(0, i, states[0]), scan_step(1, N - 1 - i, states[1]))
        return states

    def output_chunks(ns):
        rows = [pl.ds(chunk_start(n), C) for n in ns]
        items = [(r, d) for r in rows for d in range(2)]
        sbs = [sp_scr[d, r, :] for r, d in items]
        ws_qs = [jnp.dot(jnp.concatenate([w_scr[d, r, :], qg_scr[d, r, :]], axis=0), sb,
                         preferred_element_type=F32) for (r, d), sb in zip(items, sbs)]
        v_news = [(u_scr[d, r, :] - x[:C]).astype(BF16) for (r, d), x in zip(items, ws_qs)]
        os_ = [x[C:] + jnp.dot(at_scr[d, r, :], vn, preferred_element_type=F32)
               for (r, d), x, vn in zip(items, ws_qs, v_news)]
        for u, r in enumerate(rows):
            o = os_[2 * u] + os_[2 * u + 1]
            o = o * lax.rsqrt(jnp.mean(o * o, axis=-1, keepdims=True) + EPS) * nw_ref[...]
            o_ref[r, :] = (o * _silu(z_ref[r, :])).astype(BF16)

    half = U // 2
    G = N // U
    front_back = lambda g: ([g * half + j for j in range(half)]
                            + [N - (g + 1) * half + j for j in range(half)])
    finished = lambda g: ([N // 2 + g * half + j for j in range(half)]
                          + [N // 2 - 1 - g * half - j for j in range(half)])
    s0 = jnp.zeros((DN_HEAD_DIM, DN_HEAD_DIM), F32)
    prepare_chunks(front_back(0))

    def prepare_and_scan(g, states):
        states = scan_steps((g - 1) * half, states)
        prepare_chunks(front_back(g))
        return states

    states = lax.fori_loop(1, G, prepare_and_scan, (s0, s0))
    states = scan_steps((G - 1) * half, states)
    states = scan_steps(N // 2, states)

    def scan_and_output(g, states):
        states = scan_steps(N // 2 + g * half, states)
        output_chunks(finished(g - 1))
        return states

    lax.fori_loop(1, G, scan_and_output, states)
    output_chunks(finished(G - 1))


def _deltanet(dqkv, dz, grow, conv_w, norm_w, B, T):
    H = DN_HEADS
    N = T // CHUNK
    seq = lambda off: pl.BlockSpec((T, DN_HEAD_DIM), lambda b, h: (b, off + h))
    cw = lambda off: pl.BlockSpec((DN_CONV, DN_HEAD_DIM), lambda b, h: (0, off + h))
    return pl.pallas_call(
        functools.partial(_deltanet_kernel, T=T),
        grid=(B, H),
        in_specs=[seq(0), seq(H), seq(2 * H), seq(0),
                  pl.BlockSpec((N_GATES, T), lambda b, h: (0, b)),
                  cw(0), cw(H), cw(2 * H),
                  pl.BlockSpec((1, DN_HEAD_DIM), lambda b, h: (0, 0))],
        out_specs=seq(0),
        out_shape=jax.ShapeDtypeStruct((B * T, DN_W), BF16),
        scratch_shapes=[pltpu.VMEM((3, T + 2 * PAD, DN_HEAD_DIM), F32),
                        pltpu.VMEM((2, T, DN_HEAD_DIM), F32),
                        pltpu.VMEM((2, T, DN_HEAD_DIM), BF16),
                        pltpu.VMEM((2, T, DN_HEAD_DIM), BF16),
                        pltpu.VMEM((2, T, CHUNK), BF16),
                        pltpu.VMEM((2, N, 8, LANES), F32),
                        pltpu.VMEM((2, T, DN_HEAD_DIM), F32),
                        pltpu.VMEM((2, T, DN_HEAD_DIM), BF16),
                        pltpu.VMEM((2, T, DN_HEAD_DIM), BF16)],
        compiler_params=_params("parallel", "arbitrary"),
        name="deltanet",
    )(dqkv, dqkv, dqkv, dz, grow, conv_w, conv_w, conv_w, norm_w)


def _out_proj_kernel(att_t_ref, dn_ref, x_ref, woa_ref, wod_ref, g1_ref, g2_ref, wq_ref, x1_ref, qx_ref):
    subs = [slice(i * TM, (i + 1) * TM) for i in range(SUB_TILES)]
    mixes = [lax.dot_general(att_t_ref[:, r], woa_ref[...], TN_DIMS, preferred_element_type=F32)
             + jnp.dot(dn_ref[r, :], wod_ref[...], preferred_element_type=F32) for r in subs]
    x1s = [x_ref[r, :] + _rms(mix, g1_ref[...]) for r, mix in zip(subs, mixes)]
    hs = [_rms(x1, g2_ref[...]).astype(BF16) for x1 in x1s]
    for r, x1, hb in zip(subs, x1s, hs):
        x1_ref[r, :] = x1
        qx_ref[r, :] = jnp.dot(hb, wq_ref[...], preferred_element_type=F32).astype(BF16)


def _out_proj(att_t, dn, x2d, woa, wod, g1, g2, wq):
    BT = x2d.shape[0]
    row = lambda i: (i, 0)
    full = lambda a: pl.BlockSpec(a.shape, lambda i: (0, 0))
    return pl.pallas_call(
        _out_proj_kernel,
        grid=(BT // TS,),
        in_specs=[pl.BlockSpec((ATT_Q, TS), lambda i: (0, i)), pl.BlockSpec((TS, DN_W), row),
                  pl.BlockSpec((TS, D_MODEL), row),
                  full(woa), full(wod), full(g1), full(g2), full(wq)],
        out_specs=[pl.BlockSpec((TS, D_MODEL), row), pl.BlockSpec((TS, X_W), row)],
        out_shape=[jax.ShapeDtypeStruct((BT, D_MODEL), F32), jax.ShapeDtypeStruct((BT, X_W), BF16)],
        compiler_params=_params("parallel"),
        name="out_proj",
    )(att_t, dn, x2d, woa, wod, g1, g2, wq)


def _mem_kv_kernel(mem_ref, g_ref, w_ref, kv_ref):
    kv_ref[...] = _mm(_rms(mem_ref[...], g_ref[...]), w_ref[...]).astype(BF16)


def _mem_kv(mem2d, g, wkv):
    M = mem2d.shape[0]
    mem_len = 256
    return pl.pallas_call(
        _mem_kv_kernel,
        grid=(M // mem_len,),
        in_specs=[pl.BlockSpec((mem_len, D_MODEL), lambda i: (i, 0)),
                  pl.BlockSpec(g.shape, lambda i: (0, 0)), pl.BlockSpec(wkv.shape, lambda i: (0, 0))],
        out_specs=pl.BlockSpec((mem_len, 2 * X_W), lambda i: (i, 0)),
        out_shape=jax.ShapeDtypeStruct((M, 2 * X_W), BF16),
        compiler_params=_params("parallel"),
        name="mem_kv",
    )(mem2d, g, wkv)


def _xattn_kernel(qx_ref, kv_ref, x1_ref, wo_ref, g_ref, x2_ref):
    subs = [slice(i * TM, (i + 1) * TM) for i in range(SUB_TILES)]
    items = [(r, hd) for r in subs for hd in range(X_HEADS)]
    ksl = lambda hd: slice(hd * X_HEAD_DIM, (hd + 1) * X_HEAD_DIM)
    vsl = lambda hd: slice(X_W + hd * X_HEAD_DIM, X_W + (hd + 1) * X_HEAD_DIM)
    ss = [lax.dot_general(qx_ref[r, ksl(hd)], kv_ref[:, ksl(hd)], NT_DIMS, preferred_element_type=F32)
          * (X_HEAD_DIM ** -0.5) for r, hd in items]
    ps = [jnp.exp(s - jnp.max(s, axis=-1, keepdims=True)) for s in ss]
    dens = [jnp.sum(p, axis=-1, keepdims=True) for p in ps]
    os_ = [jnp.dot(p.astype(BF16), kv_ref[:, vsl(hd)], preferred_element_type=F32) / den
           for (r, hd), p, den in zip(items, ps, dens)]
    for i, r in enumerate(subs):
        o = jnp.concatenate(os_[i * X_HEADS:(i + 1) * X_HEADS], axis=1)
        x2_ref[r, :] = x1_ref[r, :] + _rms(_mm(o, wo_ref[...]), g_ref[...])


def _xattn(qx, kv, x1, wo, g, B, T):
    n_t = T // TS
    mem_len = kv.shape[0] // B
    row = lambda b, i: (b * n_t + i, 0)
    return pl.pallas_call(
        _xattn_kernel,
        grid=(B, n_t),
        in_specs=[pl.BlockSpec((TS, X_W), row), pl.BlockSpec((mem_len, 2 * X_W), lambda b, i: (b, 0)),
                  pl.BlockSpec((TS, D_MODEL), row),
                  pl.BlockSpec(wo.shape, lambda b, i: (0, 0)), pl.BlockSpec(g.shape, lambda b, i: (0, 0))],
        out_specs=pl.BlockSpec((TS, D_MODEL), row),
        out_shape=jax.ShapeDtypeStruct((B * T, D_MODEL), F32),
        compiler_params=_params("parallel", "arbitrary"),
        name="xattn",
    )(qx, kv, x1, wo, g)


def _mlp_kernel(x_ref, g1_ref, w1_ref, w2_ref, g2_ref, y_ref):
    x = x_ref[...]
    hb = _rms(x, g1_ref[...]).astype(BF16)
    acc = jnp.zeros((TM, D_MODEL), F32)
    for c in range(D_FF // FF_BLOCK):
        sl = slice(c * FF_BLOCK, (c + 1) * FF_BLOCK)
        a = jnp.maximum(jnp.dot(hb, w1_ref[:, sl], preferred_element_type=F32), 0.0)
        acc = acc + jnp.dot((a * a).astype(BF16), w2_ref[sl, :], preferred_element_type=F32)
    y_ref[...] = x + _rms(acc, g2_ref[...])


def _mlp(x2, g1, w1, w2, g2):
    BT = x2.shape[0]
    row = lambda i: (i, 0)
    full = lambda a: pl.BlockSpec(a.shape, lambda i: (0, 0))
    return pl.pallas_call(
        _mlp_kernel,
        grid=(BT // TM,),
        in_specs=[pl.BlockSpec((TM, D_MODEL), row), full(g1), full(w1), full(w2), full(g2)],
        out_specs=pl.BlockSpec((TM, D_MODEL), row),
        out_shape=jax.ShapeDtypeStruct((BT, D_MODEL), F32),
        compiler_params=_params("parallel"),
        name="mlp",
    )(x2, g1, w1, w2, g2)


def _rope_tables(T):
    half = ROPE_DIM // 2
    inv = ROPE_THETA ** (-(jnp.arange(half, dtype=F32) * 2.0 / ROPE_DIM))
    d = jnp.arange(LANES) % ATT_HEAD_DIM
    ang = jnp.arange(T).astype(F32)[:, None] * inv[d % half][None, :]
    cos, sin = jnp.cos(ang), jnp.sin(ang)
    lo, hi = (d < half)[None, :], ((d >= half) & (d < ROPE_DIM))[None, :]
    return jnp.where(lo | hi, cos, 1.0), jnp.where(lo, -sin, 0.0), jnp.where(hi, sin, 0.0)


def _gate_params(a_log_f, a_log_b, dt_f, dt_b):
    zeros = jnp.zeros((2 * DN_HEADS,), F32)
    a_vec = jnp.concatenate([zeros, a_log_f.astype(F32), a_log_b.astype(F32)])
    d_vec = jnp.concatenate([zeros, dt_f.astype(F32), dt_b.astype(F32)])
    pad = jnp.zeros((LANES - N_GATES,), F32)
    return jnp.stack([jnp.concatenate([a_vec, pad]), jnp.concatenate([d_vec, pad])])


def _layer(x, mem, rope_tables, w_in, attn_sink, conv_w, a_log_f, a_log_b, dt_f, dt_b, dn_norm_w, w_out,
           xa_wq, xa_wkv, xa_wo, mem_g, w1, w2, g_pre_mix, g_post_mix, g_pre_xa, g_post_xa,
           g_pre_mlp, g_post_mlp):
    B, T, _ = x.shape
    assert T % TS == 0 and T % (PREP_UNROLL * CHUNK) == 0 and T >= TQ + 2 * ATT_WINDOW
    x2d = x.reshape(B * T, D_MODEL)
    vec = lambda g: g.astype(F32).reshape(1, -1)
    hpg = ATT_HEADS // ATT_KV_HEADS

    def pair_heads(w, axis):
        shape = w.shape
        w = w.reshape(shape[:axis] + (ATT_KV_HEADS, hpg, ATT_HEAD_DIM) + shape[axis + 1:])
        return jnp.swapaxes(w, axis, axis + 1).reshape(shape)

    wq = pair_heads(w_in[:, :ATT_Q], 1).astype(BF16)
    wk = w_in[:, O_AK:O_AV].astype(BF16)
    wv = w_in[:, O_AV:O_DQKV].astype(BF16)
    wd = w_in[:, O_DQKV:O_DZ].astype(BF16)
    wz = w_in[:, O_DZ:O_GATES].astype(BF16)
    wg = w_in[:, O_GATES:O_GATES + N_GATES]
    wgc = jnp.pad(wg, ((0, 0), (0, LANES - N_GATES))).astype(BF16)
    cos, sa, sb = rope_tables
    pc = _gate_params(a_log_f, a_log_b, dt_f, dt_b)

    aq, ak, avt, dqkv, dz, grow = _in_proj(x2d, T, vec(g_pre_mix), wq, wk, wv, wd, wz, wgc, cos, sa, sb, pc)
    att_t = _win_attn(aq, ak, avt, attn_sink.astype(F32), B, T)
    dn = _deltanet(dqkv, dz, grow, conv_w.astype(F32), vec(dn_norm_w), B, T)
    x1, qx = _out_proj(att_t, dn, x2d, pair_heads(w_out[:ATT_Q], 0).astype(BF16), w_out[ATT_Q:].astype(BF16),
                       vec(g_post_mix), vec(g_pre_xa), xa_wq.astype(BF16))
    kv = _mem_kv(mem.reshape(-1, D_MODEL), vec(mem_g), xa_wkv.astype(BF16))
    x2 = _xattn(qx, kv, x1, xa_wo.astype(BF16), vec(g_post_xa), B, T)
    y = _mlp(x2, vec(g_pre_mlp), w1.astype(BF16), w2.astype(BF16), vec(g_post_mlp))
    return y.reshape(B, T, D_MODEL)


def kernel(x_prompt, x_sample, mem_prompt, mem_sample, w_in, attn_sink, dn_conv_w, dn_A_log_f, dn_A_log_b,
           dn_dt_bias_f, dn_dt_bias_b, dn_norm_w, w_out, xa_wq, xa_wkv, xa_wo, mem_norm_g, mlp_w1, mlp_w2,
           norm_pre_mix, norm_post_mix, norm_pre_xa, norm_post_xa, norm_pre_mlp, norm_post_mlp):
    weights = (w_in, attn_sink, dn_conv_w, dn_A_log_f, dn_A_log_b, dn_dt_bias_f, dn_dt_bias_b, dn_norm_w,
               w_out, xa_wq, xa_wkv, xa_wo, mem_norm_g, mlp_w1, mlp_w2, norm_pre_mix, norm_post_mix,
               norm_pre_xa, norm_post_xa, norm_pre_mlp, norm_post_mlp)
    rope_tables = _rope_tables(max(x_prompt.shape[1], x_sample.shape[1]))
    outs = []
    for x, mem in ((x_prompt, mem_prompt), (x_sample, mem_sample)):
        for l in range(w_in.shape[0]):
            x = _layer(x, mem, rope_tables, *(w[l] for w in weights))
        outs.append(x)
    return tuple(outs)
```

```python
import functools

import jax
import jax.numpy as jnp
from jax import lax
from jax.experimental import pallas as pl
from jax.experimental.pallas import tpu as pltpu

F32 = jnp.float32
BF16 = jnp.bfloat16

D_MODEL = 1024
ATT_HEADS = 8
ATT_KV_HEADS = 2
ATT_HEAD_DIM = 64
ATT_WINDOW = 128
ROPE_THETA = 500000.0
ROPE_DIM = ATT_HEAD_DIM // 4
DN_HEADS = 4
DN_HEAD_DIM = 128
DN_CONV = 5
X_HEADS = 4
X_HEAD_DIM = 128
D_FF = 4 * D_MODEL
EPS = 1e-6

ATT_Q = ATT_HEADS * ATT_HEAD_DIM
ATT_KV = ATT_KV_HEADS * ATT_HEAD_DIM
DN_W = DN_HEADS * DN_HEAD_DIM
X_W = X_HEADS * X_HEAD_DIM
O_AK = ATT_Q
O_AV = O_AK + ATT_KV
O_DQKV = O_AV + ATT_KV
O_DZ = O_DQKV + 3 * DN_W
O_GATES = O_DZ + DN_W
N_GATES = 4 * DN_HEADS

LANES = 128
TM = 512
TQ = 128
TQ_STEP = 256
CHUNK = 128
PREP_UNROLL = 8
PAD = 8
COPY_ROWS = 512
SUB_TILES = 2
TS = SUB_TILES * TM
FF_BLOCK = 1024
VMEM_LIMIT = 56 * 1024 * 1024

LOG2E = 1.4426950408889634
ATT_Q_SCALE = ATT_HEAD_DIM ** -0.5 * LOG2E

NT_DIMS = (((1,), (1,)), ((), ()))
TN_DIMS = (((0,), (0,)), ((), ()))


def _mm(a, b):
    return jnp.dot(a.astype(BF16), b.astype(BF16), preferred_element_type=F32)


def _mm_nt(a, b):
    return lax.dot_general(a.astype(BF16), b.astype(BF16), NT_DIMS, preferred_element_type=F32)


def _mm_tn(a, b):
    return lax.dot_general(a.astype(BF16), b.astype(BF16), TN_DIMS, preferred_element_type=F32)


def _rms(x, g):
    return x * lax.rsqrt(jnp.mean(x * x, axis=-1, keepdims=True) + EPS) * g


def _softplus(x):
    return jnp.maximum(x, 0.0) + jnp.log1p(jnp.exp(-jnp.abs(x)))


def _silu(x):
    return x * jax.nn.sigmoid(x)


def _iota(shape, dim):
    return lax.broadcasted_iota(jnp.int32, shape, dim)


def _params(*sem):
    return pltpu.CompilerParams(dimension_semantics=sem, vmem_limit_bytes=VMEM_LIMIT)


def _in_proj_kernel(x_ref, g_ref, wq_ref, wk_ref, wv_ref, wd_ref, wz_ref, wgc_ref,
                    cos_ref, sa_ref, sb_ref, pc_ref,
                    aq_ref, ak_ref, avt_ref, dqkv_ref, dz_ref, grow_ref):
    subs = [slice(i * TM, (i + 1) * TM) for i in range(SUB_TILES)]
    hbs = [_rms(x_ref[r, :], g_ref[...]).astype(BF16) for r in subs]

    def rope(t, r):
        return (t * cos_ref[r, :] + pltpu.roll(t, LANES - ROPE_DIM // 2, 1) * sa_ref[r, :]
                + pltpu.roll(t, ROPE_DIM // 2, 1) * sb_ref[r, :])

    for r, hb in zip(subs, hbs):
        q = jnp.dot(hb, wq_ref[...], preferred_element_type=F32)
        for j in range(ATT_Q // LANES):
            sl = slice(j * LANES, (j + 1) * LANES)
            aq_ref[r, sl] = (rope(q[:, sl], r) * ATT_Q_SCALE).astype(BF16)
        ak_ref[r, :] = rope(jnp.dot(hb, wk_ref[...], preferred_element_type=F32), r).astype(BF16)
        avt_ref[:, r] = jnp.dot(hb, wv_ref[...], preferred_element_type=F32).astype(BF16).T
        dqkv_ref[r, :] = jnp.dot(hb, wd_ref[...], preferred_element_type=F32)
        dz_ref[r, :] = jnp.dot(hb, wz_ref[...], preferred_element_type=F32)

        raw = jnp.dot(hb, wgc_ref[...], preferred_element_type=F32)
        lane = _iota(raw.shape, 1)
        gval = -jnp.exp(pc_ref[0:1, :]) * _softplus(raw + pc_ref[1:2, :])
        gates = jnp.where(lane < 2 * DN_HEADS, jax.nn.sigmoid(raw), gval)
        pos = _iota(raw.shape, 0) & (CHUNK - 1)
        cf = gates
        cb = gates
        s = 1
        while s < CHUNK:
            cf = cf + jnp.where(pos >= s, pltpu.roll(cf, s, 0), 0.0)
            cb = cb + jnp.where(pos < CHUNK - s, pltpu.roll(cb, TM - s, 0), 0.0)
            s *= 2
        gcol = jnp.where(lane < 2 * DN_HEADS, gates, jnp.where(lane < 3 * DN_HEADS, cf, cb))
        grow_ref[:, r] = gcol.T[:N_GATES]


def _in_proj(x2d, T, g, wq, wk, wv, wd, wz, wgc, cos, sa, sb, pc):
    BT = x2d.shape[0]
    n_t = T // TS
    row = lambda i: (i, 0)
    const = lambda i: (0, 0)
    tab = lambda i: (i % n_t, 0)
    full = lambda a: pl.BlockSpec(a.shape, const)
    return pl.pallas_call(
        _in_proj_kernel,
        grid=(BT // TS,),
        in_specs=[pl.BlockSpec((TS, D_MODEL), row), full(g), full(wq), full(wk), full(wv), full(wd), full(wz),
                  full(wgc),
                  pl.BlockSpec((TS, LANES), tab), pl.BlockSpec((TS, LANES), tab), pl.BlockSpec((TS, LANES), tab),
                  full(pc)],
        out_specs=[pl.BlockSpec((TS, ATT_Q), row), pl.BlockSpec((TS, ATT_KV), row),
                   pl.BlockSpec((ATT_KV, TS), lambda i: (0, i)),
                   pl.BlockSpec((TS, 3 * DN_W), row), pl.BlockSpec((TS, DN_W), row),
                   pl.BlockSpec((N_GATES, TS), lambda i: (0, i))],
        out_shape=[jax.ShapeDtypeStruct((BT, ATT_Q), BF16), jax.ShapeDtypeStruct((BT, ATT_KV), BF16),
                   jax.ShapeDtypeStruct((ATT_KV, BT), BF16),
                   jax.ShapeDtypeStruct((BT, 3 * DN_W), F32), jax.ShapeDtypeStruct((BT, DN_W), F32),
                   jax.ShapeDtypeStruct((N_GATES, BT), F32)],
        compiler_params=_params("parallel"),
        name="in_proj",
    )(x2d, g, wq, wk, wv, wd, wz, wgc, cos, sa, sb, pc)


def _win_attn_kernel(sink_ref, q_ref, k_ref, vt_ref, o_ref, *, T):
    W = TQ + 2 * ATT_WINDOW
    hpg = ATT_HEADS // ATT_KV_HEADS
    rel = _iota((W, TQ), 0) - _iota((W, TQ), 1)
    items = []
    for sub in range(TQ_STEP // TQ):
        t0 = pl.program_id(1) * TQ_STEP + sub * TQ
        start = pl.multiple_of(jnp.clip(t0 - ATT_WINDOW, 0, T - W), LANES)
        k = k_ref[pl.ds(start, W), :]
        vt = vt_ref[:, pl.ds(start, W)]
        k_lo = _iota(k.shape, 1) < ATT_HEAD_DIM
        v_lo = _iota(vt.shape, 0) < ATT_HEAD_DIM
        kk = jnp.concatenate([jnp.where(k_lo, k, jnp.zeros_like(k)), jnp.where(k_lo, jnp.zeros_like(k), k)], axis=0)
        vvt = jnp.concatenate([jnp.where(v_lo, vt, jnp.zeros_like(vt)), jnp.where(v_lo, jnp.zeros_like(vt), vt)],
                              axis=1)
        cap = jnp.where(jnp.abs(rel + (start - t0)) <= ATT_WINDOW, jnp.inf, -1e30)
        cap = jnp.concatenate([cap, cap], axis=1)
        rows = slice(sub * TQ, (sub + 1) * TQ)
        for pair in range(ATT_Q // LANES // 2):
            groups = (2 * pair, 2 * pair + 1)
            qp = jnp.concatenate([q_ref[rows, g * LANES:(g + 1) * LANES] for g in groups], axis=0)
            items.append(dict(kk=kk, vvt=vvt, cap=cap, qp=qp, rows=rows, groups=groups))
    for it in items:
        it["st"] = lax.dot_general(it["kk"], it["qp"], NT_DIMS, preferred_element_type=F32)
    for it in items:
        ps, dens = [], []
        for kvh in range(ATT_KV_HEADS):
            sh = jnp.minimum(it["st"][kvh * W:(kvh + 1) * W], it["cap"])
            sk = jnp.concatenate([jnp.full((1, TQ), sink_ref[kvh * hpg + g] * LOG2E, F32) for g in it["groups"]],
                                 axis=1)
            m = jnp.maximum(jnp.max(sh, axis=0, keepdims=True), sk)
            p = jnp.exp2(sh - m)
            dens.append(jnp.sum(p, axis=0, keepdims=True) + jnp.exp2(sk - m))
            ps.append(p.astype(BF16))
        it["p"] = jnp.concatenate(ps, axis=0)
        it["dens"] = dens
    for it in items:
        ot = jnp.dot(it["vvt"], it["p"], preferred_element_type=F32)
        ot = ot / jnp.where(_iota(ot.shape, 0) < ATT_HEAD_DIM, it["dens"][0], it["dens"][1])
        for i, g in enumerate(it["groups"]):
            o_ref[g * LANES:(g + 1) * LANES, it["rows"]] = ot[:, i * TQ:(i + 1) * TQ].astype(BF16)


def _win_attn(aq, ak, avt, sink, B, T):
    n_q = T // TQ_STEP
    return pl.pallas_call(
        functools.partial(_win_attn_kernel, T=T),
        grid=(B, n_q),
        in_specs=[pl.BlockSpec(memory_space=pltpu.SMEM),
                  pl.BlockSpec((TQ_STEP, ATT_Q), lambda b, i: (b * n_q + i, 0)),
                  pl.BlockSpec((T, ATT_KV), lambda b, i: (b, 0)),
                  pl.BlockSpec((ATT_KV, T), lambda b, i: (0, b))],
        out_specs=pl.BlockSpec((ATT_Q, TQ_STEP), lambda b, i: (0, b * n_q + i)),
        out_shape=jax.ShapeDtypeStruct((ATT_Q, B * T), BF16),
        compiler_params=_params("parallel", "arbitrary"),
        name="win_attn",
    )(sink, aq, ak, avt)


def _mm_pair(xb, yb):
    return jnp.concatenate([jnp.dot(xb[:, :CHUNK], yb[:, :CHUNK], preferred_element_type=F32),
                            jnp.dot(xb[:, CHUNK:], yb[:, CHUNK:], preferred_element_type=F32)], axis=1)


def _unit_tri_inverse_pairs(l2s):
    bf = lambda xs: [x.astype(BF16) for x in xs]
    r = _iota((CHUNK, 2 * CHUNK), 0)
    c = _iota((CHUNK, 2 * CHUNK), 1) & (CHUNK - 1)
    base = 16
    in_base = (r // base) == (c // base)
    ds = [jnp.where(in_base, l2, 0.0) for l2 in l2s]
    ps = [jnp.where(r == c, 1.0, 0.0) - d for d in ds]
    powb = bf(ds)
    for _ in range(3):
        powb = bf([_mm_pair(x, x) for x in powb])
        ps = [p + _mm_pair(pb, x) for p, pb, x in zip(ps, bf(ps), powb)]
    s = base
    while s < CHUNK:
        off_diag = ((r // (2 * s)) == (c // (2 * s))) & ((r // s) != (c // s))
        mbs = bf([jnp.where(off_diag, l2, 0.0) for l2 in l2s])
        pbs = bf(ps)
        tbs = bf([_mm_pair(mb, pb) for mb, pb in zip(mbs, pbs)])
        ps = [p - _mm_pair(pb, tb) for p, pb, tb in zip(ps, pbs, tbs)]
        s *= 2
    return ps


def _deltanet_kernel(q_ref, k_ref, v_ref, z_ref, grow_ref, cwq_ref, cwk_ref, cwv_ref, nw_ref,
                     o_ref, pad_scr, u_scr, w_scr, qg_scr, at_scr, gl_scr, ku_scr, kw_scr, sp_scr, *, T):
    C = CHUNK
    N = T // C
    U = PREP_UNROLL
    h = pl.program_id(1)
    ri = _iota((C, C), 0)
    ci = _iota((C, C), 1)
    incl = (ri >= ci, ri <= ci)
    strict = (ri > ci, ri < ci)
    lane = _iota((C, LANES), 1)

    zpad = jnp.zeros((PAD, DN_HEAD_DIM), F32)
    for a, ref in enumerate((q_ref, k_ref, v_ref)):
        pad_scr[a, 0:PAD, :] = zpad
        pad_scr[a, T + PAD:T + 2 * PAD, :] = zpad

        def copy_rows(i, carry, a=a, ref=ref):
            r0 = pl.multiple_of(i * COPY_ROWS, COPY_ROWS)
            pad_scr[a, pl.ds(pl.multiple_of(r0 + PAD, PAD), COPY_ROWS), :] = ref[pl.ds(r0, COPY_ROWS), :]
            return carry

        lax.fori_loop(0, T // COPY_ROWS, copy_rows, 0)

    def conv_silu(a, cw_ref, c0):
        win = pad_scr.at[a, pl.ds(c0, C + 2 * PAD), :]
        off = PAD - DN_CONV // 2
        acc = win[off:off + C, :] * cw_ref[0:1, :]
        for j in range(1, DN_CONV):
            acc = acc + win[off + j:off + j + C, :] * cw_ref[j:j + 1, :]
        return _silu(acc)

    def l2norm(t):
        return t * lax.rsqrt(jnp.sum(t * t, axis=-1, keepdims=True) + EPS)

    def chunk_start(n):
        return n * C if isinstance(n, int) else pl.multiple_of(n * C, C)

    def prepare_chunks(ns):
        c0s = [chunk_start(n) for n in ns]
        qs = [l2norm(conv_silu(0, cwq_ref, c0)) * (DN_HEAD_DIM ** -0.5) for c0 in c0s]
        ks = [l2norm(conv_silu(1, cwk_ref, c0)) for c0 in c0s]
        vs = [conv_silu(2, cwv_ref, c0) for c0 in c0s]
        kbs = [k.astype(BF16) for k in ks]
        grams = [lax.dot_general(jnp.concatenate([q.astype(BF16), kb], axis=0), kb, NT_DIMS,
                                 preferred_element_type=F32) for q, kb in zip(qs, kbs)]
        l2s, rhss, kds = [], [], []
        for u in range(U):
            rows = pl.ds(c0s[u], C)
            q, k, v = qs[u], ks[u], vs[u]
            gr = grow_ref[:, rows]
            gt = jnp.concatenate([gr, jnp.zeros((LANES - N_GATES, C), F32)], axis=0).T
            col = lambda idx: jnp.sum(jnp.where(lane == idx, gt, 0.0), axis=1, keepdims=True)
            beta = (col(h), col(DN_HEADS + h))
            gcc = (col(2 * DN_HEADS + h), col(3 * DN_HEADS + h))
            sub = _iota(gr.shape, 0)
            row = lambda idx: jnp.sum(jnp.where(sub == idx, gr, 0.0), axis=0, keepdims=True)
            gcr = (row(2 * DN_HEADS + h), row(3 * DN_HEADS + h))
            glast = (gcr[0][:, C - 1:C], gcr[1][:, 0:1])
            qk = grams[u][:C]
            kk = grams[u][C:]
            l2 = []
            for d in range(2):
                decay = jnp.where(incl[d], jnp.exp(jnp.where(incl[d], gcc[d] - gcr[d], 0.0)), 0.0)
                l2.append(jnp.where(strict[d], beta[d] * kk * decay, 0.0))
                eg = jnp.exp(gcc[d])
                qg_scr[d, rows, :] = (q * eg).astype(BF16)
                kds.append((k * jnp.exp(glast[d] - gcc[d])).astype(BF16))
                at_scr[d, rows, :] = (qk * decay).astype(BF16)
                gl_scr[d, ns[u]] = jnp.broadcast_to(jnp.exp(glast[d]), (8, LANES))
                rhss.append(jnp.concatenate([v * beta[d], (k * beta[d]) * eg], axis=1).astype(BF16))
            l2s.append(jnp.concatenate(l2, axis=1))
        ainvs = _unit_tri_inverse_pairs(l2s)
        for u in range(U):
            rows = pl.ds(c0s[u], C)
            for d in range(2):
                uw = jnp.dot(ainvs[u][:, d * C:(d + 1) * C].astype(BF16), rhss[2 * u + d],
                             preferred_element_type=F32)
                uwb = uw.astype(BF16)
                u_scr[d, rows, :] = uw[:, :DN_HEAD_DIM]
                w_scr[d, rows, :] = uwb[:, DN_HEAD_DIM:]
                kuw = lax.dot_general(kds[2 * u + d], uwb, TN_DIMS, preferred_element_type=F32)
                ku_scr[d, rows, :] = kuw[:, :DN_HEAD_DIM]
                kw_scr[d, rows, :] = kuw[:, DN_HEAD_DIM:].astype(BF16)

    def scan_step(d, n, state):
        rows = pl.ds(chunk_start(n), C)
        sb = state.astype(BF16)
        sp_scr[d, rows, :] = sb
        return (state * gl_scr[d, n][0:1, :] + ku_scr[d, rows, :]
                - jnp.dot(kw_scr[d, rows, :], sb, preferred_element_type=F32))

    def scan_steps(i0, states):
        for j in range(half):
            i = i0 + j
            states = (scan_step(0, i, states[0]), scan_step(1, N - 1 - i, states[1]))
        return states

    def output_chunks(ns):
        rows = [pl.ds(chunk_start(n), C) for n in ns]
        items = [(r, d) for r in rows for d in range(2)]
        sbs = [sp_scr[d, r, :] for r, d in items]
        ws_qs = [jnp.dot(jnp.concatenate([w_scr[d, r, :], qg_scr[d, r, :]], axis=0), sb,
                         preferred_element_type=F32) for (r, d), sb in zip(items, sbs)]
        v_news = [(u_scr[d, r, :] - x[:C]).astype(BF16) for (r, d), x in zip(items, ws_qs)]
        os_ = [x[C:] + jnp.dot(at_scr[d, r, :], vn, preferred_element_type=F32)
               for (r, d), x, vn in zip(items, ws_qs, v_news)]
        for u, r in enumerate(rows):
            o = os_[2 * u] + os_[2 * u + 1]
            o = o * lax.rsqrt(jnp.mean(o * o, axis=-1, keepdims=True) + EPS) * nw_ref[...]
            o_ref[r, :] = (o * _silu(z_ref[r, :])).astype(BF16)

    half = U // 2
    G = N // U
    front_back = lambda g: ([g * half + j for j in range(half)]
                            + [N - (g + 1) * half + j for j in range(half)])
    finished = lambda g: ([N // 2 + g * half + j for j in range(half)]
                          + [N // 2 - 1 - g * half - j for j in range(half)])
    s0 = jnp.zeros((DN_HEAD_DIM, DN_HEAD_DIM), F32)
    prepare_chunks(front_back(0))

    def prepare_and_scan(g, states):
        states = scan_steps((g - 1) * half, states)
        prepare_chunks(front_back(g))
        return states

    states = lax.fori_loop(1, G, prepare_and_scan, (s0, s0))
    states = scan_steps((G - 1) * half, states)
    states = scan_steps(N // 2, states)

    def scan_and_output(g, states):
        states = scan_steps(N // 2 + g * half, states)
        output_chunks(finished(g - 1))
        return states

    lax.fori_loop(1, G, scan_and_output, states)
    output_chunks(finished(G - 1))


def _deltanet(dqkv, dz, grow, conv_w, norm_w, B, T):
    H = DN_HEADS
    N = T // CHUNK
    seq = lambda off: pl.BlockSpec((T, DN_HEAD_DIM), lambda b, h: (b, off + h))
    cw = lambda off: pl.BlockSpec((DN_CONV, DN_HEAD_DIM), lambda b, h: (0, off + h))
    return pl.pallas_call(
        functools.partial(_deltanet_kernel, T=T),
        grid=(B, H),
        in_specs=[seq(0), seq(H), seq(2 * H), seq(0),
                  pl.BlockSpec((N_GATES, T), lambda b, h: (0, b)),
                  cw(0), cw(H), cw(2 * H),
                  pl.BlockSpec((1, DN_HEAD_DIM), lambda b, h: (0, 0))],
        out_specs=seq(0),
        out_shape=jax.ShapeDtypeStruct((B * T, DN_W), BF16),
        scratch_shapes=[pltpu.VMEM((3, T + 2 * PAD, DN_HEAD_DIM), F32),
                        pltpu.VMEM((2, T, DN_HEAD_DIM), F32),
                        pltpu.VMEM((2, T, DN_HEAD_DIM), BF16),
                        pltpu.VMEM((2, T, DN_HEAD_DIM), BF16),
                        pltpu.VMEM((2, T, CHUNK), BF16),
                        pltpu.VMEM((2, N, 8, LANES), F32),
                        pltpu.VMEM((2, T, DN_HEAD_DIM), F32),
                        pltpu.VMEM((2, T, DN_HEAD_DIM), BF16),
                        pltpu.VMEM((2, T, DN_HEAD_DIM), BF16)],
        compiler_params=_params("parallel", "arbitrary"),
        name="deltanet",
    )(dqkv, dqkv, dqkv, dz, grow, conv_w, conv_w, conv_w, norm_w)


def _out_proj_kernel(att_t_ref, dn_ref, x_ref, woa_ref, wod_ref, g1_ref, g2_ref, wq_ref, x1_ref, qx_ref):
    subs = [slice(i * TM, (i + 1) * TM) for i in range(SUB_TILES)]
    mixes = [lax.dot_general(att_t_ref[:, r], woa_ref[...], TN_DIMS, preferred_element_type=F32)
             + jnp.dot(dn_ref[r, :], wod_ref[...], preferred_element_type=F32) for r in subs]
    x1s = [x_ref[r, :] + _rms(mix, g1_ref[...]) for r, mix in zip(subs, mixes)]
    hs = [_rms(x1, g2_ref[...]).astype(BF16) for x1 in x1s]
    for r, x1, hb in zip(subs, x1s, hs):
        x1_ref[r, :] = x1
        qx_ref[r, :] = jnp.dot(hb, wq_ref[...], preferred_element_type=F32).astype(BF16)


def _out_proj(att_t, dn, x2d, woa, wod, g1, g2, wq):
    BT = x2d.shape[0]
    row = lambda i: (i, 0)
    full = lambda a: pl.BlockSpec(a.shape, lambda i: (0, 0))
    return pl.pallas_call(
        _out_proj_kernel,
        grid=(BT // TS,),
        in_specs=[pl.BlockSpec((ATT_Q, TS), lambda i: (0, i)), pl.BlockSpec((TS, DN_W), row),
                  pl.BlockSpec((TS, D_MODEL), row),
                  full(woa), full(wod), full(g1), full(g2), full(wq)],
        out_specs=[pl.BlockSpec((TS, D_MODEL), row), pl.BlockSpec((TS, X_W), row)],
        out_shape=[jax.ShapeDtypeStruct((BT, D_MODEL), F32), jax.ShapeDtypeStruct((BT, X_W), BF16)],
        compiler_params=_params("parallel"),
        name="out_proj",
    )(att_t, dn, x2d, woa, wod, g1, g2, wq)


def _mem_kv_kernel(mem_ref, g_ref, w_ref, kv_ref):
    kv_ref[...] = _mm(_rms(mem_ref[...], g_ref[...]), w_ref[...]).astype(BF16)


def _mem_kv(mem2d, g, wkv):
    M = mem2d.shape[0]
    mem_len = 256
    return pl.pallas_call(
        _mem_kv_kernel,
        grid=(M // mem_len,),
        in_specs=[pl.BlockSpec((mem_len, D_MODEL), lambda i: (i, 0)),
                  pl.BlockSpec(g.shape, lambda i: (0, 0)), pl.BlockSpec(wkv.shape, lambda i: (0, 0))],
        out_specs=pl.BlockSpec((mem_len, 2 * X_W), lambda i: (i, 0)),
        out_shape=jax.ShapeDtypeStruct((M, 2 * X_W), BF16),
        compiler_params=_params("parallel"),
        name="mem_kv",
    )(mem2d, g, wkv)


def _xattn_kernel(qx_ref, kv_ref, x1_ref, wo_ref, g_ref, x2_ref):
    subs = [slice(i * TM, (i + 1) * TM) for i in range(SUB_TILES)]
    items = [(r, hd) for r in subs for hd in range(X_HEADS)]
    ksl = lambda hd: slice(hd * X_HEAD_DIM, (hd + 1) * X_HEAD_DIM)
    vsl = lambda hd: slice(X_W + hd * X_HEAD_DIM, X_W + (hd + 1) * X_HEAD_DIM)
    ss = [lax.dot_general(qx_ref[r, ksl(hd)], kv_ref[:, ksl(hd)], NT_DIMS, preferred_element_type=F32)
          * (X_HEAD_DIM ** -0.5) for r, hd in items]
    ps = [jnp.exp(s - jnp.max(s, axis=-1, keepdims=True)) for s in ss]
    dens = [jnp.sum(p, axis=-1, keepdims=True) for p in ps]
    os_ = [jnp.dot(p.astype(BF16), kv_ref[:, vsl(hd)], preferred_element_type=F32) / den
           for (r, hd), p, den in zip(items, ps, dens)]
    for i, r in enumerate(subs):
        o = jnp.concatenate(os_[i * X_HEADS:(i + 1) * X_HEADS], axis=1)
        x2_ref[r, :] = x1_ref[r, :] + _rms(_mm(o, wo_ref[...]), g_ref[...])


def _xattn(qx, kv, x1, wo, g, B, T):
    n_t = T // TS
    mem_len = kv.shape[0] // B
    row = lambda b, i: (b * n_t + i, 0)
    return pl.pallas_call(
        _xattn_kernel,
        grid=(B, n_t),
        in_specs=[pl.BlockSpec((TS, X_W), row), pl.BlockSpec((mem_len, 2 * X_W), lambda b, i: (b, 0)),
                  pl.BlockSpec((TS, D_MODEL), row),
                  pl.BlockSpec(wo.shape, lambda b, i: (0, 0)), pl.BlockSpec(g.shape, lambda b, i: (0, 0))],
        out_specs=pl.BlockSpec((TS, D_MODEL), row),
        out_shape=jax.ShapeDtypeStruct((B * T, D_MODEL), F32),
        compiler_params=_params("parallel", "arbitrary"),
        name="xattn",
    )(qx, kv, x1, wo, g)


def _mlp_kernel(x_ref, g1_ref, w1_ref, w2_ref, g2_ref, y_ref):
    x = x_ref[...]
    hb = _rms(x, g1_ref[...]).astype(BF16)
    acc = jnp.zeros((TM, D_MODEL), F32)
    for c in range(D_FF // FF_BLOCK):
        sl = slice(c * FF_BLOCK, (c + 1) * FF_BLOCK)
        a = jnp.maximum(jnp.dot(hb, w1_ref[:, sl], preferred_element_type=F32), 0.0)
        acc = acc + jnp.dot((a * a).astype(BF16), w2_ref[sl, :], preferred_element_type=F32)
    y_ref[...] = x + _rms(acc, g2_ref[...])


def _mlp(x2, g1, w1, w2, g2):
    BT = x2.shape[0]
    row = lambda i: (i, 0)
    full = lambda a: pl.BlockSpec(a.shape, lambda i: (0, 0))
    return pl.pallas_call(
        _mlp_kernel,
        grid=(BT // TM,),
        in_specs=[pl.BlockSpec((TM, D_MODEL), row), full(g1), full(w1), full(w2), full(g2)],
        out_specs=pl.BlockSpec((TM, D_MODEL), row),
        out_shape=jax.ShapeDtypeStruct((BT, D_MODEL), F32),
        compiler_params=_params("parallel"),
        name="mlp",
    )(x2, g1, w1, w2, g2)


def _rope_tables(T):
    half = ROPE_DIM // 2
    inv = ROPE_THETA ** (-(jnp.arange(half, dtype=F32) * 2.0 / ROPE_DIM))
    d = jnp.arange(LANES) % ATT_HEAD_DIM
    ang = jnp.arange(T).astype(F32)[:, None] * inv[d % half][None, :]
    cos, sin = jnp.cos(ang), jnp.sin(ang)
    lo, hi = (d < half)[None, :], ((d >= half) & (d < ROPE_DIM))[None, :]
    return jnp.where(lo | hi, cos, 1.0), jnp.where(lo, -sin, 0.0), jnp.where(hi, sin, 0.0)


def _gate_params(a_log_f, a_log_b, dt_f, dt_b):
    zeros = jnp.zeros((2 * DN_HEADS,), F32)
    a_vec = jnp.concatenate([zeros, a_log_f.astype(F32), a_log_b.astype(F32)])
    d_vec = jnp.concatenate([zeros, dt_f.astype(F32), dt_b.astype(F32)])
    pad = jnp.zeros((LANES - N_GATES,), F32)
    return jnp.stack([jnp.concatenate([a_vec, pad]), jnp.concatenate([d_vec, pad])])


def _layer(x, mem, rope_tables, w_in, attn_sink, conv_w, a_log_f, a_log_b, dt_f, dt_b, dn_norm_w, w_out,
           xa_wq, xa_wkv, xa_wo, mem_g, w1, w2, g_pre_mix, g_post_mix, g_pre_xa, g_post_xa,
           g_pre_mlp, g_post_mlp):
    B, T, _ = x.shape
    assert T % TS == 0 and T % (PREP_UNROLL * CHUNK) == 0 and T >= TQ + 2 * ATT_WINDOW
    x2d = x.reshape(B * T, D_MODEL)
    vec = lambda g: g.astype(F32).reshape(1, -1)
    hpg = ATT_HEADS // ATT_KV_HEADS

    def pair_heads(w, axis):
        shape = w.shape
        w = w.reshape(shape[:axis] + (ATT_KV_HEADS, hpg, ATT_HEAD_DIM) + shape[axis + 1:])
        return jnp.swapaxes(w, axis, axis + 1).reshape(shape)

    wq = pair_heads(w_in[:, :ATT_Q], 1).astype(BF16)
    wk = w_in[:, O_AK:O_AV].astype(BF16)
    wv = w_in[:, O_AV:O_DQKV].astype(BF16)
    wd = w_in[:, O_DQKV:O_DZ].astype(BF16)
    wz = w_in[:, O_DZ:O_GATES].astype(BF16)
    wg = w_in[:, O_GATES:O_GATES + N_GATES]
    wgc = jnp.pad(wg, ((0, 0), (0, LANES - N_GATES))).astype(BF16)
    cos, sa, sb = rope_tables
    pc = _gate_params(a_log_f, a_log_b, dt_f, dt_b)

    aq, ak, avt, dqkv, dz, grow = _in_proj(x2d, T, vec(g_pre_mix), wq, wk, wv, wd, wz, wgc, cos, sa, sb, pc)
    att_t = _win_attn(aq, ak, avt, attn_sink.astype(F32), B, T)
    dn = _deltanet(dqkv, dz, grow, conv_w.astype(F32), vec(dn_norm_w), B, T)
    x1, qx = _out_proj(att_t, dn, x2d, pair_heads(w_out[:ATT_Q], 0).astype(BF16), w_out[ATT_Q:].astype(BF16),
                       vec(g_post_mix), vec(g_pre_xa), xa_wq.astype(BF16))
    kv = _mem_kv(mem.reshape(-1, D_MODEL), vec(mem_g), xa_wkv.astype(BF16))
    x2 = _xattn(qx, kv, x1, xa_wo.astype(BF16), vec(g_post_xa), B, T)
    y = _mlp(x2, vec(g_pre_mlp), w1.astype(BF16), w2.astype(BF16), vec(g_post_mlp))
    return y.reshape(B, T, D_MODEL)


def kernel(x_prompt, x_sample, mem_prompt, mem_sample, w_in, attn_sink, dn_conv_w, dn_A_log_f, dn_A_log_b,
           dn_dt_bias_f, dn_dt_bias_b, dn_norm_w, w_out, xa_wq, xa_wkv, xa_wo, mem_norm_g, mlp_w1, mlp_w2,
           norm_pre_mix, norm_post_mix, norm_pre_xa, norm_post_xa, norm_pre_mlp, norm_post_mlp):
    weights = (w_in, attn_sink, dn_conv_w, dn_A_log_f, dn_A_log_b, dn_dt_bias_f, dn_dt_bias_b, dn_norm_w,
               w_out, xa_wq, xa_wkv, xa_wo, mem_norm_g, mlp_w1, mlp_w2, norm_pre_mix, norm_post_mix,
               norm_pre_xa, norm_post_xa, norm_pre_mlp, norm_post_mlp)
    rope_tables = _rope_tables(max(x_prompt.shape[1], x_sample.shape[1]))
    outs = []
    for x, mem in ((x_prompt, mem_prompt), (x_sample, mem_sample)):
        for l in range(w_in.shape[0]):
            x = _layer(x, mem, rope_tables, *(w[l] for w in weights))
        outs.append(x)
    return tuple(outs)
```

```python
import functools

import jax
import jax.numpy as jnp
from jax import lax
from jax.experimental import pallas as pl
from jax.experimental.pallas import tpu as pltpu

F32 = jnp.float32
BF16 = jnp.bfloat16

D_MODEL = 1024
ATT_HEADS = 8
ATT_KV_HEADS = 2
ATT_HEAD_DIM = 64
ATT_WINDOW = 128
ROPE_THETA = 500000.0
ROPE_DIM = ATT_HEAD_DIM // 4
DN_HEADS = 4
DN_HEAD_DIM = 128
DN_CONV = 5
X_HEADS = 4
X_HEAD_DIM = 128
D_FF = 4 * D_MODEL
EPS = 1e-6

ATT_Q = ATT_HEADS * ATT_HEAD_DIM
ATT_KV = ATT_KV_HEADS * ATT_HEAD_DIM
DN_W = DN_HEADS * DN_HEAD_DIM
X_W = X_HEADS * X_HEAD_DIM
O_AK = ATT_Q
O_AV = O_AK + ATT_KV
O_DQKV = O_AV + ATT_KV
O_DZ = O_DQKV + 3 * DN_W
O_GATES = O_DZ + DN_W
N_GATES = 4 * DN_HEADS

LANES = 128
TM = 512
TQ = 128
TQ_STEP = 256
CHUNK = 128
PREP_UNROLL = 8
PAD = 8
COPY_ROWS = 512
SUB_TILES = 2
TS = SUB_TILES * TM
FF_BLOCK = 1024
VMEM_LIMIT = 56 * 1024 * 1024

LOG2E = 1.4426950408889634
ATT_Q_SCALE = ATT_HEAD_DIM ** -0.5 * LOG2E

NT_DIMS = (((1,), (1,)), ((), ()))
TN_DIMS = (((0,), (0,)), ((), ()))


def _mm(a, b):
    return jnp.dot(a.astype(BF16), b.astype(BF16), preferred_element_type=F32)


def _mm_nt(a, b):
    return lax.dot_general(a.astype(BF16), b.astype(BF16), NT_DIMS, preferred_element_type=F32)


def _mm_tn(a, b):
    return lax.dot_general(a.astype(BF16), b.astype(BF16), TN_DIMS, preferred_element_type=F32)


def _rms(x, g):
    return x * lax.rsqrt(jnp.mean(x * x, axis=-1, keepdims=True) + EPS) * g


def _softplus(x):
    return jnp.maximum(x, 0.0) + jnp.log1p(jnp.exp(-jnp.abs(x)))


def _silu(x):
    return x * jax.nn.sigmoid(x)


def _iota(shape, dim):
    return lax.broadcasted_iota(jnp.int32, shape, dim)


def _params(*sem):
    return pltpu.CompilerParams(dimension_semantics=sem, vmem_limit_bytes=VMEM_LIMIT)


def _in_proj_kernel(x_ref, g_ref, w_ref, cos_ref, sa_ref, sb_ref, pc_ref,
                    aq_ref, ak_ref, avt_ref, dqkv_ref, dz_ref, grow_ref):
    subs = [slice(i * TM, (i + 1) * TM) for i in range(SUB_TILES)]
    hbs = [_rms(x_ref[r, :], g_ref[...]).astype(BF16) for r in subs]

    def rope(t, r):
        return (t * cos_ref[r, :] + pltpu.roll(t, LANES - ROPE_DIM // 2, 1) * sa_ref[r, :]
                + pltpu.roll(t, ROPE_DIM // 2, 1) * sb_ref[r, :])

    for r, hb in zip(subs, hbs):
        qkv = jnp.dot(hb, w_ref[:, :O_DQKV], preferred_element_type=F32)
        for j in range(ATT_Q // LANES):
            sl = slice(j * LANES, (j + 1) * LANES)
            aq_ref[r, sl] = (rope(qkv[:, sl], r) * ATT_Q_SCALE).astype(BF16)
        ak_ref[r, :] = rope(qkv[:, O_AK:O_AV], r).astype(BF16)
        avt_ref[:, r] = qkv[:, O_AV:O_DQKV].astype(BF16).T
        dqkv_ref[r, :] = jnp.dot(hb, w_ref[:, O_DQKV:O_DZ], preferred_element_type=F32)
        zg = jnp.dot(hb, w_ref[:, O_DZ:], preferred_element_type=F32)
        dz_ref[r, :] = zg[:, :DN_W]

        raw = zg[:, DN_W:]
        lane = _iota(raw.shape, 1)
        gval = -jnp.exp(pc_ref[0:1, :]) * _softplus(raw + pc_ref[1:2, :])
        gates = jnp.where(lane < 2 * DN_HEADS, jax.nn.sigmoid(raw), gval)
        pos = _iota(raw.shape, 0) & (CHUNK - 1)
        cf = gates
        cb = gates
        s = 1
        while s < CHUNK:
            cf = cf + jnp.where(pos >= s, pltpu.roll(cf, s, 0), 0.0)
            cb = cb + jnp.where(pos < CHUNK - s, pltpu.roll(cb, TM - s, 0), 0.0)
            s *= 2
        gcol = jnp.where(lane < 2 * DN_HEADS, gates, jnp.where(lane < 3 * DN_HEADS, cf, cb))
        grow_ref[:, r] = gcol.T[:N_GATES]


def _in_proj(x2d, T, g, w, cos, sa, sb, pc):
    BT = x2d.shape[0]
    n_t = T // TS
    row = lambda i: (i, 0)
    const = lambda i: (0, 0)
    tab = lambda i: (i % n_t, 0)
    full = lambda a: pl.BlockSpec(a.shape, const)
    return pl.pallas_call(
        _in_proj_kernel,
        grid=(BT // TS,),
        in_specs=[pl.BlockSpec((TS, D_MODEL), row), full(g), full(w),
                  pl.BlockSpec((TS, LANES), tab), pl.BlockSpec((TS, LANES), tab), pl.BlockSpec((TS, LANES), tab),
                  full(pc)],
        out_specs=[pl.BlockSpec((TS, ATT_Q), row), pl.BlockSpec((TS, ATT_KV), row),
                   pl.BlockSpec((ATT_KV, TS), lambda i: (0, i)),
                   pl.BlockSpec((TS, 3 * DN_W), row), pl.BlockSpec((TS, DN_W), row),
                   pl.BlockSpec((N_GATES, TS), lambda i: (0, i))],
        out_shape=[jax.ShapeDtypeStruct((BT, ATT_Q), BF16), jax.ShapeDtypeStruct((BT, ATT_KV), BF16),
                   jax.ShapeDtypeStruct((ATT_KV, BT), BF16),
                   jax.ShapeDtypeStruct((BT, 3 * DN_W), F32), jax.ShapeDtypeStruct((BT, DN_W), F32),
                   jax.ShapeDtypeStruct((N_GATES, BT), F32)],
        compiler_params=_params("parallel"),
        name="in_proj",
    )(x2d, g, w, cos, sa, sb, pc)


def _win_attn_kernel(sink_ref, q_ref, k_ref, vt_ref, o_ref, *, T):
    W = TQ + 2 * ATT_WINDOW
    hpg = ATT_HEADS // ATT_KV_HEADS
    rel = _iota((W, TQ), 0) - _iota((W, TQ), 1)
    items = []
    for sub in range(TQ_STEP // TQ):
        t0 = pl.program_id(1) * TQ_STEP + sub * TQ
        start = pl.multiple_of(jnp.clip(t0 - ATT_WINDOW, 0, T - W), LANES)
        k = k_ref[pl.ds(start, W), :]
        vt = vt_ref[:, pl.ds(start, W)]
        k_lo = _iota(k.shape, 1) < ATT_HEAD_DIM
        v_lo = _iota(vt.shape, 0) < ATT_HEAD_DIM
        kk = jnp.concatenate([jnp.where(k_lo, k, jnp.zeros_like(k)), jnp.where(k_lo, jnp.zeros_like(k), k)], axis=0)
        vvt = jnp.concatenate([jnp.where(v_lo, vt, jnp.zeros_like(vt)), jnp.where(v_lo, jnp.zeros_like(vt), vt)],
                              axis=1)
        cap = jnp.where(jnp.abs(rel + (start - t0)) <= ATT_WINDOW, jnp.inf, -1e30)
        cap = jnp.concatenate([cap, cap], axis=1)
        rows = slice(sub * TQ, (sub + 1) * TQ)
        for pair in range(ATT_Q // LANES // 2):
            groups = (2 * pair, 2 * pair + 1)
            qp = jnp.concatenate([q_ref[rows, g * LANES:(g + 1) * LANES] for g in groups], axis=0)
            items.append(dict(kk=kk, vvt=vvt, cap=cap, qp=qp, rows=rows, groups=groups))
    for it in items:
        it["st"] = lax.dot_general(it["kk"], it["qp"], NT_DIMS, preferred_element_type=F32)
    for it in items:
        ps, dens = [], []
        for kvh in range(ATT_KV_HEADS):
            sh = jnp.minimum(it["st"][kvh * W:(kvh + 1) * W], it["cap"])
            sk = jnp.concatenate([jnp.full((1, TQ), sink_ref[kvh * hpg + g] * LOG2E, F32) for g in it["groups"]],
                                 axis=1)
            m = jnp.maximum(jnp.max(sh, axis=0, keepdims=True), sk)
            p = jnp.exp2(sh - m)
            dens.append(jnp.sum(p, axis=0, keepdims=True) + jnp.exp2(sk - m))
            ps.append(p.astype(BF16))
        it["p"] = jnp.concatenate(ps, axis=0)
        it["dens"] = dens
    for it in items:
        ot = jnp.dot(it["vvt"], it["p"], preferred_element_type=F32)
        ot = ot / jnp.where(_iota(ot.shape, 0) < ATT_HEAD_DIM, it["dens"][0], it["dens"][1])
        for i, g in enumerate(it["groups"]):
            o_ref[g * LANES:(g + 1) * LANES, it["rows"]] = ot[:, i * TQ:(i + 1) * TQ].astype(BF16)


def _win_attn(aq, ak, avt, sink, B, T):
    n_q = T // TQ_STEP
    return pl.pallas_call(
        functools.partial(_win_attn_kernel, T=T),
        grid=(B, n_q),
        in_specs=[pl.BlockSpec(memory_space=pltpu.SMEM),
                  pl.BlockSpec((TQ_STEP, ATT_Q), lambda b, i: (b * n_q + i, 0)),
                  pl.BlockSpec((T, ATT_KV), lambda b, i: (b, 0)),
                  pl.BlockSpec((ATT_KV, T), lambda b, i: (0, b))],
        out_specs=pl.BlockSpec((ATT_Q, TQ_STEP), lambda b, i: (0, b * n_q + i)),
        out_shape=jax.ShapeDtypeStruct((ATT_Q, B * T), BF16),
        compiler_params=_params("parallel", "arbitrary"),
        name="win_attn",
    )(sink, aq, ak, avt)


def _mm_pair(xb, yb):
    return jnp.concatenate([jnp.dot(xb[:, :CHUNK], yb[:, :CHUNK], preferred_element_type=F32),
                            jnp.dot(xb[:, CHUNK:], yb[:, CHUNK:], preferred_element_type=F32)], axis=1)


def _unit_tri_inverse_pairs(l2s):
    bf = lambda xs: [x.astype(BF16) for x in xs]
    r = _iota((CHUNK, 2 * CHUNK), 0)
    c = _iota((CHUNK, 2 * CHUNK), 1) & (CHUNK - 1)
    base = 16
    in_base = (r // base) == (c // base)
    ds = [jnp.where(in_base, l2, 0.0) for l2 in l2s]
    ps = [jnp.where(r == c, 1.0, 0.0) - d for d in ds]
    powb = bf(ds)
    for _ in range(3):
        powb = bf([_mm_pair(x, x) for x in powb])
        ps = [p + _mm_pair(pb, x) for p, pb, x in zip(ps, bf(ps), powb)]
    s = base
    while s < CHUNK:
        off_diag = ((r // (2 * s)) == (c // (2 * s))) & ((r // s) != (c // s))
        mbs = bf([jnp.where(off_diag, l2, 0.0) for l2 in l2s])
        pbs = bf(ps)
        tbs = bf([_mm_pair(mb, pb) for mb, pb in zip(mbs, pbs)])
        ps = [p - _mm_pair(pb, tb) for p, pb, tb in zip(ps, pbs, tbs)]
        s *= 2
    return ps


def _deltanet_kernel(q_ref, k_ref, v_ref, z_ref, grow_ref, cwq_ref, cwk_ref, cwv_ref, nw_ref,
                     o_ref, pad_scr, u_scr, w_scr, qg_scr, at_scr, gl_scr, ku_scr, kw_scr, sp_scr, *, T):
    C = CHUNK
    N = T // C
    U = PREP_UNROLL
    h = pl.program_id(1)
    ri = _iota((C, C), 0)
    ci = _iota((C, C), 1)
    incl = (ri >= ci, ri <= ci)
    strict = (ri > ci, ri < ci)
    lane = _iota((C, LANES), 1)

    zpad = jnp.zeros((PAD, DN_HEAD_DIM), F32)
    for a, ref in enumerate((q_ref, k_ref, v_ref)):
        pad_scr[a, 0:PAD, :] = zpad
        pad_scr[a, T + PAD:T + 2 * PAD, :] = zpad

        def copy_rows(i, carry, a=a, ref=ref):
            r0 = pl.multiple_of(i * COPY_ROWS, COPY_ROWS)
            pad_scr[a, pl.ds(pl.multiple_of(r0 + PAD, PAD), COPY_ROWS), :] = ref[pl.ds(r0, COPY_ROWS), :]
            return carry

        lax.fori_loop(0, T // COPY_ROWS, copy_rows, 0)

    def conv_silu(a, cw_ref, c0):
        win = pad_scr.at[a, pl.ds(c0, C + 2 * PAD), :]
        off = PAD - DN_CONV // 2
        acc = win[off:off + C, :] * cw_ref[0:1, :]
        for j in range(1, DN_CONV):
            acc = acc + win[off + j:off + j + C, :] * cw_ref[j:j + 1, :]
        return _silu(acc)

    def l2norm(t):
        return t * lax.rsqrt(jnp.sum(t * t, axis=-1, keepdims=True) + EPS)

    def chunk_start(n):
        return n * C if isinstance(n, int) else pl.multiple_of(n * C, C)

    def prepare_chunks(ns):
        c0s = [chunk_start(n) for n in ns]
        qs = [l2norm(conv_silu(0, cwq_ref, c0)) * (DN_HEAD_DIM ** -0.5) for c0 in c0s]
        ks = [l2norm(conv_silu(1, cwk_ref, c0)) for c0 in c0s]
        vs = [conv_silu(2, cwv_ref, c0) for c0 in c0s]
        kbs = [k.astype(BF16) for k in ks]
        grams = [lax.dot_general(jnp.concatenate([q.astype(BF16), kb], axis=0), kb, NT_DIMS,
                                 preferred_element_type=F32) for q, kb in zip(qs, kbs)]
        l2s, rhss, kds = [], [], []
        for u in range(U):
            rows = pl.ds(c0s[u], C)
            q, k, v = qs[u], ks[u], vs[u]
            gr = grow_ref[:, rows]
            gt = jnp.concatenate([gr, jnp.zeros((LANES - N_GATES, C), F32)], axis=0).T
            col = lambda idx: jnp.sum(jnp.where(lane == idx, gt, 0.0), axis=1, keepdims=True)
            beta = (col(h), col(DN_HEADS + h))
            gcc = (col(2 * DN_HEADS + h), col(3 * DN_HEADS + h))
            sub = _iota(gr.shape, 0)
            row = lambda idx: jnp.sum(jnp.where(sub == idx, gr, 0.0), axis=0, keepdims=True)
            gcr = (row(2 * DN_HEADS + h), row(3 * DN_HEADS + h))
            glast = (gcr[0][:, C - 1:C], gcr[1][:, 0:1])
            qk = grams[u][:C]
            kk = grams[u][C:]
            l2 = []
            for d in range(2):
                decay = jnp.where(incl[d], jnp.exp(jnp.where(incl[d], gcc[d] - gcr[d], 0.0)), 0.0)
                l2.append(jnp.where(strict[d], beta[d] * kk * decay, 0.0))
                eg = jnp.exp(gcc[d])
                qg_scr[d, rows, :] = (q * eg).astype(BF16)
                kds.append((k * jnp.exp(glast[d] - gcc[d])).astype(BF16))
                at_scr[d, rows, :] = (qk * decay).astype(BF16)
                gl_scr[d, ns[u]] = jnp.broadcast_to(jnp.exp(glast[d]), (8, LANES))
                rhss.append(jnp.concatenate([v * beta[d], (k * beta[d]) * eg], axis=1).astype(BF16))
            l2s.append(jnp.concatenate(l2, axis=1))
        ainvs = _unit_tri_inverse_pairs(l2s)
        for u in range(U):
            rows = pl.ds(c0s[u], C)
            for d in range(2):
                uw = jnp.dot(ainvs[u][:, d * C:(d + 1) * C].astype(BF16), rhss[2 * u + d],
                             preferred_element_type=F32)
                uwb = uw.astype(BF16)
                u_scr[d, rows, :] = uw[:, :DN_HEAD_DIM]
                w_scr[d, rows, :] = uwb[:, DN_HEAD_DIM:]
                kuw = lax.dot_general(kds[2 * u + d], uwb, TN_DIMS, preferred_element_type=F32)
                ku_scr[d, rows, :] = kuw[:, :DN_HEAD_DIM]
                kw_scr[d, rows, :] = kuw[:, DN_HEAD_DIM:].astype(BF16)

    def scan_step(d, n, state):
        rows = pl.ds(chunk_start(n), C)
        sb = state.astype(BF16)
        sp_scr[d, rows, :] = sb
        return (state * gl_scr[d, n][0:1, :] + ku_scr[d, rows, :]
                - jnp.dot(kw_scr[d, rows, :], sb, preferred_element_type=F32))

    def scan_steps(i0, states):
        for j in range(half):
            i = i0 + j
            states = (scan_step(0, i, states[0]), scan_step(1, N - 1 - i, states[1]))
        return states

    def output_chunks(ns):
        rows = [pl.ds(chunk_start(n), C) for n in ns]
        items = [(r, d) for r in rows for d in range(2)]
        sbs = [sp_scr[d, r, :] for r, d in items]
        ws_qs = [jnp.dot(jnp.concatenate([w_scr[d, r, :], qg_scr[d, r, :]], axis=0), sb,
                         preferred_element_type=F32) for (r, d), sb in zip(items, sbs)]
        v_news = [(u_scr[d, r, :] - x[:C]).astype(BF16) for (r, d), x in zip(items, ws_qs)]
        os_ = [x[C:] + jnp.dot(at_scr[d, r, :], vn, preferred_element_type=F32)
               for (r, d), x, vn in zip(items, ws_qs, v_news)]
        for u, r in enumerate(rows):
            o = os_[2 * u] + os_[2 * u + 1]
            o = o * lax.rsqrt(jnp.mean(o * o, axis=-1, keepdims=True) + EPS) * nw_ref[...]
            o_ref[r, :] = (o * _silu(z_ref[r, :])).astype(BF16)

    half = U // 2
    G = N // U
    front_back = lambda g: ([g * half + j for j in range(half)]
                            + [N - (g + 1) * half + j for j in range(half)])
    finished = lambda g: ([N // 2 + g * half + j for j in range(half)]
                          + [N // 2 - 1 - g * half - j for j in range(half)])
    s0 = jnp.zeros((DN_HEAD_DIM, DN_HEAD_DIM), F32)
    prepare_chunks(front_back(0))

    def prepare_and_scan(g, states):
        states = scan_steps((g - 1) * half, states)
        prepare_chunks(front_back(g))
        return states

    states = lax.fori_loop(1, G, prepare_and_scan, (s0, s0))
    states = scan_steps((G - 1) * half, states)
    states = scan_steps(N // 2, states)

    def scan_and_output(g, states):
        states = scan_steps(N // 2 + g * half, states)
        output_chunks(finished(g - 1))
        return states

    lax.fori_loop(1, G, scan_and_output, states)
    output_chunks(finished(G - 1))


def _deltanet(dqkv, dz, grow, conv_w, norm_w, B, T):
    H = DN_HEADS
    N = T // CHUNK
    seq = lambda off: pl.BlockSpec((T, DN_HEAD_DIM), lambda b, h: (b, off + h))
    cw = lambda off: pl.BlockSpec((DN_CONV, DN_HEAD_DIM), lambda b, h: (0, off + h))
    return pl.pallas_call(
        functools.partial(_deltanet_kernel, T=T),
        grid=(B, H),
        in_specs=[seq(0), seq(H), seq(2 * H), seq(0),
                  pl.BlockSpec((N_GATES, T), lambda b, h: (0, b)),
                  cw(0), cw(H), cw(2 * H),
                  pl.BlockSpec((1, DN_HEAD_DIM), lambda b, h: (0, 0))],
        out_specs=seq(0),
        out_shape=jax.ShapeDtypeStruct((B * T, DN_W), BF16),
        scratch_shapes=[pltpu.VMEM((3, T + 2 * PAD, DN_HEAD_DIM), F32),
                        pltpu.VMEM((2, T, DN_HEAD_DIM), F32),
                        pltpu.VMEM((2, T, DN_HEAD_DIM), BF16),
                        pltpu.VMEM((2, T, DN_HEAD_DIM), BF16),
                        pltpu.VMEM((2, T, CHUNK), BF16),
                        pltpu.VMEM((2, N, 8, LANES), F32),
                        pltpu.VMEM((2, T, DN_HEAD_DIM), F32),
                        pltpu.VMEM((2, T, DN_HEAD_DIM), BF16),
                        pltpu.VMEM((2, T, DN_HEAD_DIM), BF16)],
        compiler_params=_params("parallel", "arbitrary"),
        name="deltanet",
    )(dqkv, dqkv, dqkv, dz, grow, conv_w, conv_w, conv_w, norm_w)


def _mix_xattn_kernel(att_t_ref, dn_ref, x_ref, mem_ref, woa_ref, wod_ref, g_mix_ref, g_xa_ref, wq_ref,
                      g_mem_ref, wkv_ref, wo_ref, g_out_ref, x2_ref, kv_scr):
    @pl.when(pl.program_id(1) == 0)
    def _():
        kv_scr[...] = _mm(_rms(mem_ref[...], g_mem_ref[...]), wkv_ref[...]).astype(BF16)

    subs = [slice(i * TM, (i + 1) * TM) for i in range(SUB_TILES)]
    mixes = [lax.dot_general(att_t_ref[:, r], woa_ref[...], TN_DIMS, preferred_element_type=F32)
             + jnp.dot(dn_ref[r, :], wod_ref[...], preferred_element_type=F32) for r in subs]
    x1s = [x_ref[r, :] + _rms(mix, g_mix_ref[...]) for r, mix in zip(subs, mixes)]
    qxs = [_mm(_rms(x1, g_xa_ref[...]), wq_ref[...]).astype(BF16) for x1 in x1s]
    items = [(i, hd) for i in range(SUB_TILES) for hd in range(X_HEADS)]
    ksl = lambda hd: slice(hd * X_HEAD_DIM, (hd + 1) * X_HEAD_DIM)
    vsl = lambda hd: slice(X_W + hd * X_HEAD_DIM, X_W + (hd + 1) * X_HEAD_DIM)
    ss = [lax.dot_general(qxs[i][:, ksl(hd)], kv_scr[:, ksl(hd)], NT_DIMS, preferred_element_type=F32)
          * (X_HEAD_DIM ** -0.5) for i, hd in items]
    ps = [jnp.exp(s - jnp.max(s, axis=-1, keepdims=True)) for s in ss]
    dens = [jnp.sum(p, axis=-1, keepdims=True) for p in ps]
    os_ = [jnp.dot(p.astype(BF16), kv_scr[:, vsl(hd)], preferred_element_type=F32) / den
           for (i, hd), p, den in zip(items, ps, dens)]
    for i, r in enumerate(subs):
        o = jnp.concatenate(os_[i * X_HEADS:(i + 1) * X_HEADS], axis=1)
        x2_ref[r, :] = x1s[i] + _rms(_mm(o, wo_ref[...]), g_out_ref[...])


def _mix_xattn(att_t, dn, x2d, mem2d, woa, wod, g_mix, g_xa, wq, g_mem, wkv, wo, g_out, B, T):
    n_t = T // TS
    mem_len = mem2d.shape[0] // B
    row = lambda b, i: (b * n_t + i, 0)
    full = lambda a: pl.BlockSpec(a.shape, lambda b, i: (0, 0))
    return pl.pallas_call(
        _mix_xattn_kernel,
        grid=(B, n_t),
        in_specs=[pl.BlockSpec((ATT_Q, TS), lambda b, i: (0, b * n_t + i)), pl.BlockSpec((TS, DN_W), row),
                  pl.BlockSpec((TS, D_MODEL), row), pl.BlockSpec((mem_len, D_MODEL), lambda b, i: (b, 0)),
                  full(woa), full(wod), full(g_mix), full(g_xa), full(wq), full(g_mem), full(wkv), full(wo),
                  full(g_out)],
        out_specs=pl.BlockSpec((TS, D_MODEL), row),
        out_shape=jax.ShapeDtypeStruct((B * T, D_MODEL), F32),
        scratch_shapes=[pltpu.VMEM((mem_len, 2 * X_W), BF16)],
        compiler_params=_params("parallel", "arbitrary"),
        name="mix_xattn",
    )(att_t, dn, x2d, mem2d, woa, wod, g_mix, g_xa, wq, g_mem, wkv, wo, g_out)


def _mlp_kernel(x_ref, g1_ref, w1_ref, w2_ref, g2_ref, y_ref):
    x = x_ref[...]
    hb = _rms(x, g1_ref[...]).astype(BF16)
    acc = jnp.zeros((TM, D_MODEL), F32)
    for c in range(D_FF // FF_BLOCK):
        sl = slice(c * FF_BLOCK, (c + 1) * FF_BLOCK)
        a = jnp.maximum(jnp.dot(hb, w1_ref[:, sl], preferred_element_type=F32), 0.0)
        acc = acc + jnp.dot((a * a).astype(BF16), w2_ref[sl, :], preferred_element_type=F32)
    y_ref[...] = x + _rms(acc, g2_ref[...])


def _mlp(x2, g1, w1, w2, g2):
    BT = x2.shape[0]
    row = lambda i: (i, 0)
    full = lambda a: pl.BlockSpec(a.shape, lambda i: (0, 0))
    return pl.pallas_call(
        _mlp_kernel,
        grid=(BT // TM,),
        in_specs=[pl.BlockSpec((TM, D_MODEL), row), full(g1), full(w1), full(w2), full(g2)],
        out_specs=pl.BlockSpec((TM, D_MODEL), row),
        out_shape=jax.ShapeDtypeStruct((BT, D_MODEL), F32),
        compiler_params=_params("parallel"),
        name="mlp",
    )(x2, g1, w1, w2, g2)


def _rope_tables(T):
    half = ROPE_DIM // 2
    inv = ROPE_THETA ** (-(jnp.arange(half, dtype=F32) * 2.0 / ROPE_DIM))
    d = jnp.arange(LANES) % ATT_HEAD_DIM
    ang = jnp.arange(T).astype(F32)[:, None] * inv[d % half][None, :]
    cos, sin = jnp.cos(ang), jnp.sin(ang)
    lo, hi = (d < half)[None, :], ((d >= half) & (d < ROPE_DIM))[None, :]
    return jnp.where(lo | hi, cos, 1.0), jnp.where(lo, -sin, 0.0), jnp.where(hi, sin, 0.0)


def _gate_params(a_log_f, a_log_b, dt_f, dt_b):
    zeros = jnp.zeros((2 * DN_HEADS,), F32)
    a_vec = jnp.concatenate([zeros, a_log_f.astype(F32), a_log_b.astype(F32)])
    d_vec = jnp.concatenate([zeros, dt_f.astype(F32), dt_b.astype(F32)])
    pad = jnp.zeros((LANES - N_GATES,), F32)
    return jnp.stack([jnp.concatenate([a_vec, pad]), jnp.concatenate([d_vec, pad])])


def _layer(x, mem, rope_tables, w_in, attn_sink, conv_w, a_log_f, a_log_b, dt_f, dt_b, dn_norm_w, w_out,
           xa_wq, xa_wkv, xa_wo, mem_g, w1, w2, g_pre_mix, g_post_mix, g_pre_xa, g_post_xa,
           g_pre_mlp, g_post_mlp):
    B, T, _ = x.shape
    assert T % TS == 0 and T % (PREP_UNROLL * CHUNK) == 0 and T >= TQ + 2 * ATT_WINDOW
    x2d = x.reshape(B * T, D_MODEL)
    vec = lambda g: g.astype(F32).reshape(1, -1)
    hpg = ATT_HEADS // ATT_KV_HEADS

    def pair_heads(w, axis):
        shape = w.shape
        w = w.reshape(shape[:axis] + (ATT_KV_HEADS, hpg, ATT_HEAD_DIM) + shape[axis + 1:])
        return jnp.swapaxes(w, axis, axis + 1).reshape(shape)

    w_proj = jnp.concatenate([pair_heads(w_in[:, :ATT_Q], 1), w_in[:, ATT_Q:],
                              jnp.zeros((D_MODEL, LANES - N_GATES), w_in.dtype)], axis=1).astype(BF16)
    cos, sa, sb = rope_tables
    pc = _gate_params(a_log_f, a_log_b, dt_f, dt_b)

    aq, ak, avt, dqkv, dz, grow = _in_proj(x2d, T, vec(g_pre_mix), w_proj, cos, sa, sb, pc)
    att_t = _win_attn(aq, ak, avt, attn_sink.astype(F32), B, T)
    dn = _deltanet(dqkv, dz, grow, conv_w.astype(F32), vec(dn_norm_w), B, T)
    x2 = _mix_xattn(att_t, dn, x2d, mem.reshape(-1, D_MODEL),
                    pair_heads(w_out[:ATT_Q], 0).astype(BF16), w_out[ATT_Q:].astype(BF16),
                    vec(g_post_mix), vec(g_pre_xa), xa_wq.astype(BF16), vec(mem_g), xa_wkv.astype(BF16),
                    xa_wo.astype(BF16), vec(g_post_xa), B, T)
    y = _mlp(x2, vec(g_pre_mlp), w1.astype(BF16), w2.astype(BF16), vec(g_post_mlp))
    return y.reshape(B, T, D_MODEL)


def kernel(x_prompt, x_sample, mem_prompt, mem_sample, w_in, attn_sink, dn_conv_w, dn_A_log_f, dn_A_log_b,
           dn_dt_bias_f, dn_dt_bias_b, dn_norm_w, w_out, xa_wq, xa_wkv, xa_wo, mem_norm_g, mlp_w1, mlp_w2,
           norm_pre_mix, norm_post_mix, norm_pre_xa, norm_post_xa, norm_pre_mlp, norm_post_mlp):
    weights = (w_in, attn_sink, dn_conv_w, dn_A_log_f, dn_A_log_b, dn_dt_bias_f, dn_dt_bias_b, dn_norm_w,
               w_out, xa_wq, xa_wkv, xa_wo, mem_norm_g, mlp_w1, mlp_w2, norm_pre_mix, norm_post_mix,
               norm_pre_xa, norm_post_xa, norm_pre_mlp, norm_post_mlp)
    rope_tables = _rope_tables(max(x_prompt.shape[1], x_sample.shape[1]))
    outs = []
    for x, mem in ((x_prompt, mem_prompt), (x_sample, mem_sample)):
        for l in range(w_in.shape[0]):
            x = _layer(x, mem, rope_tables, *(w[l] for w in weights))
        outs.append(x)
    return tuple(outs)
```

```python
import functools

import jax
import jax.numpy as jnp
from jax import lax
from jax.experimental import pallas as pl
from jax.experimental.pallas import tpu as pltpu

F32 = jnp.float32
BF16 = jnp.bfloat16

D_MODEL = 1024
ATT_HEADS = 8
ATT_KV_HEADS = 2
ATT_HEAD_DIM = 64
ATT_WINDOW = 128
ROPE_THETA = 500000.0
ROPE_DIM = ATT_HEAD_DIM // 4
DN_HEADS = 4
DN_HEAD_DIM = 128
DN_CONV = 5
X_HEADS = 4
X_HEAD_DIM = 128
D_FF = 4 * D_MODEL
EPS = 1e-6

ATT_Q = ATT_HEADS * ATT_HEAD_DIM
ATT_KV = ATT_KV_HEADS * ATT_HEAD_DIM
DN_W = DN_HEADS * DN_HEAD_DIM
X_W = X_HEADS * X_HEAD_DIM
O_AK = ATT_Q
O_AV = O_AK + ATT_KV
O_DQKV = O_AV + ATT_KV
O_DZ = O_DQKV + 3 * DN_W
O_GATES = O_DZ + DN_W
N_GATES = 4 * DN_HEADS

LANES = 128
TM = 512
TQ = 128
TQ_STEP = 512
CHUNK = 128
PREP_UNROLL = 8
PAD = 8
COPY_ROWS = 512
SUB_TILES = 2
TS = SUB_TILES * TM
FF_BLOCK = 1024
VMEM_LIMIT = 56 * 1024 * 1024

LOG2E = 1.4426950408889634
ATT_Q_SCALE = ATT_HEAD_DIM ** -0.5 * LOG2E
XA_Q_SCALE = X_HEAD_DIM ** -0.5 * LOG2E

NT_DIMS = (((1,), (1,)), ((), ()))
TN_DIMS = (((0,), (0,)), ((), ()))


def _mm(a, b):
    return jnp.dot(a.astype(BF16), b.astype(BF16), preferred_element_type=F32)


def _mm_nt(a, b):
    return lax.dot_general(a.astype(BF16), b.astype(BF16), NT_DIMS, preferred_element_type=F32)


def _mm_tn(a, b):
    return lax.dot_general(a.astype(BF16), b.astype(BF16), TN_DIMS, preferred_element_type=F32)


def _rms(x, g):
    return x * lax.rsqrt(jnp.mean(x * x, axis=-1, keepdims=True) + EPS) * g


def _softplus(x):
    return jnp.maximum(x, 0.0) + jnp.log1p(jnp.exp(-jnp.abs(x)))


def _silu(x):
    return x * jax.nn.sigmoid(x)


def _iota(shape, dim):
    return lax.broadcasted_iota(jnp.int32, shape, dim)


def _params(*sem):
    return pltpu.CompilerParams(dimension_semantics=sem, vmem_limit_bytes=VMEM_LIMIT)


def _in_proj_kernel(x_ref, g_ref, w_ref, cos_ref, sa_ref, sb_ref, pc_ref,
                    aq_ref, ak_ref, avt_ref, dqkv_ref, dz_ref, grow_ref):
    subs = [slice(i * TM, (i + 1) * TM) for i in range(SUB_TILES)]
    hbs = [_rms(x_ref[r, :], g_ref[...]).astype(BF16) for r in subs]

    def rope(t, r):
        return (t * cos_ref[r, :] + pltpu.roll(t, LANES - ROPE_DIM // 2, 1) * sa_ref[r, :]
                + pltpu.roll(t, ROPE_DIM // 2, 1) * sb_ref[r, :])

    for r, hb in zip(subs, hbs):
        qkv = jnp.dot(hb, w_ref[:, :O_DQKV], preferred_element_type=F32)
        for j in range(ATT_Q // LANES):
            sl = slice(j * LANES, (j + 1) * LANES)
            aq_ref[r, sl] = (rope(qkv[:, sl], r) * ATT_Q_SCALE).astype(BF16)
        ak_ref[r, :] = rope(qkv[:, O_AK:O_AV], r).astype(BF16)
        avt_ref[:, r] = qkv[:, O_AV:O_DQKV].astype(BF16).T
        dqkv_ref[r, :] = jnp.dot(hb, w_ref[:, O_DQKV:O_DZ], preferred_element_type=F32)
        zg = jnp.dot(hb, w_ref[:, O_DZ:], preferred_element_type=F32)
        dz_ref[r, :] = zg[:, :DN_W]

        raw = zg[:, DN_W:]
        lane = _iota(raw.shape, 1)
        gval = -jnp.exp(pc_ref[0:1, :]) * _softplus(raw + pc_ref[1:2, :])
        gates = jnp.where(lane < 2 * DN_HEADS, jax.nn.sigmoid(raw), gval)
        pos = _iota(raw.shape, 0) & (CHUNK - 1)
        cf = gates
        cb = gates
        s = 1
        while s < CHUNK:
            cf = cf + jnp.where(pos >= s, pltpu.roll(cf, s, 0), 0.0)
            cb = cb + jnp.where(pos < CHUNK - s, pltpu.roll(cb, TM - s, 0), 0.0)
            s *= 2
        gcol = jnp.where(lane < 2 * DN_HEADS, gates, jnp.where(lane < 3 * DN_HEADS, cf, cb))
        grow_ref[:, r] = gcol.T[:N_GATES]


def _in_proj(x2d, T, g, w, cos, sa, sb, pc):
    BT = x2d.shape[0]
    n_t = T // TS
    row = lambda i: (i, 0)
    const = lambda i: (0, 0)
    tab = lambda i: (i % n_t, 0)
    full = lambda a: pl.BlockSpec(a.shape, const)
    return pl.pallas_call(
        _in_proj_kernel,
        grid=(BT // TS,),
        in_specs=[pl.BlockSpec((TS, D_MODEL), row), full(g), full(w),
                  pl.BlockSpec((TS, LANES), tab), pl.BlockSpec((TS, LANES), tab), pl.BlockSpec((TS, LANES), tab),
                  full(pc)],
        out_specs=[pl.BlockSpec((TS, ATT_Q), row), pl.BlockSpec((TS, ATT_KV), row),
                   pl.BlockSpec((ATT_KV, TS), lambda i: (0, i)),
                   pl.BlockSpec((TS, 3 * DN_W), row), pl.BlockSpec((TS, DN_W), row),
                   pl.BlockSpec((N_GATES, TS), lambda i: (0, i))],
        out_shape=[jax.ShapeDtypeStruct((BT, ATT_Q), BF16), jax.ShapeDtypeStruct((BT, ATT_KV), BF16),
                   jax.ShapeDtypeStruct((ATT_KV, BT), BF16),
                   jax.ShapeDtypeStruct((BT, 3 * DN_W), F32), jax.ShapeDtypeStruct((BT, DN_W), F32),
                   jax.ShapeDtypeStruct((N_GATES, BT), F32)],
        compiler_params=_params("parallel"),
        name="in_proj",
    )(x2d, g, w, cos, sa, sb, pc)


def _win_attn_kernel(sink_ref, q_ref, k_ref, vt_ref, o_ref, *, T):
    W = TQ + 2 * ATT_WINDOW
    hpg = ATT_HEADS // ATT_KV_HEADS
    rel = _iota((W, TQ), 0) - _iota((W, TQ), 1)
    items = []
    for sub in range(TQ_STEP // TQ):
        t0 = pl.program_id(1) * TQ_STEP + sub * TQ
        start = pl.multiple_of(jnp.clip(t0 - ATT_WINDOW, 0, T - W), LANES)
        k = k_ref[pl.ds(start, W), :]
        vt = vt_ref[:, pl.ds(start, W)]
        k_lo = _iota(k.shape, 1) < ATT_HEAD_DIM
        v_lo = _iota(vt.shape, 0) < ATT_HEAD_DIM
        kk = jnp.concatenate([jnp.where(k_lo, k, jnp.zeros_like(k)), jnp.where(k_lo, jnp.zeros_like(k), k)], axis=0)
        vvt = jnp.concatenate([jnp.where(v_lo, vt, jnp.zeros_like(vt)), jnp.where(v_lo, jnp.zeros_like(vt), vt)],
                              axis=1)
        cap = jnp.where(jnp.abs(rel + (start - t0)) <= ATT_WINDOW, jnp.inf, -1e30)
        cap = jnp.concatenate([cap, cap], axis=1)
        rows = slice(sub * TQ, (sub + 1) * TQ)
        for pair in range(ATT_Q // LANES // 2):
            groups = (2 * pair, 2 * pair + 1)
            qp = jnp.concatenate([q_ref[rows, g * LANES:(g + 1) * LANES] for g in groups], axis=0)
            items.append(dict(kk=kk, vvt=vvt, cap=cap, qp=qp, rows=rows, groups=groups))
    for it in items:
        it["st"] = lax.dot_general(it["kk"], it["qp"], NT_DIMS, preferred_element_type=F32)
    for it in items:
        ps, dens = [], []
        for kvh in range(ATT_KV_HEADS):
            sh = jnp.minimum(it["st"][kvh * W:(kvh + 1) * W], it["cap"])
            sk = jnp.concatenate([jnp.full((1, TQ), sink_ref[kvh * hpg + g] * LOG2E, F32) for g in it["groups"]],
                                 axis=1)
            m = jnp.maximum(jnp.max(sh, axis=0, keepdims=True), sk)
            p = jnp.exp2(sh - m)
            dens.append(jnp.sum(p, axis=0, keepdims=True) + jnp.exp2(sk - m))
            ps.append(p.astype(BF16))
        it["p"] = jnp.concatenate(ps, axis=0)
        it["dens"] = dens
    for it in items:
        ot = jnp.dot(it["vvt"], it["p"], preferred_element_type=F32)
        ot = ot / jnp.where(_iota(ot.shape, 0) < ATT_HEAD_DIM, it["dens"][0], it["dens"][1])
        for i, g in enumerate(it["groups"]):
            o_ref[g * LANES:(g + 1) * LANES, it["rows"]] = ot[:, i * TQ:(i + 1) * TQ].astype(BF16)


def _win_attn(aq, ak, avt, sink, B, T):
    n_q = T // TQ_STEP
    return pl.pallas_call(
        functools.partial(_win_attn_kernel, T=T),
        grid=(B, n_q),
        in_specs=[pl.BlockSpec(memory_space=pltpu.SMEM),
                  pl.BlockSpec((TQ_STEP, ATT_Q), lambda b, i: (b * n_q + i, 0)),
                  pl.BlockSpec((T, ATT_KV), lambda b, i: (b, 0)),
                  pl.BlockSpec((ATT_KV, T), lambda b, i: (0, b))],
        out_specs=pl.BlockSpec((ATT_Q, TQ_STEP), lambda b, i: (0, b * n_q + i)),
        out_shape=jax.ShapeDtypeStruct((ATT_Q, B * T), BF16),
        compiler_params=_params("parallel", "arbitrary"),
        name="win_attn",
    )(sink, aq, ak, avt)


def _mm_pair(xb, yb):
    return jnp.concatenate([jnp.dot(xb[:, :CHUNK], yb[:, :CHUNK], preferred_element_type=F32),
                            jnp.dot(xb[:, CHUNK:], yb[:, CHUNK:], preferred_element_type=F32)], axis=1)


def _unit_tri_inverse_pairs(l2s):
    bf = lambda xs: [x.astype(BF16) for x in xs]
    r = _iota((CHUNK, 2 * CHUNK), 0)
    c = _iota((CHUNK, 2 * CHUNK), 1) & (CHUNK - 1)
    base = 16
    in_base = (r // base) == (c // base)
    ds = [jnp.where(in_base, l2, 0.0) for l2 in l2s]
    ps = [jnp.where(r == c, 1.0, 0.0) - d for d in ds]
    powb = bf(ds)
    for _ in range(3):
        powb = bf([_mm_pair(x, x) for x in powb])
        ps = [p + _mm_pair(pb, x) for p, pb, x in zip(ps, bf(ps), powb)]
    s = base
    while s < CHUNK:
        off_diag = ((r // (2 * s)) == (c // (2 * s))) & ((r // s) != (c // s))
        mbs = bf([jnp.where(off_diag, l2, 0.0) for l2 in l2s])
        pbs = bf(ps)
        tbs = bf([_mm_pair(mb, pb) for mb, pb in zip(mbs, pbs)])
        ps = [p - _mm_pair(pb, tb) for p, pb, tb in zip(ps, pbs, tbs)]
        s *= 2
    return ps


def _deltanet_kernel(q_ref, k_ref, v_ref, z_ref, grow_ref, cwq_ref, cwk_ref, cwv_ref, nw_ref,
                     o_ref, pad_scr, u_scr, w_scr, qg_scr, at_scr, gl_scr, ku_scr, kw_scr, sp_scr, *, T):
    C = CHUNK
    N = T // C
    U = PREP_UNROLL
    h = pl.program_id(1)
    ri = _iota((C, C), 0)
    ci = _iota((C, C), 1)
    incl = (ri >= ci, ri <= ci)
    strict = (ri > ci, ri < ci)
    lane = _iota((C, LANES), 1)

    zpad = jnp.zeros((PAD, DN_HEAD_DIM), F32)
    for a, ref in enumerate((q_ref, k_ref, v_ref)):
        pad_scr[a, 0:PAD, :] = zpad
        pad_scr[a, T + PAD:T + 2 * PAD, :] = zpad

        def copy_rows(i, carry, a=a, ref=ref):
            r0 = pl.multiple_of(i * COPY_ROWS, COPY_ROWS)
            pad_scr[a, pl.ds(pl.multiple_of(r0 + PAD, PAD), COPY_ROWS), :] = ref[pl.ds(r0, COPY_ROWS), :]
            return carry

        lax.fori_loop(0, T // COPY_ROWS, copy_rows, 0)

    def conv_silu(a, cw_ref, c0):
        win = pad_scr.at[a, pl.ds(c0, C + 2 * PAD), :]
        off = PAD - DN_CONV // 2
        acc = win[off:off + C, :] * cw_ref[0:1, :]
        for j in range(1, DN_CONV):
            acc = acc + win[off + j:off + j + C, :] * cw_ref[j:j + 1, :]
        return _silu(acc)

    def l2norm(t):
        return t * lax.rsqrt(jnp.sum(t * t, axis=-1, keepdims=True) + EPS)

    def chunk_start(n):
        return n * C if isinstance(n, int) else pl.multiple_of(n * C, C)

    def prepare_chunks(ns):
        c0s = [chunk_start(n) for n in ns]
        qs = [l2norm(conv_silu(0, cwq_ref, c0)) * (DN_HEAD_DIM ** -0.5) for c0 in c0s]
        ks = [l2norm(conv_silu(1, cwk_ref, c0)) for c0 in c0s]
        vs = [conv_silu(2, cwv_ref, c0) for c0 in c0s]
        kbs = [k.astype(BF16) for k in ks]
        grams = [lax.dot_general(jnp.concatenate([q.astype(BF16), kb], axis=0), kb, NT_DIMS,
                                 preferred_element_type=F32) for q, kb in zip(qs, kbs)]
        l2s, rhss, kds = [], [], []
        for u in range(U):
            rows = pl.ds(c0s[u], C)
            q, k, v = qs[u], ks[u], vs[u]
            gr = grow_ref[:, rows]
            gt = jnp.concatenate([gr, jnp.zeros((LANES - N_GATES, C), F32)], axis=0).T
            col = lambda idx: jnp.sum(jnp.where(lane == idx, gt, 0.0), axis=1, keepdims=True)
            beta = (col(h), col(DN_HEADS + h))
            gcc = (col(2 * DN_HEADS + h), col(3 * DN_HEADS + h))
            sub = _iota(gr.shape, 0)
            row = lambda idx: jnp.sum(jnp.where(sub == idx, gr, 0.0), axis=0, keepdims=True)
            gcr = (row(2 * DN_HEADS + h), row(3 * DN_HEADS + h))
            glast = (gcr[0][:, C - 1:C], gcr[1][:, 0:1])
            qk = grams[u][:C]
            kk = grams[u][C:]
            l2 = []
            for d in range(2):
                decay = jnp.where(incl[d], jnp.exp(jnp.where(incl[d], gcc[d] - gcr[d], 0.0)), 0.0)
                l2.append(jnp.where(strict[d], beta[d] * kk * decay, 0.0))
                eg = jnp.exp(gcc[d])
                qg_scr[d, rows, :] = (q * eg).astype(BF16)
                kds.append((k * jnp.exp(glast[d] - gcc[d])).astype(BF16))
                at_scr[d, rows, :] = (qk * decay).astype(BF16)
                gl_scr[d, ns[u]] = jnp.broadcast_to(jnp.exp(glast[d]), (8, LANES))
                rhss.append(jnp.concatenate([v * beta[d], (k * beta[d]) * eg], axis=1).astype(BF16))
            l2s.append(jnp.concatenate(l2, axis=1))
        ainvs = _unit_tri_inverse_pairs(l2s)
        for u in range(U):
            rows = pl.ds(c0s[u], C)
            for d in range(2):
                uw = jnp.dot(ainvs[u][:, d * C:(d + 1) * C].astype(BF16), rhss[2 * u + d],
                             preferred_element_type=F32)
                uwb = uw.astype(BF16)
                u_scr[d, rows, :] = uw[:, :DN_HEAD_DIM]
                w_scr[d, rows, :] = uwb[:, DN_HEAD_DIM:]
                kuw = lax.dot_general(kds[2 * u + d], uwb, TN_DIMS, preferred_element_type=F32)
                ku_scr[d, rows, :] = kuw[:, :DN_HEAD_DIM]
                kw_scr[d, rows, :] = kuw[:, DN_HEAD_DIM:].astype(BF16)

    def scan_step(d, n, state):
        rows = pl.ds(chunk_start(n), C)
        sb = state.astype(BF16)
        sp_scr[d, rows, :] = sb
        return (state * gl_scr[d, n][0:1, :] + ku_scr[d, rows, :]
                - jnp.dot(kw_scr[d, rows, :], sb, preferred_element_type=F32))

    def scan_steps(i0, states):
        for j in range(half):
            i = i0 + j
            states = (scan_step(0, i, states[0]), scan_step(1, N - 1 - i, states[1]))
        return states

    def output_chunks(ns, between=()):
        pending = list(between)

        def emit(count):
            for _ in range(min(count, len(pending))):
                pending.pop(0)()

        rows = [pl.ds(chunk_start(n), C) for n in ns]
        items = [(r, d) for r in rows for d in range(2)]
        emit(1)
        sbs = [sp_scr[d, r, :] for r, d in items]
        ws_qs = [jnp.dot(jnp.concatenate([w_scr[d, r, :], qg_scr[d, r, :]], axis=0), sb,
                         preferred_element_type=F32) for (r, d), sb in zip(items, sbs)]
        emit(1)
        v_news = [(u_scr[d, r, :] - x[:C]).astype(BF16) for (r, d), x in zip(items, ws_qs)]
        os_ = [x[C:] + jnp.dot(at_scr[d, r, :], vn, preferred_element_type=F32)
               for (r, d), x, vn in zip(items, ws_qs, v_news)]
        emit(len(pending))
        for u, r in enumerate(rows):
            o = os_[2 * u] + os_[2 * u + 1]
            o = o * lax.rsqrt(jnp.mean(o * o, axis=-1, keepdims=True) + EPS) * nw_ref[...]
            o_ref[r, :] = (o * _silu(z_ref[r, :])).astype(BF16)

    half = U // 2
    G = N // U
    front_back = lambda g: ([g * half + j for j in range(half)]
                            + [N - (g + 1) * half + j for j in range(half)])
    finished = lambda g: ([N // 2 + g * half + j for j in range(half)]
                          + [N // 2 - 1 - g * half - j for j in range(half)])
    s0 = jnp.zeros((DN_HEAD_DIM, DN_HEAD_DIM), F32)
    prepare_chunks(front_back(0))

    def prepare_and_scan(g, states):
        states = scan_steps((g - 1) * half, states)
        prepare_chunks(front_back(g))
        return states

    states = lax.fori_loop(1, G, prepare_and_scan, (s0, s0))
    states = scan_steps((G - 1) * half, states)
    states = scan_steps(N // 2, states)

    def scan_and_output(g, states):
        holder = [states]

        def one_step(j):
            def run():
                i = N // 2 + g * half + j
                holder[0] = (scan_step(0, i, holder[0][0]), scan_step(1, N - 1 - i, holder[0][1]))
            return run

        output_chunks(finished(g - 1), between=[one_step(j) for j in range(half)])
        return holder[0]

    lax.fori_loop(1, G, scan_and_output, states)
    output_chunks(finished(G - 1))


def _deltanet(dqkv, dz, grow, conv_w, norm_w, B, T):
    H = DN_HEADS
    N = T // CHUNK
    seq = lambda off: pl.BlockSpec((T, DN_HEAD_DIM), lambda b, h: (b, off + h))
    cw = lambda off: pl.BlockSpec((DN_CONV, DN_HEAD_DIM), lambda b, h: (0, off + h))
    return pl.pallas_call(
        functools.partial(_deltanet_kernel, T=T),
        grid=(B, H),
        in_specs=[seq(0), seq(H), seq(2 * H), seq(0),
                  pl.BlockSpec((N_GATES, T), lambda b, h: (0, b)),
                  cw(0), cw(H), cw(2 * H),
                  pl.BlockSpec((1, DN_HEAD_DIM), lambda b, h: (0, 0))],
        out_specs=seq(0),
        out_shape=jax.ShapeDtypeStruct((B * T, DN_W), BF16),
        scratch_shapes=[pltpu.VMEM((3, T + 2 * PAD, DN_HEAD_DIM), F32),
                        pltpu.VMEM((2, T, DN_HEAD_DIM), F32),
                        pltpu.VMEM((2, T, DN_HEAD_DIM), BF16),
                        pltpu.VMEM((2, T, DN_HEAD_DIM), BF16),
                        pltpu.VMEM((2, T, CHUNK), BF16),
                        pltpu.VMEM((2, N, 8, LANES), F32),
                        pltpu.VMEM((2, T, DN_HEAD_DIM), F32),
                        pltpu.VMEM((2, T, DN_HEAD_DIM), BF16),
                        pltpu.VMEM((2, T, DN_HEAD_DIM), BF16)],
        compiler_params=_params("parallel", "arbitrary"),
        name="deltanet",
    )(dqkv, dqkv, dqkv, dz, grow, conv_w, conv_w, conv_w, norm_w)


def _mix_xattn_kernel(att_t_ref, dn_ref, x_ref, mem_ref, woa_ref, wod_ref, g_mix_ref, g_xa_ref, wq_ref,
                      g_mem_ref, wkv_ref, wo_ref, g_out_ref, x2_ref, kv_scr):
    @pl.when(pl.program_id(1) == 0)
    def _():
        kv_scr[...] = _mm(_rms(mem_ref[...], g_mem_ref[...]), wkv_ref[...]).astype(BF16)

    subs = [slice(i * TM, (i + 1) * TM) for i in range(SUB_TILES)]
    mixes = [lax.dot_general(att_t_ref[:, r], woa_ref[...], TN_DIMS, preferred_element_type=F32)
             + jnp.dot(dn_ref[r, :], wod_ref[...], preferred_element_type=F32) for r in subs]
    x1s = [x_ref[r, :] + _rms(mix, g_mix_ref[...]) for r, mix in zip(subs, mixes)]
    qxs = [(_mm(_rms(x1, g_xa_ref[...]), wq_ref[...]) * XA_Q_SCALE).astype(BF16) for x1 in x1s]
    items = [(i, hd) for i in range(SUB_TILES) for hd in range(X_HEADS)]
    ksl = lambda hd: slice(hd * X_HEAD_DIM, (hd + 1) * X_HEAD_DIM)
    vsl = lambda hd: slice(X_W + hd * X_HEAD_DIM, X_W + (hd + 1) * X_HEAD_DIM)
    ss = [lax.dot_general(qxs[i][:, ksl(hd)], kv_scr[:, ksl(hd)], NT_DIMS, preferred_element_type=F32)
          for i, hd in items]
    ps = [jnp.exp2(s - jnp.max(s, axis=-1, keepdims=True)) for s in ss]
    dens = [jnp.sum(p, axis=-1, keepdims=True) for p in ps]
    os_ = [jnp.dot(p.astype(BF16), kv_scr[:, vsl(hd)], preferred_element_type=F32) / den
           for (i, hd), p, den in zip(items, ps, dens)]
    for i, r in enumerate(subs):
        o = jnp.concatenate(os_[i * X_HEADS:(i + 1) * X_HEADS], axis=1)
        x2_ref[r, :] = x1s[i] + _rms(_mm(o, wo_ref[...]), g_out_ref[...])


def _mix_xattn(att_t, dn, x2d, mem2d, woa, wod, g_mix, g_xa, wq, g_mem, wkv, wo, g_out, B, T):
    n_t = T // TS
    mem_len = mem2d.shape[0] // B
    row = lambda b, i: (b * n_t + i, 0)
    full = lambda a: pl.BlockSpec(a.shape, lambda b, i: (0, 0))
    return pl.pallas_call(
        _mix_xattn_kernel,
        grid=(B, n_t),
        in_specs=[pl.BlockSpec((ATT_Q, TS), lambda b, i: (0, b * n_t + i)), pl.BlockSpec((TS, DN_W), row),
                  pl.BlockSpec((TS, D_MODEL), row), pl.BlockSpec((mem_len, D_MODEL), lambda b, i: (b, 0)),
                  full(woa), full(wod), full(g_mix), full(g_xa), full(wq), full(g_mem), full(wkv), full(wo),
                  full(g_out)],
        out_specs=pl.BlockSpec((TS, D_MODEL), row),
        out_shape=jax.ShapeDtypeStruct((B * T, D_MODEL), F32),
        scratch_shapes=[pltpu.VMEM((mem_len, 2 * X_W), BF16)],
        compiler_params=_params("parallel", "arbitrary"),
        name="mix_xattn",
    )(att_t, dn, x2d, mem2d, woa, wod, g_mix, g_xa, wq, g_mem, wkv, wo, g_out)


def _mlp_kernel(x_ref, g1_ref, w1_ref, w2_ref, g2_ref, y_ref):
    x = x_ref[...]
    hb = _rms(x, g1_ref[...]).astype(BF16)
    acc = jnp.zeros((TM, D_MODEL), F32)
    for c in range(D_FF // FF_BLOCK):
        sl = slice(c * FF_BLOCK, (c + 1) * FF_BLOCK)
        a = jnp.maximum(jnp.dot(hb, w1_ref[:, sl], preferred_element_type=F32), 0.0)
        acc = acc + jnp.dot((a * a).astype(BF16), w2_ref[sl, :], preferred_element_type=F32)
    y_ref[...] = x + _rms(acc, g2_ref[...])


def _mlp(x2, g1, w1, w2, g2):
    BT = x2.shape[0]
    row = lambda i: (i, 0)
    full = lambda a: pl.BlockSpec(a.shape, lambda i: (0, 0))
    return pl.pallas_call(
        _mlp_kernel,
        grid=(BT // TM,),
        in_specs=[pl.BlockSpec((TM, D_MODEL), row), full(g1), full(w1), full(w2), full(g2)],
        out_specs=pl.BlockSpec((TM, D_MODEL), row),
        out_shape=jax.ShapeDtypeStruct((BT, D_MODEL), F32),
        compiler_params=_params("parallel"),
        name="mlp",
    )(x2, g1, w1, w2, g2)


def _rope_tables(T):
    half = ROPE_DIM // 2
    inv = ROPE_THETA ** (-(jnp.arange(half, dtype=F32) * 2.0 / ROPE_DIM))
    d = jnp.arange(LANES) % ATT_HEAD_DIM
    ang = jnp.arange(T).astype(F32)[:, None] * inv[d % half][None, :]
    cos, sin = jnp.cos(ang), jnp.sin(ang)
    lo, hi = (d < half)[None, :], ((d >= half) & (d < ROPE_DIM))[None, :]
    return jnp.where(lo | hi, cos, 1.0), jnp.where(lo, -sin, 0.0), jnp.where(hi, sin, 0.0)


def _gate_params(a_log_f, a_log_b, dt_f, dt_b):
    zeros = jnp.zeros((2 * DN_HEADS,), F32)
    a_vec = jnp.concatenate([zeros, a_log_f.astype(F32), a_log_b.astype(F32)])
    d_vec = jnp.concatenate([zeros, dt_f.astype(F32), dt_b.astype(F32)])
    pad = jnp.zeros((LANES - N_GATES,), F32)
    return jnp.stack([jnp.concatenate([a_vec, pad]), jnp.concatenate([d_vec, pad])])


def _layer(x, mem, rope_tables, w_in, attn_sink, conv_w, a_log_f, a_log_b, dt_f, dt_b, dn_norm_w, w_out,
           xa_wq, xa_wkv, xa_wo, mem_g, w1, w2, g_pre_mix, g_post_mix, g_pre_xa, g_post_xa,
           g_pre_mlp, g_post_mlp):
    B, T, _ = x.shape
    assert T % TS == 0 and T % TQ_STEP == 0 and T % (PREP_UNROLL * CHUNK) == 0 and T >= TQ + 2 * ATT_WINDOW
    x2d = x.reshape(B * T, D_MODEL)
    vec = lambda g: g.astype(F32).reshape(1, -1)
    hpg = ATT_HEADS // ATT_KV_HEADS

    def pair_heads(w, axis):
        shape = w.shape
        w = w.reshape(shape[:axis] + (ATT_KV_HEADS, hpg, ATT_HEAD_DIM) + shape[axis + 1:])
        return jnp.swapaxes(w, axis, axis + 1).reshape(shape)

    w_proj = jnp.concatenate([pair_heads(w_in[:, :ATT_Q], 1), w_in[:, ATT_Q:],
                              jnp.zeros((D_MODEL, LANES - N_GATES), w_in.dtype)], axis=1).astype(BF16)
    cos, sa, sb = rope_tables
    pc = _gate_params(a_log_f, a_log_b, dt_f, dt_b)

    aq, ak, avt, dqkv, dz, grow = _in_proj(x2d, T, vec(g_pre_mix), w_proj, cos, sa, sb, pc)
    att_t = _win_attn(aq, ak, avt, attn_sink.astype(F32), B, T)
    dn = _deltanet(dqkv, dz, grow, conv_w.astype(F32), vec(dn_norm_w), B, T)
    x2 = _mix_xattn(att_t, dn, x2d, mem.reshape(-1, D_MODEL),
                    pair_heads(w_out[:ATT_Q], 0).astype(BF16), w_out[ATT_Q:].astype(BF16),
                    vec(g_post_mix), vec(g_pre_xa), xa_wq.astype(BF16), vec(mem_g), xa_wkv.astype(BF16),
                    xa_wo.astype(BF16), vec(g_post_xa), B, T)
    y = _mlp(x2, vec(g_pre_mlp), w1.astype(BF16), w2.astype(BF16), vec(g_post_mlp))
    return y.reshape(B, T, D_MODEL)


def kernel(x_prompt, x_sample, mem_prompt, mem_sample, w_in, attn_sink, dn_conv_w, dn_A_log_f, dn_A_log_b,
           dn_dt_bias_f, dn_dt_bias_b, dn_norm_w, w_out, xa_wq, xa_wkv, xa_wo, mem_norm_g, mlp_w1, mlp_w2,
           norm_pre_mix, norm_post_mix, norm_pre_xa, norm_post_xa, norm_pre_mlp, norm_post_mlp):
    weights = (w_in, attn_sink, dn_conv_w, dn_A_log_f, dn_A_log_b, dn_dt_bias_f, dn_dt_bias_b, dn_norm_w,
               w_out, xa_wq, xa_wkv, xa_wo, mem_norm_g, mlp_w1, mlp_w2, norm_pre_mix, norm_post_mix,
               norm_pre_xa, norm_post_xa, norm_pre_mlp, norm_post_mlp)
    rope_tables = _rope_tables(max(x_prompt.shape[1], x_sample.shape[1]))
    outs = []
    for x, mem in ((x_prompt, mem_prompt), (x_sample, mem_sample)):
        for l in range(w_in.shape[0]):
            x = _layer(x, mem, rope_tables, *(w[l] for w in weights))
        outs.append(x)
    return tuple(outs)
```

```python
import functools

import jax
import jax.numpy as jnp
from jax import lax
from jax.experimental import pallas as pl
from jax.experimental.pallas import tpu as pltpu

F32 = jnp.float32
BF16 = jnp.bfloat16

D_MODEL = 1024
ATT_HEADS = 8
ATT_KV_HEADS = 2
ATT_HEAD_DIM = 64
ATT_WINDOW = 128
ROPE_THETA = 500000.0
ROPE_DIM = ATT_HEAD_DIM // 4
DN_HEADS = 4
DN_HEAD_DIM = 128
DN_CONV = 5
X_HEADS = 4
X_HEAD_DIM = 128
D_FF = 4 * D_MODEL
EPS = 1e-6

ATT_Q = ATT_HEADS * ATT_HEAD_DIM
ATT_KV = ATT_KV_HEADS * ATT_HEAD_DIM
DN_W = DN_HEADS * DN_HEAD_DIM
X_W = X_HEADS * X_HEAD_DIM
O_AK = ATT_Q
O_AV = O_AK + ATT_KV
O_DQKV = O_AV + ATT_KV
O_DZ = O_DQKV + 3 * DN_W
O_GATES = O_DZ + DN_W
N_GATES = 4 * DN_HEADS

LANES = 128
TM = 512
TQ = 128
TQ_STEP = 512
CHUNK = 128
PREP_UNROLL = 8
PAD = 8
COPY_ROWS = 512
SUB_TILES = 2
TS = SUB_TILES * TM
FF_BLOCK = 1024
VMEM_LIMIT = 56 * 1024 * 1024

LOG2E = 1.4426950408889634
ATT_Q_SCALE = ATT_HEAD_DIM ** -0.5 * LOG2E
XA_Q_SCALE = X_HEAD_DIM ** -0.5 * LOG2E

NT_DIMS = (((1,), (1,)), ((), ()))
TN_DIMS = (((0,), (0,)), ((), ()))


def _mm(a, b):
    return jnp.dot(a.astype(BF16), b.astype(BF16), preferred_element_type=F32)


def _mm_nt(a, b):
    return lax.dot_general(a.astype(BF16), b.astype(BF16), NT_DIMS, preferred_element_type=F32)


def _mm_tn(a, b):
    return lax.dot_general(a.astype(BF16), b.astype(BF16), TN_DIMS, preferred_element_type=F32)


def _rms(x, g):
    return x * lax.rsqrt(jnp.mean(x * x, axis=-1, keepdims=True) + EPS) * g


def _softplus(x):
    return jnp.maximum(x, 0.0) + jnp.log1p(jnp.exp(-jnp.abs(x)))


def _silu(x):
    return x * jax.nn.sigmoid(x)


def _iota(shape, dim):
    return lax.broadcasted_iota(jnp.int32, shape, dim)


def _params(*sem):
    return pltpu.CompilerParams(dimension_semantics=sem, vmem_limit_bytes=VMEM_LIMIT)


def _in_proj_kernel(x_ref, g_ref, w_ref, cos_ref, sa_ref, sb_ref, pc_ref,
                    aq_ref, ak_ref, avt_ref, dqkv_ref, dz_ref, grow_ref):
    subs = [slice(i * TM, (i + 1) * TM) for i in range(SUB_TILES)]
    hbs = [_rms(x_ref[r, :], g_ref[...]).astype(BF16) for r in subs]

    def rope(t, r):
        return (t * cos_ref[r, :] + pltpu.roll(t, LANES - ROPE_DIM // 2, 1) * sa_ref[r, :]
                + pltpu.roll(t, ROPE_DIM // 2, 1) * sb_ref[r, :])

    qkvs = [jnp.dot(hb, w_ref[:, :O_DQKV], preferred_element_type=F32) for hb in hbs]
    for r, hb in zip(subs, hbs):
        dqkv_ref[r, :] = jnp.dot(hb, w_ref[:, O_DQKV:O_DZ], preferred_element_type=F32)
    zgs = [jnp.dot(hb, w_ref[:, O_DZ:], preferred_element_type=F32) for hb in hbs]
    for r, qkv, zg in zip(subs, qkvs, zgs):
        for j in range(ATT_Q // LANES):
            sl = slice(j * LANES, (j + 1) * LANES)
            aq_ref[r, sl] = (rope(qkv[:, sl], r) * ATT_Q_SCALE).astype(BF16)
        ak_ref[r, :] = rope(qkv[:, O_AK:O_AV], r).astype(BF16)
        avt_ref[:, r] = qkv[:, O_AV:O_DQKV].astype(BF16).T
        dz_ref[r, :] = zg[:, :DN_W]

        raw = zg[:, DN_W:]
        lane = _iota(raw.shape, 1)
        gval = -jnp.exp(pc_ref[0:1, :]) * _softplus(raw + pc_ref[1:2, :])
        gates = jnp.where(lane < 2 * DN_HEADS, jax.nn.sigmoid(raw), gval)
        pos = _iota(raw.shape, 0) & (CHUNK - 1)
        cf = gates
        cb = gates
        s = 1
        while s < CHUNK:
            cf = cf + jnp.where(pos >= s, pltpu.roll(cf, s, 0), 0.0)
            cb = cb + jnp.where(pos < CHUNK - s, pltpu.roll(cb, TM - s, 0), 0.0)
            s *= 2
        gcol = jnp.where(lane < 2 * DN_HEADS, gates, jnp.where(lane < 3 * DN_HEADS, cf, cb))
        grow_ref[:, r] = gcol.T[:N_GATES]


def _in_proj(x2d, T, g, w, cos, sa, sb, pc):
    BT = x2d.shape[0]
    n_t = T // TS
    row = lambda i: (i, 0)
    const = lambda i: (0, 0)
    tab = lambda i: (i % n_t, 0)
    full = lambda a: pl.BlockSpec(a.shape, const)
    return pl.pallas_call(
        _in_proj_kernel,
        grid=(BT // TS,),
        in_specs=[pl.BlockSpec((TS, D_MODEL), row), full(g), full(w),
                  pl.BlockSpec((TS, LANES), tab), pl.BlockSpec((TS, LANES), tab), pl.BlockSpec((TS, LANES), tab),
                  full(pc)],
        out_specs=[pl.BlockSpec((TS, ATT_Q), row), pl.BlockSpec((TS, ATT_KV), row),
                   pl.BlockSpec((ATT_KV, TS), lambda i: (0, i)),
                   pl.BlockSpec((TS, 3 * DN_W), row), pl.BlockSpec((TS, DN_W), row),
                   pl.BlockSpec((N_GATES, TS), lambda i: (0, i))],
        out_shape=[jax.ShapeDtypeStruct((BT, ATT_Q), BF16), jax.ShapeDtypeStruct((BT, ATT_KV), BF16),
                   jax.ShapeDtypeStruct((ATT_KV, BT), BF16),
                   jax.ShapeDtypeStruct((BT, 3 * DN_W), F32), jax.ShapeDtypeStruct((BT, DN_W), F32),
                   jax.ShapeDtypeStruct((N_GATES, BT), F32)],
        compiler_params=_params("parallel"),
        name="in_proj",
    )(x2d, g, w, cos, sa, sb, pc)


def _win_attn_kernel(sink_ref, q_ref, k_ref, vt_ref, o_ref, *, T):
    W = TQ + 2 * ATT_WINDOW
    hpg = ATT_HEADS // ATT_KV_HEADS
    rel = _iota((W, TQ), 0) - _iota((W, TQ), 1)
    items = []
    for sub in range(TQ_STEP // TQ):
        t0 = pl.program_id(1) * TQ_STEP + sub * TQ
        start = pl.multiple_of(jnp.clip(t0 - ATT_WINDOW, 0, T - W), LANES)
        k = k_ref[pl.ds(start, W), :]
        vt = vt_ref[:, pl.ds(start, W)]
        k_lo = _iota(k.shape, 1) < ATT_HEAD_DIM
        v_lo = _iota(vt.shape, 0) < ATT_HEAD_DIM
        kk = jnp.concatenate([jnp.where(k_lo, k, jnp.zeros_like(k)), jnp.where(k_lo, jnp.zeros_like(k), k)], axis=0)
        vvt = jnp.concatenate([jnp.where(v_lo, vt, jnp.zeros_like(vt)), jnp.where(v_lo, jnp.zeros_like(vt), vt)],
                              axis=1)
        cap = jnp.where(jnp.abs(rel + (start - t0)) <= ATT_WINDOW, jnp.inf, -1e30)
        cap = jnp.concatenate([cap, cap], axis=1)
        rows = slice(sub * TQ, (sub + 1) * TQ)
        for pair in range(ATT_Q // LANES // 2):
            groups = (2 * pair, 2 * pair + 1)
            qp = jnp.concatenate([q_ref[rows, g * LANES:(g + 1) * LANES] for g in groups], axis=0)
            items.append(dict(kk=kk, vvt=vvt, cap=cap, qp=qp, rows=rows, groups=groups))
    for it in items:
        it["st"] = lax.dot_general(it["kk"], it["qp"], NT_DIMS, preferred_element_type=F32)
    for it in items:
        ps, dens = [], []
        for kvh in range(ATT_KV_HEADS):
            sh = jnp.minimum(it["st"][kvh * W:(kvh + 1) * W], it["cap"])
            sk = jnp.concatenate([jnp.full((1, TQ), sink_ref[kvh * hpg + g] * LOG2E, F32) for g in it["groups"]],
                                 axis=1)
            m = jnp.maximum(jnp.max(sh, axis=0, keepdims=True), sk)
            p = jnp.exp2(sh - m)
            dens.append(jnp.sum(p, axis=0, keepdims=True) + jnp.exp2(sk - m))
            ps.append(p.astype(BF16))
        it["p"] = jnp.concatenate(ps, axis=0)
        it["dens"] = dens
    for it in items:
        ot = jnp.dot(it["vvt"], it["p"], preferred_element_type=F32)
        ot = ot / jnp.where(_iota(ot.shape, 0) < ATT_HEAD_DIM, it["dens"][0], it["dens"][1])
        for i, g in enumerate(it["groups"]):
            o_ref[g * LANES:(g + 1) * LANES, it["rows"]] = ot[:, i * TQ:(i + 1) * TQ].astype(BF16)


def _win_attn(aq, ak, avt, sink, B, T):
    n_q = T // TQ_STEP
    return pl.pallas_call(
        functools.partial(_win_attn_kernel, T=T),
        grid=(B, n_q),
        in_specs=[pl.BlockSpec(memory_space=pltpu.SMEM),
                  pl.BlockSpec((TQ_STEP, ATT_Q), lambda b, i: (b * n_q + i, 0)),
                  pl.BlockSpec((T, ATT_KV), lambda b, i: (b, 0)),
                  pl.BlockSpec((ATT_KV, T), lambda b, i: (0, b))],
        out_specs=pl.BlockSpec((ATT_Q, TQ_STEP), lambda b, i: (0, b * n_q + i)),
        out_shape=jax.ShapeDtypeStruct((ATT_Q, B * T), BF16),
        compiler_params=_params("parallel", "arbitrary"),
        name="win_attn",
    )(sink, aq, ak, avt)


def _mm_pair(xb, yb):
    return jnp.concatenate([jnp.dot(xb[:, :CHUNK], yb[:, :CHUNK], preferred_element_type=F32),
                            jnp.dot(xb[:, CHUNK:], yb[:, CHUNK:], preferred_element_type=F32)], axis=1)


def _unit_tri_inverse_pairs(l2s):
    bf = lambda xs: [x.astype(BF16) for x in xs]
    r = _iota((CHUNK, 2 * CHUNK), 0)
    c = _iota((CHUNK, 2 * CHUNK), 1) & (CHUNK - 1)
    base = 16
    in_base = (r // base) == (c // base)
    ds = [jnp.where(in_base, l2, 0.0) for l2 in l2s]
    ps = [jnp.where(r == c, 1.0, 0.0) - d for d in ds]
    powb = bf(ds)
    for _ in range(3):
        powb = bf([_mm_pair(x, x) for x in powb])
        ps = [p + _mm_pair(pb, x) for p, pb, x in zip(ps, bf(ps), powb)]
    s = base
    while s < CHUNK:
        off_diag = ((r // (2 * s)) == (c // (2 * s))) & ((r // s) != (c // s))
        mbs = bf([jnp.where(off_diag, l2, 0.0) for l2 in l2s])
        pbs = bf(ps)
        tbs = bf([_mm_pair(mb, pb) for mb, pb in zip(mbs, pbs)])
        ps = [p - _mm_pair(pb, tb) for p, pb, tb in zip(ps, pbs, tbs)]
        s *= 2
    return ps


def _deltanet_kernel(q_ref, k_ref, v_ref, z_ref, grow_ref, cwq_ref, cwk_ref, cwv_ref, nw_ref,
                     o_ref, pad_scr, u_scr, w_scr, qg_scr, at_scr, gl_scr, ku_scr, kw_scr, sp_scr, *, T):
    C = CHUNK
    N = T // C
    U = PREP_UNROLL
    h = pl.program_id(1)
    ri = _iota((C, C), 0)
    ci = _iota((C, C), 1)
    incl = (ri >= ci, ri <= ci)
    strict = (ri > ci, ri < ci)
    lane = _iota((C, LANES), 1)

    zpad = jnp.zeros((PAD, DN_HEAD_DIM), F32)
    for a, ref in enumerate((q_ref, k_ref, v_ref)):
        pad_scr[a, 0:PAD, :] = zpad
        pad_scr[a, T + PAD:T + 2 * PAD, :] = zpad

        def copy_rows(i, carry, a=a, ref=ref):
            r0 = pl.multiple_of(i * COPY_ROWS, COPY_ROWS)
            pad_scr[a, pl.ds(pl.multiple_of(r0 + PAD, PAD), COPY_ROWS), :] = ref[pl.ds(r0, COPY_ROWS), :]
            return carry

        lax.fori_loop(0, T // COPY_ROWS, copy_rows, 0)

    def conv_silu(a, cw_ref, c0):
        win = pad_scr.at[a, pl.ds(c0, C + 2 * PAD), :]
        off = PAD - DN_CONV // 2
        acc = win[off:off + C, :] * cw_ref[0:1, :]
        for j in range(1, DN_CONV):
            acc = acc + win[off + j:off + j + C, :] * cw_ref[j:j + 1, :]
        return _silu(acc)

    def l2norm(t):
        return t * lax.rsqrt(jnp.sum(t * t, axis=-1, keepdims=True) + EPS)

    def chunk_start(n):
        return n * C if isinstance(n, int) else pl.multiple_of(n * C, C)

    def prepare_chunks(ns):
        c0s = [chunk_start(n) for n in ns]
        qs = [l2norm(conv_silu(0, cwq_ref, c0)) * (DN_HEAD_DIM ** -0.5) for c0 in c0s]
        ks = [l2norm(conv_silu(1, cwk_ref, c0)) for c0 in c0s]
        vs = [conv_silu(2, cwv_ref, c0) for c0 in c0s]
        kbs = [k.astype(BF16) for k in ks]
        grams = [lax.dot_general(jnp.concatenate([q.astype(BF16), kb], axis=0), kb, NT_DIMS,
                                 preferred_element_type=F32) for q, kb in zip(qs, kbs)]
        l2s, rhss, kds = [], [], []
        for u in range(U):
            rows = pl.ds(c0s[u], C)
            q, k, v = qs[u], ks[u], vs[u]
            gr = grow_ref[:, rows]
            gt = jnp.concatenate([gr, jnp.zeros((LANES - N_GATES, C), F32)], axis=0).T
            col = lambda idx: jnp.sum(jnp.where(lane == idx, gt, 0.0), axis=1, keepdims=True)
            beta = (col(h), col(DN_HEADS + h))
            gcc = (col(2 * DN_HEADS + h), col(3 * DN_HEADS + h))
            sub = _iota(gr.shape, 0)
            row = lambda idx: jnp.sum(jnp.where(sub == idx, gr, 0.0), axis=0, keepdims=True)
            gcr = (row(2 * DN_HEADS + h), row(3 * DN_HEADS + h))
            glast = (gcr[0][:, C - 1:C], gcr[1][:, 0:1])
            qk = grams[u][:C]
            kk = grams[u][C:]
            l2 = []
            for d in range(2):
                decay = jnp.where(incl[d], jnp.exp(jnp.where(incl[d], gcc[d] - gcr[d], 0.0)), 0.0)
                l2.append(jnp.where(strict[d], beta[d] * kk * decay, 0.0))
                eg = jnp.exp(gcc[d])
                qg_scr[d, rows, :] = (q * eg).astype(BF16)
                kds.append((k * jnp.exp(glast[d] - gcc[d])).astype(BF16))
                at_scr[d, rows, :] = (qk * decay).astype(BF16)
                gl_scr[d, ns[u]] = jnp.broadcast_to(jnp.exp(glast[d]), (8, LANES))
                rhss.append(jnp.concatenate([v * beta[d], (k * beta[d]) * eg], axis=1).astype(BF16))
            l2s.append(jnp.concatenate(l2, axis=1))
        ainvs = _unit_tri_inverse_pairs(l2s)
        for u in range(U):
            rows = pl.ds(c0s[u], C)
            for d in range(2):
                uw = jnp.dot(ainvs[u][:, d * C:(d + 1) * C].astype(BF16), rhss[2 * u + d],
                             preferred_element_type=F32)
                uwb = uw.astype(BF16)
                u_scr[d, rows, :] = uw[:, :DN_HEAD_DIM]
                w_scr[d, rows, :] = uwb[:, DN_HEAD_DIM:]
                kuw = lax.dot_general(kds[2 * u + d], uwb, TN_DIMS, preferred_element_type=F32)
                ku_scr[d, rows, :] = kuw[:, :DN_HEAD_DIM]
                kw_scr[d, rows, :] = kuw[:, DN_HEAD_DIM:].astype(BF16)

    def scan_step(d, n, state):
        rows = pl.ds(chunk_start(n), C)
        sb = state.astype(BF16)
        sp_scr[d, rows, :] = sb
        return (state * gl_scr[d, n][0:1, :] + ku_scr[d, rows, :]
                - jnp.dot(kw_scr[d, rows, :], sb, preferred_element_type=F32))

    def scan_steps(i0, states):
        for j in range(half):
            i = i0 + j
            states = (scan_step(0, i, states[0]), scan_step(1, N - 1 - i, states[1]))
        return states

    def output_chunks(ns, between=()):
        pending = list(between)

        def emit(count):
            for _ in range(min(count, len(pending))):
                pending.pop(0)()

        rows = [pl.ds(chunk_start(n), C) for n in ns]
        items = [(r, d) for r in rows for d in range(2)]
        emit(1)
        sbs = [sp_scr[d, r, :] for r, d in items]
        ws_qs = [jnp.dot(jnp.concatenate([w_scr[d, r, :], qg_scr[d, r, :]], axis=0), sb,
                         preferred_element_type=F32) for (r, d), sb in zip(items, sbs)]
        emit(1)
        v_news = [(u_scr[d, r, :] - x[:C]).astype(BF16) for (r, d), x in zip(items, ws_qs)]
        os_ = [x[C:] + jnp.dot(at_scr[d, r, :], vn, preferred_element_type=F32)
               for (r, d), x, vn in zip(items, ws_qs, v_news)]
        emit(len(pending))
        for u, r in enumerate(rows):
            o = os_[2 * u] + os_[2 * u + 1]
            o = o * lax.rsqrt(jnp.mean(o * o, axis=-1, keepdims=True) + EPS) * nw_ref[...]
            o_ref[r, :] = (o * _silu(z_ref[r, :])).astype(BF16)

    half = U // 2
    G = N // U
    front_back = lambda g: ([g * half + j for j in range(half)]
                            + [N - (g + 1) * half + j for j in range(half)])
    finished = lambda g: ([N // 2 + g * half + j for j in range(half)]
                          + [N // 2 - 1 - g * half - j for j in range(half)])
    s0 = jnp.zeros((DN_HEAD_DIM, DN_HEAD_DIM), F32)
    prepare_chunks(front_back(0))

    def prepare_and_scan(g, states):
        states = scan_steps((g - 1) * half, states)
        prepare_chunks(front_back(g))
        return states

    states = lax.fori_loop(1, G, prepare_and_scan, (s0, s0))
    states = scan_steps((G - 1) * half, states)
    states = scan_steps(N // 2, states)

    def scan_and_output(g, states):
        holder = [states]

        def one_step(j):
            def run():
                i = N // 2 + g * half + j
                holder[0] = (scan_step(0, i, holder[0][0]), scan_step(1, N - 1 - i, holder[0][1]))
            return run

        output_chunks(finished(g - 1), between=[one_step(j) for j in range(half)])
        return holder[0]

    lax.fori_loop(1, G, scan_and_output, states)
    output_chunks(finished(G - 1))


def _deltanet(dqkv, dz, grow, conv_w, norm_w, B, T):
    H = DN_HEADS
    N = T // CHUNK
    seq = lambda off: pl.BlockSpec((T, DN_HEAD_DIM), lambda b, h: (b, off + h))
    cw = lambda off: pl.BlockSpec((DN_CONV, DN_HEAD_DIM), lambda b, h: (0, off + h))
    return pl.pallas_call(
        functools.partial(_deltanet_kernel, T=T),
        grid=(B, H),
        in_specs=[seq(0), seq(H), seq(2 * H), seq(0),
                  pl.BlockSpec((N_GATES, T), lambda b, h: (0, b)),
                  cw(0), cw(H), cw(2 * H),
                  pl.BlockSpec((1, DN_HEAD_DIM), lambda b, h: (0, 0))],
        out_specs=seq(0),
        out_shape=jax.ShapeDtypeStruct((B * T, DN_W), BF16),
        scratch_shapes=[pltpu.VMEM((3, T + 2 * PAD, DN_HEAD_DIM), F32),
                        pltpu.VMEM((2, T, DN_HEAD_DIM), F32),
                        pltpu.VMEM((2, T, DN_HEAD_DIM), BF16),
                        pltpu.VMEM((2, T, DN_HEAD_DIM), BF16),
                        pltpu.VMEM((2, T, CHUNK), BF16),
                        pltpu.VMEM((2, N, 8, LANES), F32),
                        pltpu.VMEM((2, T, DN_HEAD_DIM), F32),
                        pltpu.VMEM((2, T, DN_HEAD_DIM), BF16),
                        pltpu.VMEM((2, T, DN_HEAD_DIM), BF16)],
        compiler_params=_params("parallel", "arbitrary"),
        name="deltanet",
    )(dqkv, dqkv, dqkv, dz, grow, conv_w, conv_w, conv_w, norm_w)


def _mix_xattn_kernel(att_t_ref, dn_ref, x_ref, mem_ref, woa_ref, wod_ref, g_mix_ref, g_xa_ref, wq_ref,
                      g_mem_ref, wkv_ref, wo_ref, g_out_ref, x2_ref, kv_scr):
    @pl.when(pl.program_id(1) == 0)
    def _():
        kv_scr[...] = _mm(_rms(mem_ref[...], g_mem_ref[...]), wkv_ref[...]).astype(BF16)

    subs = [slice(i * TM, (i + 1) * TM) for i in range(SUB_TILES)]
    mixes = [lax.dot_general(att_t_ref[:, r], woa_ref[...], TN_DIMS, preferred_element_type=F32)
             + jnp.dot(dn_ref[r, :], wod_ref[...], preferred_element_type=F32) for r in subs]
    x1s = [x_ref[r, :] + _rms(mix, g_mix_ref[...]) for r, mix in zip(subs, mixes)]
    qxs = [(_mm(_rms(x1, g_xa_ref[...]), wq_ref[...]) * XA_Q_SCALE).astype(BF16) for x1 in x1s]
    items = [(i, hd) for i in range(SUB_TILES) for hd in range(X_HEADS)]
    ksl = lambda hd: slice(hd * X_HEAD_DIM, (hd + 1) * X_HEAD_DIM)
    vsl = lambda hd: slice(X_W + hd * X_HEAD_DIM, X_W + (hd + 1) * X_HEAD_DIM)
    ss = [lax.dot_general(qxs[i][:, ksl(hd)], kv_scr[:, ksl(hd)], NT_DIMS, preferred_element_type=F32)
          for i, hd in items]
    ps = [jnp.exp2(s - jnp.max(s, axis=-1, keepdims=True)) for s in ss]
    dens = [jnp.sum(p, axis=-1, keepdims=True) for p in ps]
    os_ = [jnp.dot(p.astype(BF16), kv_scr[:, vsl(hd)], preferred_element_type=F32) / den
           for (i, hd), p, den in zip(items, ps, dens)]
    for i, r in enumerate(subs):
        o = jnp.concatenate(os_[i * X_HEADS:(i + 1) * X_HEADS], axis=1)
        x2_ref[r, :] = x1s[i] + _rms(_mm(o, wo_ref[...]), g_out_ref[...])


def _mix_xattn(att_t, dn, x2d, mem2d, woa, wod, g_mix, g_xa, wq, g_mem, wkv, wo, g_out, B, T):
    n_t = T // TS
    mem_len = mem2d.shape[0] // B
    row = lambda b, i: (b * n_t + i, 0)
    full = lambda a: pl.BlockSpec(a.shape, lambda b, i: (0, 0))
    return pl.pallas_call(
        _mix_xattn_kernel,
        grid=(B, n_t),
        in_specs=[pl.BlockSpec((ATT_Q, TS), lambda b, i: (0, b * n_t + i)), pl.BlockSpec((TS, DN_W), row),
                  pl.BlockSpec((TS, D_MODEL), row), pl.BlockSpec((mem_len, D_MODEL), lambda b, i: (b, 0)),
                  full(woa), full(wod), full(g_mix), full(g_xa), full(wq), full(g_mem), full(wkv), full(wo),
                  full(g_out)],
        out_specs=pl.BlockSpec((TS, D_MODEL), row),
        out_shape=jax.ShapeDtypeStruct((B * T, D_MODEL), F32),
        scratch_shapes=[pltpu.VMEM((mem_len, 2 * X_W), BF16)],
        compiler_params=_params("parallel", "arbitrary"),
        name="mix_xattn",
    )(att_t, dn, x2d, mem2d, woa, wod, g_mix, g_xa, wq, g_mem, wkv, wo, g_out)


def _mlp_kernel(x_ref, g1_ref, w1_ref, w2_ref, g2_ref, y_ref):
    x = x_ref[...]
    hb = _rms(x, g1_ref[...]).astype(BF16)
    acc = jnp.zeros((TM, D_MODEL), F32)
    for c in range(D_FF // FF_BLOCK):
        sl = slice(c * FF_BLOCK, (c + 1) * FF_BLOCK)
        a = jnp.maximum(jnp.dot(hb, w1_ref[:, sl], preferred_element_type=F32), 0.0)
        acc = acc + jnp.dot((a * a).astype(BF16), w2_ref[sl, :], preferred_element_type=F32)
    y_ref[...] = x + _rms(acc, g2_ref[...])


def _mlp(x2, g1, w1, w2, g2):
    BT = x2.shape[0]
    row = lambda i: (i, 0)
    full = lambda a: pl.BlockSpec(a.shape, lambda i: (0, 0))
    return pl.pallas_call(
        _mlp_kernel,
        grid=(BT // TM,),
        in_specs=[pl.BlockSpec((TM, D_MODEL), row), full(g1), full(w1), full(w2), full(g2)],
        out_specs=pl.BlockSpec((TM, D_MODEL), row),
        out_shape=jax.ShapeDtypeStruct((BT, D_MODEL), F32),
        compiler_params=_params("parallel"),
        name="mlp",
    )(x2, g1, w1, w2, g2)


def _rope_tables(T):
    half = ROPE_DIM // 2
    inv = ROPE_THETA ** (-(jnp.arange(half, dtype=F32) * 2.0 / ROPE_DIM))
    ang = jnp.arange(T).astype(F32)[:, None] * inv[None, :]
    tile = lambda t: jnp.broadcast_to(t[:, None, :], (T, LANES // half, half)).reshape(T, LANES)
    cos, sin = tile(jnp.cos(ang)), tile(jnp.sin(ang))
    d = jnp.arange(LANES) % ATT_HEAD_DIM
    lo, hi = (d < half)[None, :], ((d >= half) & (d < ROPE_DIM))[None, :]
    return jnp.where(lo | hi, cos, 1.0), jnp.where(lo, -sin, 0.0), jnp.where(hi, sin, 0.0)


def _gate_params(a_log_f, a_log_b, dt_f, dt_b):
    zeros = jnp.zeros((2 * DN_HEADS,), F32)
    a_vec = jnp.concatenate([zeros, a_log_f.astype(F32), a_log_b.astype(F32)])
    d_vec = jnp.concatenate([zeros, dt_f.astype(F32), dt_b.astype(F32)])
    pad = jnp.zeros((LANES - N_GATES,), F32)
    return jnp.stack([jnp.concatenate([a_vec, pad]), jnp.concatenate([d_vec, pad])])


def _layer(x, mem, rope_tables, w_in, attn_sink, conv_w, a_log_f, a_log_b, dt_f, dt_b, dn_norm_w, w_out,
           xa_wq, xa_wkv, xa_wo, mem_g, w1, w2, g_pre_mix, g_post_mix, g_pre_xa, g_post_xa,
           g_pre_mlp, g_post_mlp):
    B, T, _ = x.shape
    assert T % TS == 0 and T % TQ_STEP == 0 and T % (PREP_UNROLL * CHUNK) == 0 and T >= TQ + 2 * ATT_WINDOW
    x2d = x.reshape(B * T, D_MODEL)
    vec = lambda g: g.astype(F32).reshape(1, -1)
    hpg = ATT_HEADS // ATT_KV_HEADS

    def pair_heads(w, axis):
        shape = w.shape
        w = w.reshape(shape[:axis] + (ATT_KV_HEADS, hpg, ATT_HEAD_DIM) + shape[axis + 1:])
        return jnp.swapaxes(w, axis, axis + 1).reshape(shape)

    w_proj = jnp.concatenate([pair_heads(w_in[:, :ATT_Q], 1).astype(BF16), w_in[:, ATT_Q:].astype(BF16),
                              jnp.zeros((D_MODEL, LANES - N_GATES), BF16)], axis=1)
    cos, sa, sb = rope_tables
    pc = _gate_params(a_log_f, a_log_b, dt_f, dt_b)

    aq, ak, avt, dqkv, dz, grow = _in_proj(x2d, T, vec(g_pre_mix), w_proj, cos, sa, sb, pc)
    att_t = _win_attn(aq, ak, avt, attn_sink.astype(F32), B, T)
    dn = _deltanet(dqkv, dz, grow, conv_w.astype(F32), vec(dn_norm_w), B, T)
    x2 = _mix_xattn(att_t, dn, x2d, mem.reshape(-1, D_MODEL),
                    pair_heads(w_out[:ATT_Q], 0).astype(BF16), w_out[ATT_Q:].astype(BF16),
                    vec(g_post_mix), vec(g_pre_xa), xa_wq.astype(BF16), vec(mem_g), xa_wkv.astype(BF16),
                    xa_wo.astype(BF16), vec(g_post_xa), B, T)
    y = _mlp(x2, vec(g_pre_mlp), w1.astype(BF16), w2.astype(BF16), vec(g_post_mlp))
    return y.reshape(B, T, D_MODEL)


def kernel(x_prompt, x_sample, mem_prompt, mem_sample, w_in, attn_sink, dn_conv_w, dn_A_log_f, dn_A_log_b,
           dn_dt_bias_f, dn_dt_bias_b, dn_norm_w, w_out, xa_wq, xa_wkv, xa_wo, mem_norm_g, mlp_w1, mlp_w2,
           norm_pre_mix, norm_post_mix, norm_pre_xa, norm_post_xa, norm_pre_mlp, norm_post_mlp):
    weights = (w_in, attn_sink, dn_conv_w, dn_A_log_f, dn_A_log_b, dn_dt_bias_f, dn_dt_bias_b, dn_norm_w,
               w_out, xa_wq, xa_wkv, xa_wo, mem_norm_g, mlp_w1, mlp_w2, norm_pre_mix, norm_post_mix,
               norm_pre_xa, norm_post_xa, norm_pre_mlp, norm_post_mlp)
    rope_tables = _rope_tables(max(x_prompt.shape[1], x_sample.shape[1]))
    outs = []
    for x, mem in ((x_prompt, mem_prompt), (x_sample, mem_sample)):
        for l in range(w_in.shape[0]):
            x = _layer(x, mem, rope_tables, *(w[l] for w in weights))
        outs.append(x)
    return tuple(outs)
```

```python
import functools

import jax
import jax.numpy as jnp
from jax import lax
from jax.experimental import pallas as pl
from jax.experimental.pallas import tpu as pltpu

F32 = jnp.float32
BF16 = jnp.bfloat16

D_MODEL = 1024
ATT_HEADS = 8
ATT_KV_HEADS = 2
ATT_HEAD_DIM = 64
ATT_WINDOW = 128
ROPE_THETA = 500000.0
ROPE_DIM = ATT_HEAD_DIM // 4
DN_HEADS = 4
DN_HEAD_DIM = 128
DN_CONV = 5
X_HEADS = 4
X_HEAD_DIM = 128
D_FF = 4 * D_MODEL
EPS = 1e-6

ATT_Q = ATT_HEADS * ATT_HEAD_DIM
ATT_KV = ATT_KV_HEADS * ATT_HEAD_DIM
DN_W = DN_HEADS * DN_HEAD_DIM
X_W = X_HEADS * X_HEAD_DIM
O_AK = ATT_Q
O_AV = O_AK + ATT_KV
O_DQKV = O_AV + ATT_KV
O_DZ = O_DQKV + 3 * DN_W
O_GATES = O_DZ + DN_W
N_GATES = 4 * DN_HEADS

LANES = 128
TM = 512
TQ = 128
TQ_STEP = 512
CHUNK = 128
PREP_UNROLL = 8
PAD = 8
COPY_ROWS = 512
SUB_TILES = 2
TS = SUB_TILES * TM
FF_BLOCK = 1024
VMEM_LIMIT = 56 * 1024 * 1024

LOG2E = 1.4426950408889634
ATT_Q_SCALE = ATT_HEAD_DIM ** -0.5 * LOG2E
XA_Q_SCALE = X_HEAD_DIM ** -0.5 * LOG2E

NT_DIMS = (((1,), (1,)), ((), ()))
TN_DIMS = (((0,), (0,)), ((), ()))


def _mm(a, b):
    return jnp.dot(a.astype(BF16), b.astype(BF16), preferred_element_type=F32)


def _mm_nt(a, b):
    return lax.dot_general(a.astype(BF16), b.astype(BF16), NT_DIMS, preferred_element_type=F32)


def _mm_tn(a, b):
    return lax.dot_general(a.astype(BF16), b.astype(BF16), TN_DIMS, preferred_element_type=F32)


def _rms(x, g):
    return x * lax.rsqrt(jnp.mean(x * x, axis=-1, keepdims=True) + EPS) * g


def _softplus(x):
    return jnp.maximum(x, 0.0) + jnp.log1p(jnp.exp(-jnp.abs(x)))


def _silu(x):
    return x * jax.nn.sigmoid(x)


def _iota(shape, dim):
    return lax.broadcasted_iota(jnp.int32, shape, dim)


def _params(*sem):
    return pltpu.CompilerParams(dimension_semantics=sem, vmem_limit_bytes=VMEM_LIMIT)


def _in_proj_kernel(x_ref, g_ref, w_ref, cos_ref, sa_ref, sb_ref, pc_ref,
                    aq_ref, ak_ref, avt_ref, dqkv_ref, dz_ref, grow_ref):
    subs = [slice(i * TM, (i + 1) * TM) for i in range(SUB_TILES)]
    hbs = [_rms(x_ref[r, :], g_ref[...]).astype(BF16) for r in subs]

    def rope(t, r):
        return (t * cos_ref[r, :] + pltpu.roll(t, LANES - ROPE_DIM // 2, 1) * sa_ref[r, :]
                + pltpu.roll(t, ROPE_DIM // 2, 1) * sb_ref[r, :])

    qkvs = [jnp.dot(hb, w_ref[:, :O_DQKV], preferred_element_type=F32) for hb in hbs]
    for r, hb in zip(subs, hbs):
        dqkv_ref[r, :] = jnp.dot(hb, w_ref[:, O_DQKV:O_DZ], preferred_element_type=F32)
    zgs = [jnp.dot(hb, w_ref[:, O_DZ:], preferred_element_type=F32) for hb in hbs]
    for r, qkv, zg in zip(subs, qkvs, zgs):
        for j in range(ATT_Q // LANES):
            sl = slice(j * LANES, (j + 1) * LANES)
            aq_ref[r, sl] = (rope(qkv[:, sl], r) * ATT_Q_SCALE).astype(BF16)
        ak_ref[r, :] = rope(qkv[:, O_AK:O_AV], r).astype(BF16)
        avt_ref[:, r] = qkv[:, O_AV:O_DQKV].astype(BF16).T
        dz_ref[r, :] = zg[:, :DN_W]

        raw = zg[:, DN_W:]
        lane = _iota(raw.shape, 1)
        gval = -jnp.exp(pc_ref[0:1, :]) * _softplus(raw + pc_ref[1:2, :])
        gates = jnp.where(lane < 2 * DN_HEADS, jax.nn.sigmoid(raw), gval)
        pos = _iota(raw.shape, 0) & (CHUNK - 1)
        cf = gates
        cb = gates
        s = 1
        while s < CHUNK:
            cf = cf + jnp.where(pos >= s, pltpu.roll(cf, s, 0), 0.0)
            cb = cb + jnp.where(pos < CHUNK - s, pltpu.roll(cb, TM - s, 0), 0.0)
            s *= 2
        gcol = jnp.where(lane < 2 * DN_HEADS, gates, jnp.where(lane < 3 * DN_HEADS, cf, cb))
        grow_ref[:, r] = gcol.T[:N_GATES]


def _in_proj(x2d, T, g, w, cos, sa, sb, pc):
    BT = x2d.shape[0]
    n_t = T // TS
    row = lambda i: (i, 0)
    const = lambda i: (0, 0)
    tab = lambda i: (i % n_t, 0)
    full = lambda a: pl.BlockSpec(a.shape, const)
    return pl.pallas_call(
        _in_proj_kernel,
        grid=(BT // TS,),
        in_specs=[pl.BlockSpec((TS, D_MODEL), row), full(g), full(w),
                  pl.BlockSpec((TS, LANES), tab), pl.BlockSpec((TS, LANES), tab), pl.BlockSpec((TS, LANES), tab),
                  full(pc)],
        out_specs=[pl.BlockSpec((TS, ATT_Q), row), pl.BlockSpec((TS, ATT_KV), row),
                   pl.BlockSpec((ATT_KV, TS), lambda i: (0, i)),
                   pl.BlockSpec((TS, 3 * DN_W), row), pl.BlockSpec((TS, DN_W), row),
                   pl.BlockSpec((N_GATES, TS), lambda i: (0, i))],
        out_shape=[jax.ShapeDtypeStruct((BT, ATT_Q), BF16), jax.ShapeDtypeStruct((BT, ATT_KV), BF16),
                   jax.ShapeDtypeStruct((ATT_KV, BT), BF16),
                   jax.ShapeDtypeStruct((BT, 3 * DN_W), F32), jax.ShapeDtypeStruct((BT, DN_W), F32),
                   jax.ShapeDtypeStruct((N_GATES, BT), F32)],
        compiler_params=_params("parallel"),
        name="in_proj",
    )(x2d, g, w, cos, sa, sb, pc)


def _win_attn_kernel(sink_ref, q_ref, k_ref, vt_ref, o_ref, *, T):
    W = TQ + 2 * ATT_WINDOW
    hpg = ATT_HEADS // ATT_KV_HEADS
    rel = _iota((W, TQ), 0) - _iota((W, TQ), 1)
    items = []
    for sub in range(TQ_STEP // TQ):
        t0 = pl.program_id(1) * TQ_STEP + sub * TQ
        start = pl.multiple_of(jnp.clip(t0 - ATT_WINDOW, 0, T - W), LANES)
        k = k_ref[pl.ds(start, W), :]
        vt = vt_ref[:, pl.ds(start, W)]
        k_lo = _iota(k.shape, 1) < ATT_HEAD_DIM
        v_lo = _iota(vt.shape, 0) < ATT_HEAD_DIM
        kk = jnp.concatenate([jnp.where(k_lo, k, jnp.zeros_like(k)), jnp.where(k_lo, jnp.zeros_like(k), k)], axis=0)
        vvt = jnp.concatenate([jnp.where(v_lo, vt, jnp.zeros_like(vt)), jnp.where(v_lo, jnp.zeros_like(vt), vt)],
                              axis=1)
        cap = jnp.where(jnp.abs(rel + (start - t0)) <= ATT_WINDOW, jnp.inf, -1e30)
        cap = jnp.concatenate([cap, cap], axis=1)
        rows = slice(sub * TQ, (sub + 1) * TQ)
        for pair in range(ATT_Q // LANES // 2):
            groups = (2 * pair, 2 * pair + 1)
            qp = jnp.concatenate([q_ref[rows, g * LANES:(g + 1) * LANES] for g in groups], axis=0)
            items.append(dict(kk=kk, vvt=vvt, cap=cap, qp=qp, rows=rows, groups=groups))
    for it in items:
        it["st"] = lax.dot_general(it["kk"], it["qp"], NT_DIMS, preferred_element_type=F32)
    for it in items:
        ps, dens = [], []
        for kvh in range(ATT_KV_HEADS):
            sh = jnp.minimum(it["st"][kvh * W:(kvh + 1) * W], it["cap"])
            sk = jnp.concatenate([jnp.full((1, TQ), sink_ref[kvh * hpg + g] * LOG2E, F32) for g in it["groups"]],
                                 axis=1)
            m = jnp.maximum(jnp.max(sh, axis=0, keepdims=True), sk)
            p = jnp.exp2(sh - m)
            dens.append(jnp.sum(p, axis=0, keepdims=True) + jnp.exp2(sk - m))
            ps.append(p.astype(BF16))
        it["p"] = jnp.concatenate(ps, axis=0)
        it["dens"] = dens
    for it in items:
        ot = jnp.dot(it["vvt"], it["p"], preferred_element_type=F32)
        ot = ot / jnp.where(_iota(ot.shape, 0) < ATT_HEAD_DIM, it["dens"][0], it["dens"][1])
        for i, g in enumerate(it["groups"]):
            o_ref[g * LANES:(g + 1) * LANES, it["rows"]] = ot[:, i * TQ:(i + 1) * TQ].astype(BF16)


def _win_attn(aq, ak, avt, sink, B, T):
    n_q = T // TQ_STEP
    return pl.pallas_call(
        functools.partial(_win_attn_kernel, T=T),
        grid=(B, n_q),
        in_specs=[pl.BlockSpec(memory_space=pltpu.SMEM),
                  pl.BlockSpec((TQ_STEP, ATT_Q), lambda b, i: (b * n_q + i, 0)),
                  pl.BlockSpec((T, ATT_KV), lambda b, i: (b, 0)),
                  pl.BlockSpec((ATT_KV, T), lambda b, i: (0, b))],
        out_specs=pl.BlockSpec((ATT_Q, TQ_STEP), lambda b, i: (0, b * n_q + i)),
        out_shape=jax.ShapeDtypeStruct((ATT_Q, B * T), BF16),
        compiler_params=_params("parallel", "arbitrary"),
        name="win_attn",
    )(sink, aq, ak, avt)


def _mm_pair(xb, yb):
    return jnp.concatenate([jnp.dot(xb[:, :CHUNK], yb[:, :CHUNK], preferred_element_type=F32),
                            jnp.dot(xb[:, CHUNK:], yb[:, CHUNK:], preferred_element_type=F32)], axis=1)


def _unit_tri_inverse_pairs(l2s):
    bf = lambda xs: [x.astype(BF16) for x in xs]
    r = _iota((CHUNK, 2 * CHUNK), 0)
    c = _iota((CHUNK, 2 * CHUNK), 1) & (CHUNK - 1)
    base = 16
    in_base = (r // base) == (c // base)
    ds = [jnp.where(in_base, l2, 0.0) for l2 in l2s]
    ps = [jnp.where(r == c, 1.0, 0.0) - d for d in ds]
    powb = bf(ds)
    for _ in range(3):
        powb = bf([_mm_pair(x, x) for x in powb])
        ps = [p + _mm_pair(pb, x) for p, pb, x in zip(ps, bf(ps), powb)]
    s = base
    while s < CHUNK:
        off_diag = ((r // (2 * s)) == (c // (2 * s))) & ((r // s) != (c // s))
        mbs = bf([jnp.where(off_diag, l2, 0.0) for l2 in l2s])
        pbs = bf(ps)
        tbs = bf([_mm_pair(mb, pb) for mb, pb in zip(mbs, pbs)])
        ps = [p - _mm_pair(pb, tb) for p, pb, tb in zip(ps, pbs, tbs)]
        s *= 2
    return ps


def _deltanet_kernel(q_ref, k_ref, v_ref, z_ref, grow_ref, cwq_ref, cwk_ref, cwv_ref, nw_ref,
                     o_ref, pad_scr, u_scr, w_scr, qg_scr, at_scr, gl_scr, ku_scr, kw_scr, sp_scr, *, T):
    C = CHUNK
    N = T // C
    U = PREP_UNROLL
    h = pl.program_id(1)
    ri = _iota((C, C), 0)
    ci = _iota((C, C), 1)
    incl = (ri >= ci, ri <= ci)
    strict = (ri > ci, ri < ci)
    lane = _iota((C, LANES), 1)

    zpad = jnp.zeros((PAD, DN_HEAD_DIM), F32)
    for a, ref in enumerate((q_ref, k_ref, v_ref)):
        pad_scr[a, 0:PAD, :] = zpad
        pad_scr[a, T + PAD:T + 2 * PAD, :] = zpad

        def copy_rows(i, carry, a=a, ref=ref):
            r0 = pl.multiple_of(i * COPY_ROWS, COPY_ROWS)
            pad_scr[a, pl.ds(pl.multiple_of(r0 + PAD, PAD), COPY_ROWS), :] = ref[pl.ds(r0, COPY_ROWS), :]
            return carry

        lax.fori_loop(0, T // COPY_ROWS, copy_rows, 0)

    def conv_silu(a, cw_ref, c0):
        win = pad_scr.at[a, pl.ds(c0, C + 2 * PAD), :]
        off = PAD - DN_CONV // 2
        acc = win[off:off + C, :] * cw_ref[0:1, :]
        for j in range(1, DN_CONV):
            acc = acc + win[off + j:off + j + C, :] * cw_ref[j:j + 1, :]
        return _silu(acc)

    def l2norm(t):
        return t * lax.rsqrt(jnp.sum(t * t, axis=-1, keepdims=True) + EPS)

    def chunk_start(n):
        return n * C if isinstance(n, int) else pl.multiple_of(n * C, C)

    def prepare_chunks(ns):
        c0s = [chunk_start(n) for n in ns]
        qs = [l2norm(conv_silu(0, cwq_ref, c0)) * (DN_HEAD_DIM ** -0.5) for c0 in c0s]
        ks = [l2norm(conv_silu(1, cwk_ref, c0)) for c0 in c0s]
        vs = [conv_silu(2, cwv_ref, c0) for c0 in c0s]
        kbs = [k.astype(BF16) for k in ks]
        grams = [lax.dot_general(jnp.concatenate([q.astype(BF16), kb], axis=0), kb, NT_DIMS,
                                 preferred_element_type=F32) for q, kb in zip(qs, kbs)]
        l2s, rhss, kds = [], [], []
        for u in range(U):
            rows = pl.ds(c0s[u], C)
            q, k, v = qs[u], ks[u], vs[u]
            gr = grow_ref[:, rows]
            gt = jnp.concatenate([gr, jnp.zeros((LANES - N_GATES, C), F32)], axis=0).T
            col = lambda idx: jnp.sum(jnp.where(lane == idx, gt, 0.0), axis=1, keepdims=True)
            beta = (col(h), col(DN_HEADS + h))
            gcc = (col(2 * DN_HEADS + h), col(3 * DN_HEADS + h))
            sub = _iota(gr.shape, 0)
            row = lambda idx: jnp.sum(jnp.where(sub == idx, gr, 0.0), axis=0, keepdims=True)
            gcr = (row(2 * DN_HEADS + h), row(3 * DN_HEADS + h))
            glast = (gcr[0][:, C - 1:C], gcr[1][:, 0:1])
            qk = grams[u][:C]
            kk = grams[u][C:]
            l2 = []
            for d in range(2):
                decay = jnp.where(incl[d], jnp.exp(jnp.where(incl[d], gcc[d] - gcr[d], 0.0)), 0.0)
                l2.append(jnp.where(strict[d], beta[d] * kk * decay, 0.0))
                eg = jnp.exp(gcc[d])
                qg_scr[d, rows, :] = (q * eg).astype(BF16)
                kds.append((k * jnp.exp(glast[d] - gcc[d])).astype(BF16))
                at_scr[d, rows, :] = (qk * decay).astype(BF16)
                gl_scr[d, ns[u]] = jnp.broadcast_to(jnp.exp(glast[d]), (8, LANES))
                rhss.append(jnp.concatenate([v * beta[d], (k * beta[d]) * eg], axis=1).astype(BF16))
            l2s.append(jnp.concatenate(l2, axis=1))
        ainvs = _unit_tri_inverse_pairs(l2s)
        for u in range(U):
            rows = pl.ds(c0s[u], C)
            for d in range(2):
                uw = jnp.dot(ainvs[u][:, d * C:(d + 1) * C].astype(BF16), rhss[2 * u + d],
                             preferred_element_type=F32)
                uwb = uw.astype(BF16)
                u_scr[d, rows, :] = uw[:, :DN_HEAD_DIM]
                w_scr[d, rows, :] = uwb[:, DN_HEAD_DIM:]
                kuw = lax.dot_general(kds[2 * u + d], uwb, TN_DIMS, preferred_element_type=F32)
                ku_scr[d, rows, :] = kuw[:, :DN_HEAD_DIM]
                kw_scr[d, rows, :] = kuw[:, DN_HEAD_DIM:].astype(BF16)

    def scan_step(d, n, state):
        rows = pl.ds(chunk_start(n), C)
        sb = state.astype(BF16)
        sp_scr[d, rows, :] = sb
        return (state * gl_scr[d, n][0:1, :] + ku_scr[d, rows, :]
                - jnp.dot(kw_scr[d, rows, :], sb, preferred_element_type=F32))

    def scan_steps(i0, states):
        for j in range(half):
            i = i0 + j
            states = (scan_step(0, i, states[0]), scan_step(1, N - 1 - i, states[1]))
        return states

    def output_chunks(ns, between=()):
        pending = list(between)

        def emit(count):
            for _ in range(min(count, len(pending))):
                pending.pop(0)()

        rows = [pl.ds(chunk_start(n), C) for n in ns]
        items = [(r, d) for r in rows for d in range(2)]
        emit(1)
        sbs = [sp_scr[d, r, :] for r, d in items]
        ws_qs = [jnp.dot(jnp.concatenate([w_scr[d, r, :], qg_scr[d, r, :]], axis=0), sb,
                         preferred_element_type=F32) for (r, d), sb in zip(items, sbs)]
        emit(1)
        v_news = [(u_scr[d, r, :] - x[:C]).astype(BF16) for (r, d), x in zip(items, ws_qs)]
        os_ = [x[C:] + jnp.dot(at_scr[d, r, :], vn, preferred_element_type=F32)
               for (r, d), x, vn in zip(items, ws_qs, v_news)]
        emit(len(pending))
        for u, r in enumerate(rows):
            o = os_[2 * u] + os_[2 * u + 1]
            o = o * lax.rsqrt(jnp.mean(o * o, axis=-1, keepdims=True) + EPS) * nw_ref[...]
            o_ref[r, :] = (o * _silu(z_ref[r, :])).astype(BF16)

    half = U // 2
    G = N // U
    front_back = lambda g: ([g * half + j for j in range(half)]
                            + [N - (g + 1) * half + j for j in range(half)])
    finished = lambda g: ([N // 2 + g * half + j for j in range(half)]
                          + [N // 2 - 1 - g * half - j for j in range(half)])
    s0 = jnp.zeros((DN_HEAD_DIM, DN_HEAD_DIM), F32)
    prepare_chunks(front_back(0))

    def prepare_and_scan(g, states):
        states = scan_steps((g - 1) * half, states)
        prepare_chunks(front_back(g))
        return states

    states = lax.fori_loop(1, G, prepare_and_scan, (s0, s0))
    states = scan_steps((G - 1) * half, states)
    states = scan_steps(N // 2, states)

    def scan_and_output(g, states):
        holder = [states]

        def one_step(j):
            def run():
                i = N // 2 + g * half + j
                holder[0] = (scan_step(0, i, holder[0][0]), scan_step(1, N - 1 - i, holder[0][1]))
            return run

        output_chunks(finished(g - 1), between=[one_step(j) for j in range(half)])
        return holder[0]

    lax.fori_loop(1, G, scan_and_output, states)
    output_chunks(finished(G - 1))


def _deltanet(dqkv, dz, grow, conv_w, norm_w, B, T):
    H = DN_HEADS
    N = T // CHUNK
    seq = lambda off: pl.BlockSpec((T, DN_HEAD_DIM), lambda b, h: (b, off + h))
    cw = lambda off: pl.BlockSpec((DN_CONV, DN_HEAD_DIM), lambda b, h: (0, off + h))
    return pl.pallas_call(
        functools.partial(_deltanet_kernel, T=T),
        grid=(B, H),
        in_specs=[seq(0), seq(H), seq(2 * H), seq(0),
                  pl.BlockSpec((N_GATES, T), lambda b, h: (0, b)),
                  cw(0), cw(H), cw(2 * H),
                  pl.BlockSpec((1, DN_HEAD_DIM), lambda b, h: (0, 0))],
        out_specs=seq(0),
        out_shape=jax.ShapeDtypeStruct((B * T, DN_W), BF16),
        scratch_shapes=[pltpu.VMEM((3, T + 2 * PAD, DN_HEAD_DIM), F32),
                        pltpu.VMEM((2, T, DN_HEAD_DIM), F32),
                        pltpu.VMEM((2, T, DN_HEAD_DIM), BF16),
                        pltpu.VMEM((2, T, DN_HEAD_DIM), BF16),
                        pltpu.VMEM((2, T, CHUNK), BF16),
                        pltpu.VMEM((2, N, 8, LANES), F32),
                        pltpu.VMEM((2, T, DN_HEAD_DIM), F32),
                        pltpu.VMEM((2, T, DN_HEAD_DIM), BF16),
                        pltpu.VMEM((2, T, DN_HEAD_DIM), BF16)],
        compiler_params=_params("parallel", "arbitrary"),
        name="deltanet",
    )(dqkv, dqkv, dqkv, dz, grow, conv_w, conv_w, conv_w, norm_w)


def _mix_xattn_kernel(att_t_ref, dn_ref, x_ref, mem_ref, woa_ref, wod_ref, g_mix_ref, g_xa_ref, wq_ref,
                      g_mem_ref, wkv_ref, wo_ref, g_out_ref, x2_ref, kv_scr):
    @pl.when(pl.program_id(1) == 0)
    def _():
        kv_scr[...] = _mm(_rms(mem_ref[...], g_mem_ref[...]), wkv_ref[...]).astype(BF16)

    subs = [slice(i * TM, (i + 1) * TM) for i in range(SUB_TILES)]
    mixes = [lax.dot_general(att_t_ref[:, r], woa_ref[...], TN_DIMS, preferred_element_type=F32)
             + jnp.dot(dn_ref[r, :], wod_ref[...], preferred_element_type=F32) for r in subs]
    x1s = [x_ref[r, :] + _rms(mix, g_mix_ref[...]) for r, mix in zip(subs, mixes)]
    qxs = [(_mm(_rms(x1, g_xa_ref[...]), wq_ref[...]) * XA_Q_SCALE).astype(BF16) for x1 in x1s]
    items = [(i, hd) for i in range(SUB_TILES) for hd in range(X_HEADS)]
    ksl = lambda hd: slice(hd * X_HEAD_DIM, (hd + 1) * X_HEAD_DIM)
    vsl = lambda hd: slice(X_W + hd * X_HEAD_DIM, X_W + (hd + 1) * X_HEAD_DIM)
    ss = [lax.dot_general(qxs[i][:, ksl(hd)], kv_scr[:, ksl(hd)], NT_DIMS, preferred_element_type=F32)
          for i, hd in items]
    ps = [jnp.exp2(s - jnp.max(s, axis=-1, keepdims=True)) for s in ss]
    dens = [jnp.sum(p, axis=-1, keepdims=True) for p in ps]
    os_ = [jnp.dot(p.astype(BF16), kv_scr[:, vsl(hd)], preferred_element_type=F32) / den
           for (i, hd), p, den in zip(items, ps, dens)]
    for i, r in enumerate(subs):
        o = jnp.concatenate(os_[i * X_HEADS:(i + 1) * X_HEADS], axis=1)
        x2_ref[r, :] = x1s[i] + _rms(_mm(o, wo_ref[...]), g_out_ref[...])


def _mix_xattn(att_t, dn, x2d, mem2d, woa, wod, g_mix, g_xa, wq, g_mem, wkv, wo, g_out, B, T):
    n_t = T // TS
    mem_len = mem2d.shape[0] // B
    row = lambda b, i: (b * n_t + i, 0)
    full = lambda a: pl.BlockSpec(a.shape, lambda b, i: (0, 0))
    return pl.pallas_call(
        _mix_xattn_kernel,
        grid=(B, n_t),
        in_specs=[pl.BlockSpec((ATT_Q, TS), lambda b, i: (0, b * n_t + i)), pl.BlockSpec((TS, DN_W), row),
                  pl.BlockSpec((TS, D_MODEL), row), pl.BlockSpec((mem_len, D_MODEL), lambda b, i: (b, 0)),
                  full(woa), full(wod), full(g_mix), full(g_xa), full(wq), full(g_mem), full(wkv), full(wo),
                  full(g_out)],
        out_specs=pl.BlockSpec((TS, D_MODEL), row),
        out_shape=jax.ShapeDtypeStruct((B * T, D_MODEL), F32),
        scratch_shapes=[pltpu.VMEM((mem_len, 2 * X_W), BF16)],
        compiler_params=_params("parallel", "arbitrary"),
        name="mix_xattn",
    )(att_t, dn, x2d, mem2d, woa, wod, g_mix, g_xa, wq, g_mem, wkv, wo, g_out)


def _mlp_kernel(x_ref, g1_ref, w1_ref, w2_ref, g2_ref, y_ref):
    subs = [slice(i * TM, (i + 1) * TM) for i in range(SUB_TILES)]
    hbs = [_rms(x_ref[r, :], g1_ref[...]).astype(BF16) for r in subs]
    accs = [jnp.zeros((TM, D_MODEL), F32) for _ in subs]
    for c in range(D_FF // FF_BLOCK):
        sl = slice(c * FF_BLOCK, (c + 1) * FF_BLOCK)
        hidden = [jnp.maximum(jnp.dot(hb, w1_ref[:, sl], preferred_element_type=F32), 0.0) for hb in hbs]
        accs = [acc + jnp.dot((a * a).astype(BF16), w2_ref[sl, :], preferred_element_type=F32)
                for acc, a in zip(accs, hidden)]
    for r, acc in zip(subs, accs):
        y_ref[r, :] = x_ref[r, :] + _rms(acc, g2_ref[...])


def _mlp(x2, g1, w1, w2, g2):
    BT = x2.shape[0]
    row = lambda i: (i, 0)
    full = lambda a: pl.BlockSpec(a.shape, lambda i: (0, 0), pipeline_mode=pl.Buffered(1))
    return pl.pallas_call(
        _mlp_kernel,
        grid=(BT // TS,),
        in_specs=[pl.BlockSpec((TS, D_MODEL), row), full(g1), full(w1), full(w2), full(g2)],
        out_specs=pl.BlockSpec((TS, D_MODEL), row),
        out_shape=jax.ShapeDtypeStruct((BT, D_MODEL), F32),
        compiler_params=_params("parallel"),
        name="mlp",
    )(x2, g1, w1, w2, g2)


def _rope_tables(T):
    half = ROPE_DIM // 2
    inv = ROPE_THETA ** (-(jnp.arange(half, dtype=F32) * 2.0 / ROPE_DIM))
    ang = jnp.arange(T).astype(F32)[:, None] * inv[None, :]
    tile = lambda t: jnp.broadcast_to(t[:, None, :], (T, LANES // half, half)).reshape(T, LANES)
    cos, sin = tile(jnp.cos(ang)), tile(jnp.sin(ang))
    d = jnp.arange(LANES) % ATT_HEAD_DIM
    lo, hi = (d < half)[None, :], ((d >= half) & (d < ROPE_DIM))[None, :]
    return jnp.where(lo | hi, cos, 1.0), jnp.where(lo, -sin, 0.0), jnp.where(hi, sin, 0.0)


def _gate_params(a_log_f, a_log_b, dt_f, dt_b):
    zeros = jnp.zeros((2 * DN_HEADS,), F32)
    a_vec = jnp.concatenate([zeros, a_log_f.astype(F32), a_log_b.astype(F32)])
    d_vec = jnp.concatenate([zeros, dt_f.astype(F32), dt_b.astype(F32)])
    pad = jnp.zeros((LANES - N_GATES,), F32)
    return jnp.stack([jnp.concatenate([a_vec, pad]), jnp.concatenate([d_vec, pad])])


def _layer(x, mem, rope_tables, w_in, attn_sink, conv_w, a_log_f, a_log_b, dt_f, dt_b, dn_norm_w, w_out,
           xa_wq, xa_wkv, xa_wo, mem_g, w1, w2, g_pre_mix, g_post_mix, g_pre_xa, g_post_xa,
           g_pre_mlp, g_post_mlp):
    B, T, _ = x.shape
    assert T % TS == 0 and T % TQ_STEP == 0 and T % (PREP_UNROLL * CHUNK) == 0 and T >= TQ + 2 * ATT_WINDOW
    x2d = x.reshape(B * T, D_MODEL)
    vec = lambda g: g.astype(F32).reshape(1, -1)
    hpg = ATT_HEADS // ATT_KV_HEADS

    def pair_heads(w, axis):
        shape = w.shape
        w = w.reshape(shape[:axis] + (ATT_KV_HEADS, hpg, ATT_HEAD_DIM) + shape[axis + 1:])
        return jnp.swapaxes(w, axis, axis + 1).reshape(shape)

    w_proj = jnp.concatenate([pair_heads(w_in[:, :ATT_Q], 1).astype(BF16), w_in[:, ATT_Q:].astype(BF16),
                              jnp.zeros((D_MODEL, LANES - N_GATES), BF16)], axis=1)
    cos, sa, sb = rope_tables
    pc = _gate_params(a_log_f, a_log_b, dt_f, dt_b)

    aq, ak, avt, dqkv, dz, grow = _in_proj(x2d, T, vec(g_pre_mix), w_proj, cos, sa, sb, pc)
    att_t = _win_attn(aq, ak, avt, attn_sink.astype(F32), B, T)
    dn = _deltanet(dqkv, dz, grow, conv_w.astype(F32), vec(dn_norm_w), B, T)
    x2 = _mix_xattn(att_t, dn, x2d, mem.reshape(-1, D_MODEL),
                    pair_heads(w_out[:ATT_Q], 0).astype(BF16), w_out[ATT_Q:].astype(BF16),
                    vec(g_post_mix), vec(g_pre_xa), xa_wq.astype(BF16), vec(mem_g), xa_wkv.astype(BF16),
                    xa_wo.astype(BF16), vec(g_post_xa), B, T)
    y = _mlp(x2, vec(g_pre_mlp), w1.astype(BF16), w2.astype(BF16), vec(g_post_mlp))
    return y.reshape(B, T, D_MODEL)


def kernel(x_prompt, x_sample, mem_prompt, mem_sample, w_in, attn_sink, dn_conv_w, dn_A_log_f, dn_A_log_b,
           dn_dt_bias_f, dn_dt_bias_b, dn_norm_w, w_out, xa_wq, xa_wkv, xa_wo, mem_norm_g, mlp_w1, mlp_w2,
           norm_pre_mix, norm_post_mix, norm_pre_xa, norm_post_xa, norm_pre_mlp, norm_post_mlp):
    weights = (w_in, attn_sink, dn_conv_w, dn_A_log_f, dn_A_log_b, dn_dt_bias_f, dn_dt_bias_b, dn_norm_w,
               w_out, xa_wq, xa_wkv, xa_wo, mem_norm_g, mlp_w1, mlp_w2, norm_pre_mix, norm_post_mix,
               norm_pre_xa, norm_post_xa, norm_pre_mlp, norm_post_mlp)
    rope_tables = _rope_tables(max(x_prompt.shape[1], x_sample.shape[1]))
    outs = []
    for x, mem in ((x_prompt, mem_prompt), (x_sample, mem_sample)):
        for l in range(w_in.shape[0]):
            x = _layer(x, mem, rope_tables, *(w[l] for w in weights))
        outs.append(x)
    return tuple(outs)
```

```python
import functools

import jax
import jax.numpy as jnp
from jax import lax
from jax.experimental import pallas as pl
from jax.experimental.pallas import tpu as pltpu

F32 = jnp.float32
BF16 = jnp.bfloat16

D_MODEL = 1024
ATT_HEADS = 8
ATT_KV_HEADS = 2
ATT_HEAD_DIM = 64
ATT_WINDOW = 128
ROPE_THETA = 500000.0
ROPE_DIM = ATT_HEAD_DIM // 4
DN_HEADS = 4
DN_HEAD_DIM = 128
DN_CONV = 5
X_HEADS = 4
X_HEAD_DIM = 128
D_FF = 4 * D_MODEL
EPS = 1e-6

ATT_Q = ATT_HEADS * ATT_HEAD_DIM
ATT_KV = ATT_KV_HEADS * ATT_HEAD_DIM
DN_W = DN_HEADS * DN_HEAD_DIM
X_W = X_HEADS * X_HEAD_DIM
O_AK = ATT_Q
O_AV = O_AK + ATT_KV
O_DQKV = O_AV + ATT_KV
O_DZ = O_DQKV + 3 * DN_W
O_GATES = O_DZ + DN_W
N_GATES = 4 * DN_HEADS

LANES = 128
TM = 512
TQ = 128
TQ_STEP = 1024
CHUNK = 128
PREP_UNROLL = 8
PAD = 8
COPY_ROWS = 512
SUB_TILES = 2
TS = SUB_TILES * TM
FF_BLOCK = 1024
VMEM_LIMIT = 56 * 1024 * 1024

LOG2E = 1.4426950408889634
ATT_Q_SCALE = ATT_HEAD_DIM ** -0.5 * LOG2E
XA_Q_SCALE = X_HEAD_DIM ** -0.5 * LOG2E

NT_DIMS = (((1,), (1,)), ((), ()))
TN_DIMS = (((0,), (0,)), ((), ()))


def _mm(a, b):
    return jnp.dot(a.astype(BF16), b.astype(BF16), preferred_element_type=F32)


def _mm_nt(a, b):
    return lax.dot_general(a.astype(BF16), b.astype(BF16), NT_DIMS, preferred_element_type=F32)


def _mm_tn(a, b):
    return lax.dot_general(a.astype(BF16), b.astype(BF16), TN_DIMS, preferred_element_type=F32)


def _rms(x, g):
    return x * lax.rsqrt(jnp.mean(x * x, axis=-1, keepdims=True) + EPS) * g


def _softplus(x):
    return jnp.maximum(x, 0.0) + jnp.log1p(jnp.exp(-jnp.abs(x)))


def _silu(x):
    return x * jax.nn.sigmoid(x)


def _iota(shape, dim):
    return lax.broadcasted_iota(jnp.int32, shape, dim)


def _params(*sem):
    return pltpu.CompilerParams(dimension_semantics=sem, vmem_limit_bytes=VMEM_LIMIT)


def _in_proj_kernel(x_ref, g_ref, w_ref, cos_ref, sa_ref, sb_ref, pc_ref,
                    aq_ref, ak_ref, avt_ref, dqkv_ref, dz_ref, grow_ref):
    subs = [slice(i * TM, (i + 1) * TM) for i in range(SUB_TILES)]
    hbs = [_rms(x_ref[r, :], g_ref[...]).astype(BF16) for r in subs]

    def rope(t, r):
        return (t * cos_ref[r, :] + pltpu.roll(t, LANES - ROPE_DIM // 2, 1) * sa_ref[r, :]
                + pltpu.roll(t, ROPE_DIM // 2, 1) * sb_ref[r, :])

    qkvs = [jnp.dot(hb, w_ref[:, :O_DQKV], preferred_element_type=F32) for hb in hbs]
    for r, hb in zip(subs, hbs):
        dqkv_ref[r, :] = jnp.dot(hb, w_ref[:, O_DQKV:O_DZ], preferred_element_type=F32)
    zgs = [jnp.dot(hb, w_ref[:, O_DZ:], preferred_element_type=F32) for hb in hbs]
    for r, qkv, zg in zip(subs, qkvs, zgs):
        for j in range(ATT_Q // LANES):
            sl = slice(j * LANES, (j + 1) * LANES)
            aq_ref[r, sl] = (rope(qkv[:, sl], r) * ATT_Q_SCALE).astype(BF16)
        ak_ref[r, :] = rope(qkv[:, O_AK:O_AV], r).astype(BF16)
        avt_ref[:, r] = qkv[:, O_AV:O_DQKV].astype(BF16).T
        dz_ref[r, :] = zg[:, :DN_W]

        raw = zg[:, DN_W:]
        lane = _iota(raw.shape, 1)
        gval = -jnp.exp(pc_ref[0:1, :]) * _softplus(raw + pc_ref[1:2, :])
        gates = jnp.where(lane < 2 * DN_HEADS, jax.nn.sigmoid(raw), gval)
        pos = _iota(raw.shape, 0) & (CHUNK - 1)
        cf = gates
        cb = gates
        s = 1
        while s < CHUNK:
            cf = cf + jnp.where(pos >= s, pltpu.roll(cf, s, 0), 0.0)
            cb = cb + jnp.where(pos < CHUNK - s, pltpu.roll(cb, TM - s, 0), 0.0)
            s *= 2
        gcol = jnp.where(lane < 2 * DN_HEADS, gates, jnp.where(lane < 3 * DN_HEADS, cf, cb))
        grow_ref[:, r] = gcol.T[:N_GATES]


def _in_proj(x2d, T, g, w, cos, sa, sb, pc):
    BT = x2d.shape[0]
    n_t = T // TS
    row = lambda i: (i, 0)
    const = lambda i: (0, 0)
    tab = lambda i: (i % n_t, 0)
    full = lambda a: pl.BlockSpec(a.shape, const)
    return pl.pallas_call(
        _in_proj_kernel,
        grid=(BT // TS,),
        in_specs=[pl.BlockSpec((TS, D_MODEL), row), full(g), full(w),
                  pl.BlockSpec((TS, LANES), tab), pl.BlockSpec((TS, LANES), tab), pl.BlockSpec((TS, LANES), tab),
                  full(pc)],
        out_specs=[pl.BlockSpec((TS, ATT_Q), row), pl.BlockSpec((TS, ATT_KV), row),
                   pl.BlockSpec((ATT_KV, TS), lambda i: (0, i)),
                   pl.BlockSpec((TS, 3 * DN_W), row), pl.BlockSpec((TS, DN_W), row),
                   pl.BlockSpec((N_GATES, TS), lambda i: (0, i))],
        out_shape=[jax.ShapeDtypeStruct((BT, ATT_Q), BF16), jax.ShapeDtypeStruct((BT, ATT_KV), BF16),
                   jax.ShapeDtypeStruct((ATT_KV, BT), BF16),
                   jax.ShapeDtypeStruct((BT, 3 * DN_W), F32), jax.ShapeDtypeStruct((BT, DN_W), F32),
                   jax.ShapeDtypeStruct((N_GATES, BT), F32)],
        compiler_params=_params("parallel"),
        name="in_proj",
    )(x2d, g, w, cos, sa, sb, pc)


def _win_attn_kernel(sink_ref, q_ref, k_ref, vt_ref, o_ref, *, T):
    W = TQ + 2 * ATT_WINDOW
    hpg = ATT_HEADS // ATT_KV_HEADS
    rel = _iota((W, TQ), 0) - _iota((W, TQ), 1)
    items = []
    for sub in range(TQ_STEP // TQ):
        t0 = pl.program_id(1) * TQ_STEP + sub * TQ
        start = pl.multiple_of(jnp.clip(t0 - ATT_WINDOW, 0, T - W), LANES)
        k = k_ref[pl.ds(start, W), :]
        vt = vt_ref[:, pl.ds(start, W)]
        k_lo = _iota(k.shape, 1) < ATT_HEAD_DIM
        v_lo = _iota(vt.shape, 0) < ATT_HEAD_DIM
        kk = jnp.concatenate([jnp.where(k_lo, k, jnp.zeros_like(k)), jnp.where(k_lo, jnp.zeros_like(k), k)], axis=0)
        vvt = jnp.concatenate([jnp.where(v_lo, vt, jnp.zeros_like(vt)), jnp.where(v_lo, jnp.zeros_like(vt), vt)],
                              axis=1)
        cap = jnp.where(jnp.abs(rel + (start - t0)) <= ATT_WINDOW, jnp.inf, -1e30)
        cap = jnp.concatenate([cap, cap], axis=1)
        rows = slice(sub * TQ, (sub + 1) * TQ)
        for pair in range(ATT_Q // LANES // 2):
            groups = (2 * pair, 2 * pair + 1)
            qp = jnp.concatenate([q_ref[rows, g * LANES:(g + 1) * LANES] for g in groups], axis=0)
            items.append(dict(kk=kk, vvt=vvt, cap=cap, qp=qp, rows=rows, groups=groups))
    for it in items:
        it["st"] = lax.dot_general(it["kk"], it["qp"], NT_DIMS, preferred_element_type=F32)
    for it in items:
        ps, dens = [], []
        for kvh in range(ATT_KV_HEADS):
            sh = jnp.minimum(it["st"][kvh * W:(kvh + 1) * W], it["cap"])
            sk = jnp.concatenate([jnp.full((1, TQ), sink_ref[kvh * hpg + g] * LOG2E, F32) for g in it["groups"]],
                                 axis=1)
            m = jnp.maximum(jnp.max(sh, axis=0, keepdims=True), sk)
            p = jnp.exp2(sh - m)
            dens.append(jnp.sum(p, axis=0, keepdims=True) + jnp.exp2(sk - m))
            ps.append(p.astype(BF16))
        it["p"] = jnp.concatenate(ps, axis=0)
        it["dens"] = dens
    for it in items:
        ot = jnp.dot(it["vvt"], it["p"], preferred_element_type=F32)
        ot = ot / jnp.where(_iota(ot.shape, 0) < ATT_HEAD_DIM, it["dens"][0], it["dens"][1])
        for i, g in enumerate(it["groups"]):
            o_ref[g * LANES:(g + 1) * LANES, it["rows"]] = ot[:, i * TQ:(i + 1) * TQ].astype(BF16)


def _win_attn(aq, ak, avt, sink, B, T):
    n_q = T // TQ_STEP
    return pl.pallas_call(
        functools.partial(_win_attn_kernel, T=T),
        grid=(B, n_q),
        in_specs=[pl.BlockSpec(memory_space=pltpu.SMEM),
                  pl.BlockSpec((TQ_STEP, ATT_Q), lambda b, i: (b * n_q + i, 0)),
                  pl.BlockSpec((T, ATT_KV), lambda b, i: (b, 0)),
                  pl.BlockSpec((ATT_KV, T), lambda b, i: (0, b))],
        out_specs=pl.BlockSpec((ATT_Q, TQ_STEP), lambda b, i: (0, b * n_q + i)),
        out_shape=jax.ShapeDtypeStruct((ATT_Q, B * T), BF16),
        compiler_params=_params("parallel", "arbitrary"),
        name="win_attn",
    )(sink, aq, ak, avt)


def _mm_pair(xb, yb):
    return jnp.concatenate([jnp.dot(xb[:, :CHUNK], yb[:, :CHUNK], preferred_element_type=F32),
                            jnp.dot(xb[:, CHUNK:], yb[:, CHUNK:], preferred_element_type=F32)], axis=1)


def _unit_tri_inverse_pairs(l2s):
    bf = lambda xs: [x.astype(BF16) for x in xs]
    r = _iota((CHUNK, 2 * CHUNK), 0)
    c = _iota((CHUNK, 2 * CHUNK), 1) & (CHUNK - 1)
    base = 16
    in_base = (r // base) == (c // base)
    ds = [jnp.where(in_base, l2, 0.0) for l2 in l2s]
    ps = [jnp.where(r == c, 1.0, 0.0) - d for d in ds]
    powb = bf(ds)
    for _ in range(3):
        powb = bf([_mm_pair(x, x) for x in powb])
        ps = [p + _mm_pair(pb, x) for p, pb, x in zip(ps, bf(ps), powb)]
    s = base
    while s < CHUNK:
        off_diag = ((r // (2 * s)) == (c // (2 * s))) & ((r // s) != (c // s))
        mbs = bf([jnp.where(off_diag, l2, 0.0) for l2 in l2s])
        pbs = bf(ps)
        tbs = bf([_mm_pair(mb, pb) for mb, pb in zip(mbs, pbs)])
        ps = [p - _mm_pair(pb, tb) for p, pb, tb in zip(ps, pbs, tbs)]
        s *= 2
    return ps


def _deltanet_kernel(q_ref, k_ref, v_ref, z_ref, grow_ref, cwq_ref, cwk_ref, cwv_ref, nw_ref,
                     o_ref, pad_scr, u_scr, w_scr, qg_scr, at_scr, gl_scr, ku_scr, kw_scr, sp_scr, *, T):
    C = CHUNK
    N = T // C
    U = PREP_UNROLL
    h = pl.program_id(1)
    ri = _iota((C, C), 0)
    ci = _iota((C, C), 1)
    incl = (ri >= ci, ri <= ci)
    strict = (ri > ci, ri < ci)
    lane = _iota((C, LANES), 1)

    zpad = jnp.zeros((PAD, DN_HEAD_DIM), F32)
    for a, ref in enumerate((q_ref, k_ref, v_ref)):
        pad_scr[a, 0:PAD, :] = zpad
        pad_scr[a, T + PAD:T + 2 * PAD, :] = zpad

        def copy_rows(i, carry, a=a, ref=ref):
            r0 = pl.multiple_of(i * COPY_ROWS, COPY_ROWS)
            pad_scr[a, pl.ds(pl.multiple_of(r0 + PAD, PAD), COPY_ROWS), :] = ref[pl.ds(r0, COPY_ROWS), :]
            return carry

        lax.fori_loop(0, T // COPY_ROWS, copy_rows, 0)

    def conv_silu(a, cw_ref, c0):
        win = pad_scr.at[a, pl.ds(c0, C + 2 * PAD), :]
        off = PAD - DN_CONV // 2
        acc = win[off:off + C, :] * cw_ref[0:1, :]
        for j in range(1, DN_CONV):
            acc = acc + win[off + j:off + j + C, :] * cw_ref[j:j + 1, :]
        return _silu(acc)

    def l2norm(t):
        return t * lax.rsqrt(jnp.sum(t * t, axis=-1, keepdims=True) + EPS)

    def chunk_start(n):
        return n * C if isinstance(n, int) else pl.multiple_of(n * C, C)

    def prepare_chunks(ns):
        c0s = [chunk_start(n) for n in ns]
        qs = [l2norm(conv_silu(0, cwq_ref, c0)) * (DN_HEAD_DIM ** -0.5) for c0 in c0s]
        ks = [l2norm(conv_silu(1, cwk_ref, c0)) for c0 in c0s]
        vs = [conv_silu(2, cwv_ref, c0) for c0 in c0s]
        kbs = [k.astype(BF16) for k in ks]
        grams = [lax.dot_general(jnp.concatenate([q.astype(BF16), kb], axis=0), kb, NT_DIMS,
                                 preferred_element_type=F32) for q, kb in zip(qs, kbs)]
        l2s, rhss, kds = [], [], []
        for u in range(U):
            rows = pl.ds(c0s[u], C)
            q, k, v = qs[u], ks[u], vs[u]
            gr = grow_ref[:, rows]
            gt = jnp.concatenate([gr, jnp.zeros((LANES - N_GATES, C), F32)], axis=0).T
            col = lambda idx: jnp.sum(jnp.where(lane == idx, gt, 0.0), axis=1, keepdims=True)
            beta = (col(h), col(DN_HEADS + h))
            gcc = (col(2 * DN_HEADS + h), col(3 * DN_HEADS + h))
            sub = _iota(gr.shape, 0)
            row = lambda idx: jnp.sum(jnp.where(sub == idx, gr, 0.0), axis=0, keepdims=True)
            gcr = (row(2 * DN_HEADS + h), row(3 * DN_HEADS + h))
            glast = (gcr[0][:, C - 1:C], gcr[1][:, 0:1])
            qk = grams[u][:C]
            kk = grams[u][C:]
            l2 = []
            for d in range(2):
                decay = jnp.where(incl[d], jnp.exp(jnp.where(incl[d], gcc[d] - gcr[d], 0.0)), 0.0)
                l2.append(jnp.where(strict[d], beta[d] * kk * decay, 0.0))
                eg = jnp.exp(gcc[d])
                qg_scr[d, rows, :] = (q * eg).astype(BF16)
                kds.append((k * jnp.exp(glast[d] - gcc[d])).astype(BF16))
                at_scr[d, rows, :] = (qk * decay).astype(BF16)
                gl_scr[d, ns[u]] = jnp.broadcast_to(jnp.exp(glast[d]), (8, LANES))
                rhss.append(jnp.concatenate([v * beta[d], (k * beta[d]) * eg], axis=1).astype(BF16))
            l2s.append(jnp.concatenate(l2, axis=1))
        ainvs = _unit_tri_inverse_pairs(l2s)
        for u in range(U):
            rows = pl.ds(c0s[u], C)
            for d in range(2):
                uw = jnp.dot(ainvs[u][:, d * C:(d + 1) * C].astype(BF16), rhss[2 * u + d],
                             preferred_element_type=F32)
                uwb = uw.astype(BF16)
                u_scr[d, rows, :] = uw[:, :DN_HEAD_DIM]
                w_scr[d, rows, :] = uwb[:, DN_HEAD_DIM:]
                kuw = lax.dot_general(kds[2 * u + d], uwb, TN_DIMS, preferred_element_type=F32)
                ku_scr[d, rows, :] = kuw[:, :DN_HEAD_DIM]
                kw_scr[d, rows, :] = kuw[:, DN_HEAD_DIM:].astype(BF16)

    def scan_step(d, n, state):
        rows = pl.ds(chunk_start(n), C)
        sb = state.astype(BF16)
        sp_scr[d, rows, :] = sb
        return (state * gl_scr[d, n][0:1, :] + ku_scr[d, rows, :]
                - jnp.dot(kw_scr[d, rows, :], sb, preferred_element_type=F32))

    def scan_steps(i0, states):
        for j in range(half):
            i = i0 + j
            states = (scan_step(0, i, states[0]), scan_step(1, N - 1 - i, states[1]))
        return states

    def output_chunks(ns, between=()):
        pending = list(between)

        def emit(count):
            for _ in range(min(count, len(pending))):
                pending.pop(0)()

        rows = [pl.ds(chunk_start(n), C) for n in ns]
        items = [(r, d) for r in rows for d in range(2)]
        emit(1)
        sbs = [sp_scr[d, r, :] for r, d in items]
        ws_qs = [jnp.dot(jnp.concatenate([w_scr[d, r, :], qg_scr[d, r, :]], axis=0), sb,
                         preferred_element_type=F32) for (r, d), sb in zip(items, sbs)]
        emit(1)
        v_news = [(u_scr[d, r, :] - x[:C]).astype(BF16) for (r, d), x in zip(items, ws_qs)]
        os_ = [x[C:] + jnp.dot(at_scr[d, r, :], vn, preferred_element_type=F32)
               for (r, d), x, vn in zip(items, ws_qs, v_news)]
        emit(len(pending))
        for u, r in enumerate(rows):
            o = os_[2 * u] + os_[2 * u + 1]
            o = o * lax.rsqrt(jnp.mean(o * o, axis=-1, keepdims=True) + EPS) * nw_ref[...]
            o_ref[r, :] = (o * _silu(z_ref[r, :])).astype(BF16)

    half = U // 2
    G = N // U
    front_back = lambda g: ([g * half + j for j in range(half)]
                            + [N - (g + 1) * half + j for j in range(half)])
    finished = lambda g: ([N // 2 + g * half + j for j in range(half)]
                          + [N // 2 - 1 - g * half - j for j in range(half)])
    s0 = jnp.zeros((DN_HEAD_DIM, DN_HEAD_DIM), F32)
    prepare_chunks(front_back(0))

    def prepare_and_scan(g, states):
        states = scan_steps((g - 1) * half, states)
        prepare_chunks(front_back(g))
        return states

    states = lax.fori_loop(1, G, prepare_and_scan, (s0, s0))
    states = scan_steps((G - 1) * half, states)
    states = scan_steps(N // 2, states)

    def scan_and_output(g, states):
        holder = [states]

        def one_step(j):
            def run():
                i = N // 2 + g * half + j
                holder[0] = (scan_step(0, i, holder[0][0]), scan_step(1, N - 1 - i, holder[0][1]))
            return run

        output_chunks(finished(g - 1), between=[one_step(j) for j in range(half)])
        return holder[0]

    lax.fori_loop(1, G, scan_and_output, states)
    output_chunks(finished(G - 1))


def _deltanet(dqkv, dz, grow, conv_w, norm_w, B, T):
    H = DN_HEADS
    N = T // CHUNK
    seq = lambda off: pl.BlockSpec((T, DN_HEAD_DIM), lambda b, h: (b, off + h))
    cw = lambda off: pl.BlockSpec((DN_CONV, DN_HEAD_DIM), lambda b, h: (0, off + h))
    return pl.pallas_call(
        functools.partial(_deltanet_kernel, T=T),
        grid=(B, H),
        in_specs=[seq(0), seq(H), seq(2 * H), seq(0),
                  pl.BlockSpec((N_GATES, T), lambda b, h: (0, b)),
                  cw(0), cw(H), cw(2 * H),
                  pl.BlockSpec((1, DN_HEAD_DIM), lambda b, h: (0, 0))],
        out_specs=seq(0),
        out_shape=jax.ShapeDtypeStruct((B * T, DN_W), BF16),
        scratch_shapes=[pltpu.VMEM((3, T + 2 * PAD, DN_HEAD_DIM), F32),
                        pltpu.VMEM((2, T, DN_HEAD_DIM), F32),
                        pltpu.VMEM((2, T, DN_HEAD_DIM), BF16),
                        pltpu.VMEM((2, T, DN_HEAD_DIM), BF16),
                        pltpu.VMEM((2, T, CHUNK), BF16),
                        pltpu.VMEM((2, N, 8, LANES), F32),
                        pltpu.VMEM((2, T, DN_HEAD_DIM), F32),
                        pltpu.VMEM((2, T, DN_HEAD_DIM), BF16),
                        pltpu.VMEM((2, T, DN_HEAD_DIM), BF16)],
        compiler_params=_params("parallel", "arbitrary"),
        name="deltanet",
    )(dqkv, dqkv, dqkv, dz, grow, conv_w, conv_w, conv_w, norm_w)


def _mix_xattn_kernel(att_t_ref, dn_ref, x_ref, mem_ref, woa_ref, wod_ref, g_mix_ref, g_xa_ref, wq_ref,
                      g_mem_ref, wkv_ref, wo_ref, g_out_ref, x2_ref, kv_scr):
    @pl.when(pl.program_id(1) == 0)
    def _():
        kv_scr[...] = _mm(_rms(mem_ref[...], g_mem_ref[...]), wkv_ref[...]).astype(BF16)

    subs = [slice(i * TM, (i + 1) * TM) for i in range(SUB_TILES)]
    mixes = [lax.dot_general(att_t_ref[:, r], woa_ref[...], TN_DIMS, preferred_element_type=F32)
             + jnp.dot(dn_ref[r, :], wod_ref[...], preferred_element_type=F32) for r in subs]
    x1s = [x_ref[r, :] + _rms(mix, g_mix_ref[...]) for r, mix in zip(subs, mixes)]
    qxs = [(_mm(_rms(x1, g_xa_ref[...]), wq_ref[...]) * XA_Q_SCALE).astype(BF16) for x1 in x1s]
    items = [(i, hd) for i in range(SUB_TILES) for hd in range(X_HEADS)]
    ksl = lambda hd: slice(hd * X_HEAD_DIM, (hd + 1) * X_HEAD_DIM)
    vsl = lambda hd: slice(X_W + hd * X_HEAD_DIM, X_W + (hd + 1) * X_HEAD_DIM)
    ss = [lax.dot_general(qxs[i][:, ksl(hd)], kv_scr[:, ksl(hd)], NT_DIMS, preferred_element_type=F32)
          for i, hd in items]
    ps = [jnp.exp2(s - jnp.max(s, axis=-1, keepdims=True)) for s in ss]
    dens = [jnp.sum(p, axis=-1, keepdims=True) for p in ps]
    os_ = [jnp.dot(p.astype(BF16), kv_scr[:, vsl(hd)], preferred_element_type=F32) / den
           for (i, hd), p, den in zip(items, ps, dens)]
    for i, r in enumerate(subs):
        o = jnp.concatenate(os_[i * X_HEADS:(i + 1) * X_HEADS], axis=1)
        x2_ref[r, :] = x1s[i] + _rms(_mm(o, wo_ref[...]), g_out_ref[...])


def _mix_xattn(att_t, dn, x2d, mem2d, woa, wod, g_mix, g_xa, wq, g_mem, wkv, wo, g_out, B, T):
    n_t = T // TS
    mem_len = mem2d.shape[0] // B
    row = lambda b, i: (b * n_t + i, 0)
    full = lambda a: pl.BlockSpec(a.shape, lambda b, i: (0, 0))
    return pl.pallas_call(
        _mix_xattn_kernel,
        grid=(B, n_t),
        in_specs=[pl.BlockSpec((ATT_Q, TS), lambda b, i: (0, b * n_t + i)), pl.BlockSpec((TS, DN_W), row),
                  pl.BlockSpec((TS, D_MODEL), row), pl.BlockSpec((mem_len, D_MODEL), lambda b, i: (b, 0)),
                  full(woa), full(wod), full(g_mix), full(g_xa), full(wq), full(g_mem), full(wkv), full(wo),
                  full(g_out)],
        out_specs=pl.BlockSpec((TS, D_MODEL), row),
        out_shape=jax.ShapeDtypeStruct((B * T, D_MODEL), F32),
        scratch_shapes=[pltpu.VMEM((mem_len, 2 * X_W), BF16)],
        compiler_params=_params("parallel", "arbitrary"),
        name="mix_xattn",
    )(att_t, dn, x2d, mem2d, woa, wod, g_mix, g_xa, wq, g_mem, wkv, wo, g_out)


def _mlp_kernel(x_ref, g1_ref, w1_ref, w2_ref, g2_ref, y_ref):
    subs = [slice(i * TM, (i + 1) * TM) for i in range(SUB_TILES)]
    hbs = [_rms(x_ref[r, :], g1_ref[...]).astype(BF16) for r in subs]
    accs = [jnp.zeros((TM, D_MODEL), F32) for _ in subs]
    for c in range(D_FF // FF_BLOCK):
        sl = slice(c * FF_BLOCK, (c + 1) * FF_BLOCK)
        hidden = [jnp.maximum(jnp.dot(hb, w1_ref[:, sl], preferred_element_type=F32), 0.0) for hb in hbs]
        accs = [acc + jnp.dot((a * a).astype(BF16), w2_ref[sl, :], preferred_element_type=F32)
                for acc, a in zip(accs, hidden)]
    for r, acc in zip(subs, accs):
        y_ref[r, :] = x_ref[r, :] + _rms(acc, g2_ref[...])


def _mlp(x2, g1, w1, w2, g2):
    BT = x2.shape[0]
    row = lambda i: (i, 0)
    full = lambda a: pl.BlockSpec(a.shape, lambda i: (0, 0), pipeline_mode=pl.Buffered(1))
    return pl.pallas_call(
        _mlp_kernel,
        grid=(BT // TS,),
        in_specs=[pl.BlockSpec((TS, D_MODEL), row), full(g1), full(w1), full(w2), full(g2)],
        out_specs=pl.BlockSpec((TS, D_MODEL), row),
        out_shape=jax.ShapeDtypeStruct((BT, D_MODEL), F32),
        compiler_params=_params("parallel"),
        name="mlp",
    )(x2, g1, w1, w2, g2)


def _rope_tables(T):
    half = ROPE_DIM // 2
    inv = ROPE_THETA ** (-(jnp.arange(half, dtype=F32) * 2.0 / ROPE_DIM))
    ang = jnp.arange(T).astype(F32)[:, None] * inv[None, :]
    tile = lambda t: jnp.broadcast_to(t[:, None, :], (T, LANES // half, half)).reshape(T, LANES)
    cos, sin = tile(jnp.cos(ang)), tile(jnp.sin(ang))
    d = jnp.arange(LANES) % ATT_HEAD_DIM
    lo, hi = (d < half)[None, :], ((d >= half) & (d < ROPE_DIM))[None, :]
    return jnp.where(lo | hi, cos, 1.0), jnp.where(lo, -sin, 0.0), jnp.where(hi, sin, 0.0)


def _gate_params(a_log_f, a_log_b, dt_f, dt_b):
    zeros = jnp.zeros((2 * DN_HEADS,), F32)
    a_vec = jnp.concatenate([zeros, a_log_f.astype(F32), a_log_b.astype(F32)])
    d_vec = jnp.concatenate([zeros, dt_f.astype(F32), dt_b.astype(F32)])
    pad = jnp.zeros((LANES - N_GATES,), F32)
    return jnp.stack([jnp.concatenate([a_vec, pad]), jnp.concatenate([d_vec, pad])])


def _layer(x, mem, rope_tables, w_in, attn_sink, conv_w, a_log_f, a_log_b, dt_f, dt_b, dn_norm_w, w_out,
           xa_wq, xa_wkv, xa_wo, mem_g, w1, w2, g_pre_mix, g_post_mix, g_pre_xa, g_post_xa,
           g_pre_mlp, g_post_mlp):
    B, T, _ = x.shape
    assert T % TS == 0 and T % TQ_STEP == 0 and T % (PREP_UNROLL * CHUNK) == 0 and T >= TQ + 2 * ATT_WINDOW
    x2d = x.reshape(B * T, D_MODEL)
    vec = lambda g: g.astype(F32).reshape(1, -1)
    hpg = ATT_HEADS // ATT_KV_HEADS

    def pair_heads(w, axis):
        shape = w.shape
        w = w.reshape(shape[:axis] + (ATT_KV_HEADS, hpg, ATT_HEAD_DIM) + shape[axis + 1:])
        return jnp.swapaxes(w, axis, axis + 1).reshape(shape)

    w_proj = jnp.pad(w_in.astype(BF16), ((0, 0), (0, LANES - N_GATES)))
    w_proj = w_proj.at[:, :ATT_Q].set(pair_heads(w_proj[:, :ATT_Q], 1))
    cos, sa, sb = rope_tables
    pc = _gate_params(a_log_f, a_log_b, dt_f, dt_b)

    aq, ak, avt, dqkv, dz, grow = _in_proj(x2d, T, vec(g_pre_mix), w_proj, cos, sa, sb, pc)
    att_t = _win_attn(aq, ak, avt, attn_sink.astype(F32), B, T)
    dn = _deltanet(dqkv, dz, grow, conv_w.astype(F32), vec(dn_norm_w), B, T)
    x2 = _mix_xattn(att_t, dn, x2d, mem.reshape(-1, D_MODEL),
                    pair_heads(w_out[:ATT_Q], 0).astype(BF16), w_out[ATT_Q:].astype(BF16),
                    vec(g_post_mix), vec(g_pre_xa), xa_wq.astype(BF16), vec(mem_g), xa_wkv.astype(BF16),
                    xa_wo.astype(BF16), vec(g_post_xa), B, T)
    y = _mlp(x2, vec(g_pre_mlp), w1.astype(BF16), w2.astype(BF16), vec(g_post_mlp))
    return y.reshape(B, T, D_MODEL)


def kernel(x_prompt, x_sample, mem_prompt, mem_sample, w_in, attn_sink, dn_conv_w, dn_A_log_f, dn_A_log_b,
           dn_dt_bias_f, dn_dt_bias_b, dn_norm_w, w_out, xa_wq, xa_wkv, xa_wo, mem_norm_g, mlp_w1, mlp_w2,
           norm_pre_mix, norm_post_mix, norm_pre_xa, norm_post_xa, norm_pre_mlp, norm_post_mlp):
    weights = (w_in, attn_sink, dn_conv_w, dn_A_log_f, dn_A_log_b, dn_dt_bias_f, dn_dt_bias_b, dn_norm_w,
               w_out, xa_wq, xa_wkv, xa_wo, mem_norm_g, mlp_w1, mlp_w2, norm_pre_mix, norm_post_mix,
               norm_pre_xa, norm_post_xa, norm_pre_mlp, norm_post_mlp)
    rope_tables = _rope_tables(max(x_prompt.shape[1], x_sample.shape[1]))
    outs = []
    for x, mem in ((x_prompt, mem_prompt), (x_sample, mem_sample)):
        for l in range(w_in.shape[0]):
            x = _layer(x, mem, rope_tables, *(w[l] for w in weights))
        outs.append(x)
    return tuple(outs)
```

```python
import functools

import jax
import jax.numpy as jnp
from jax import lax
from jax.experimental import pallas as pl
from jax.experimental.pallas import tpu as pltpu

F32 = jnp.float32
BF16 = jnp.bfloat16

D_MODEL = 1024
ATT_HEADS = 8
ATT_KV_HEADS = 2
ATT_HEAD_DIM = 64
ATT_WINDOW = 128
ROPE_THETA = 500000.0
ROPE_DIM = ATT_HEAD_DIM // 4
DN_HEADS = 4
DN_HEAD_DIM = 128
DN_CONV = 5
X_HEADS = 4
X_HEAD_DIM = 128
D_FF = 4 * D_MODEL
EPS = 1e-6

ATT_Q = ATT_HEADS * ATT_HEAD_DIM
ATT_KV = ATT_KV_HEADS * ATT_HEAD_DIM
DN_W = DN_HEADS * DN_HEAD_DIM
X_W = X_HEADS * X_HEAD_DIM
O_AK = ATT_Q
O_AV = O_AK + ATT_KV
O_DQKV = O_AV + ATT_KV
O_DZ = O_DQKV + 3 * DN_W
O_GATES = O_DZ + DN_W
N_GATES = 4 * DN_HEADS

LANES = 128
TM = 512
TQ = 128
TQ_STEP = 1024
CHUNK = 128
PREP_UNROLL = 8
PAD = 8
COPY_ROWS = 512
SUB_TILES = 2
TS = SUB_TILES * TM
FF_BLOCK = 1024
VMEM_LIMIT = 56 * 1024 * 1024

LOG2E = 1.4426950408889634
ATT_Q_SCALE = ATT_HEAD_DIM ** -0.5 * LOG2E
XA_Q_SCALE = X_HEAD_DIM ** -0.5 * LOG2E

NT_DIMS = (((1,), (1,)), ((), ()))
TN_DIMS = (((0,), (0,)), ((), ()))


def _mm(a, b):
    return jnp.dot(a.astype(BF16), b.astype(BF16), preferred_element_type=F32)


def _mm_nt(a, b):
    return lax.dot_general(a.astype(BF16), b.astype(BF16), NT_DIMS, preferred_element_type=F32)


def _mm_tn(a, b):
    return lax.dot_general(a.astype(BF16), b.astype(BF16), TN_DIMS, preferred_element_type=F32)


def _rms(x, g):
    return x * lax.rsqrt(jnp.mean(x * x, axis=-1, keepdims=True) + EPS) * g


def _softplus(x):
    return jnp.maximum(x, 0.0) + jnp.log1p(jnp.exp(-jnp.abs(x)))


def _silu(x):
    return x * jax.nn.sigmoid(x)


def _iota(shape, dim):
    return lax.broadcasted_iota(jnp.int32, shape, dim)


def _params(*sem):
    return pltpu.CompilerParams(dimension_semantics=sem, vmem_limit_bytes=VMEM_LIMIT)


def _in_proj_kernel(x_ref, g_ref, w_ref, cos_ref, sa_ref, sb_ref, pc_ref,
                    aq_ref, ak_ref, avt_ref, dqkv_ref, dz_ref, grow_ref):
    subs = [slice(i * TM, (i + 1) * TM) for i in range(SUB_TILES)]
    hbs = [_rms(x_ref[r, :], g_ref[...]).astype(BF16) for r in subs]

    def rope(t, r):
        return (t * cos_ref[r, :] + pltpu.roll(t, LANES - ROPE_DIM // 2, 1) * sa_ref[r, :]
                + pltpu.roll(t, ROPE_DIM // 2, 1) * sb_ref[r, :])

    qkvs = [jnp.dot(hb, w_ref[:, :O_DQKV], preferred_element_type=F32) for hb in hbs]
    for r, hb in zip(subs, hbs):
        dqkv_ref[r, :] = jnp.dot(hb, w_ref[:, O_DQKV:O_DZ], preferred_element_type=F32)
    zgs = [jnp.dot(hb, w_ref[:, O_DZ:], preferred_element_type=F32) for hb in hbs]
    for r, qkv, zg in zip(subs, qkvs, zgs):
        for j in range(ATT_Q // LANES):
            sl = slice(j * LANES, (j + 1) * LANES)
            aq_ref[r, sl] = (rope(qkv[:, sl], r) * ATT_Q_SCALE).astype(BF16)
        ak_ref[r, :] = rope(qkv[:, O_AK:O_AV], r).astype(BF16)
        avt_ref[:, r] = qkv[:, O_AV:O_DQKV].astype(BF16).T
        dz_ref[r, :] = zg[:, :DN_W]

        raw = zg[:, DN_W:]
        lane = _iota(raw.shape, 1)
        gval = -jnp.exp(pc_ref[0:1, :]) * _softplus(raw + pc_ref[1:2, :])
        gates = jnp.where(lane < 2 * DN_HEADS, jax.nn.sigmoid(raw), gval)
        pos = _iota(raw.shape, 0) & (CHUNK - 1)
        cf = gates
        cb = gates
        s = 1
        while s < CHUNK:
            cf = cf + jnp.where(pos >= s, pltpu.roll(cf, s, 0), 0.0)
            cb = cb + jnp.where(pos < CHUNK - s, pltpu.roll(cb, TM - s, 0), 0.0)
            s *= 2
        gcol = jnp.where(lane < 2 * DN_HEADS, gates, jnp.where(lane < 3 * DN_HEADS, cf, cb))
        grow_ref[:, r] = gcol.T[:N_GATES]


def _in_proj(x2d, T, g, w, cos, sa, sb, pc):
    BT = x2d.shape[0]
    n_t = T // TS
    row = lambda i: (i, 0)
    const = lambda i: (0, 0)
    tab = lambda i: (i % n_t, 0)
    full = lambda a: pl.BlockSpec(a.shape, const)
    return pl.pallas_call(
        _in_proj_kernel,
        grid=(BT // TS,),
        in_specs=[pl.BlockSpec((TS, D_MODEL), row), full(g), full(w),
                  pl.BlockSpec((TS, LANES), tab), pl.BlockSpec((TS, LANES), tab), pl.BlockSpec((TS, LANES), tab),
                  full(pc)],
        out_specs=[pl.BlockSpec((TS, ATT_Q), row), pl.BlockSpec((TS, ATT_KV), row),
                   pl.BlockSpec((ATT_KV, TS), lambda i: (0, i)),
                   pl.BlockSpec((TS, 3 * DN_W), row), pl.BlockSpec((TS, DN_W), row),
                   pl.BlockSpec((N_GATES, TS), lambda i: (0, i))],
        out_shape=[jax.ShapeDtypeStruct((BT, ATT_Q), BF16), jax.ShapeDtypeStruct((BT, ATT_KV), BF16),
                   jax.ShapeDtypeStruct((ATT_KV, BT), BF16),
                   jax.ShapeDtypeStruct((BT, 3 * DN_W), F32), jax.ShapeDtypeStruct((BT, DN_W), F32),
                   jax.ShapeDtypeStruct((N_GATES, BT), F32)],
        compiler_params=_params("parallel"),
        name="in_proj",
    )(x2d, g, w, cos, sa, sb, pc)


def _win_attn_kernel(sink_ref, q_ref, k_ref, vt_ref, o_ref, *, T):
    W = TQ + 2 * ATT_WINDOW
    hpg = ATT_HEADS // ATT_KV_HEADS
    rel = _iota((W, TQ), 0) - _iota((W, TQ), 1)
    items = []
    for sub in range(TQ_STEP // TQ):
        t0 = pl.program_id(1) * TQ_STEP + sub * TQ
        start = pl.multiple_of(jnp.clip(t0 - ATT_WINDOW, 0, T - W), LANES)
        k = k_ref[pl.ds(start, W), :]
        vt = vt_ref[:, pl.ds(start, W)]
        k_lo = _iota(k.shape, 1) < ATT_HEAD_DIM
        v_lo = _iota(vt.shape, 0) < ATT_HEAD_DIM
        kk = jnp.concatenate([jnp.where(k_lo, k, jnp.zeros_like(k)), jnp.where(k_lo, jnp.zeros_like(k), k)], axis=0)
        vvt = jnp.concatenate([jnp.where(v_lo, vt, jnp.zeros_like(vt)), jnp.where(v_lo, jnp.zeros_like(vt), vt)],
                              axis=1)
        ones_rows = jnp.where((_iota((8, 2 * W), 0) == 0) == (_iota((8, 2 * W), 1) < W), 1.0, 0.0)
        ones_rows = jnp.where(_iota((8, 2 * W), 0) < 2, ones_rows, 0.0).astype(BF16)
        vvt = jnp.concatenate([vvt, ones_rows], axis=0)
        cap = jnp.where(jnp.abs(rel + (start - t0)) <= ATT_WINDOW, jnp.inf, -1e30)
        cap = jnp.concatenate([cap, cap], axis=1)
        rows = slice(sub * TQ, (sub + 1) * TQ)
        for pair in range(ATT_Q // LANES // 2):
            groups = (2 * pair, 2 * pair + 1)
            qp = jnp.concatenate([q_ref[rows, g * LANES:(g + 1) * LANES] for g in groups], axis=0)
            items.append(dict(kk=kk, vvt=vvt, cap=cap, qp=qp, rows=rows, groups=groups))
    for it in items:
        it["st"] = lax.dot_general(it["kk"], it["qp"], NT_DIMS, preferred_element_type=F32)
    for it in items:
        ps, dens = [], []
        for kvh in range(ATT_KV_HEADS):
            sh = jnp.minimum(it["st"][kvh * W:(kvh + 1) * W], it["cap"])
            sk = jnp.concatenate([jnp.full((1, TQ), sink_ref[kvh * hpg + g] * LOG2E, F32) for g in it["groups"]],
                                 axis=1)
            m = jnp.maximum(jnp.max(sh, axis=0, keepdims=True), sk)
            p = jnp.exp2(sh - m)
            dens.append(jnp.exp2(sk - m))
            ps.append(p.astype(BF16))
        it["p"] = jnp.concatenate(ps, axis=0)
        it["dens"] = dens
    for it in items:
        ot = jnp.dot(it["vvt"], it["p"], preferred_element_type=F32)
        den0 = ot[LANES:LANES + 1] + it["dens"][0]
        den1 = ot[LANES + 1:LANES + 2] + it["dens"][1]
        ot = ot[:LANES]
        ot = ot / jnp.where(_iota(ot.shape, 0) < ATT_HEAD_DIM, den0, den1)
        for i, g in enumerate(it["groups"]):
            o_ref[g * LANES:(g + 1) * LANES, it["rows"]] = ot[:, i * TQ:(i + 1) * TQ].astype(BF16)


def _win_attn(aq, ak, avt, sink, B, T):
    n_q = T // TQ_STEP
    return pl.pallas_call(
        functools.partial(_win_attn_kernel, T=T),
        grid=(B, n_q),
        in_specs=[pl.BlockSpec(memory_space=pltpu.SMEM),
                  pl.BlockSpec((TQ_STEP, ATT_Q), lambda b, i: (b * n_q + i, 0)),
                  pl.BlockSpec((T, ATT_KV), lambda b, i: (b, 0)),
                  pl.BlockSpec((ATT_KV, T), lambda b, i: (0, b))],
        out_specs=pl.BlockSpec((ATT_Q, TQ_STEP), lambda b, i: (0, b * n_q + i)),
        out_shape=jax.ShapeDtypeStruct((ATT_Q, B * T), BF16),
        compiler_params=_params("parallel", "arbitrary"),
        name="win_attn",
    )(sink, aq, ak, avt)


def _mm_pair(xb, yb):
    return jnp.concatenate([jnp.dot(xb[:, :CHUNK], yb[:, :CHUNK], preferred_element_type=F32),
                            jnp.dot(xb[:, CHUNK:], yb[:, CHUNK:], preferred_element_type=F32)], axis=1)


def _unit_tri_inverse_pairs(l2s):
    bf = lambda xs: [x.astype(BF16) for x in xs]
    r = _iota((CHUNK, 2 * CHUNK), 0)
    c = _iota((CHUNK, 2 * CHUNK), 1) & (CHUNK - 1)
    base = 16
    in_base = (r // base) == (c // base)
    ds = [jnp.where(in_base, l2, 0.0) for l2 in l2s]
    ps = [jnp.where(r == c, 1.0, 0.0) - d for d in ds]
    powb = bf(ds)
    for _ in range(3):
        powb = bf([_mm_pair(x, x) for x in powb])
        ps = [p + _mm_pair(pb, x) for p, pb, x in zip(ps, bf(ps), powb)]
    s = base
    while s < CHUNK:
        off_diag = ((r // (2 * s)) == (c // (2 * s))) & ((r // s) != (c // s))
        mbs = bf([jnp.where(off_diag, l2, 0.0) for l2 in l2s])
        pbs = bf(ps)
        tbs = bf([_mm_pair(mb, pb) for mb, pb in zip(mbs, pbs)])
        ps = [p - _mm_pair(pb, tb) for p, pb, tb in zip(ps, pbs, tbs)]
        s *= 2
    return ps


def _deltanet_kernel(q_ref, k_ref, v_ref, z_ref, grow_ref, cwq_ref, cwk_ref, cwv_ref, nw_ref,
                     o_ref, pad_scr, u_scr, w_scr, qg_scr, at_scr, gl_scr, ku_scr, kw_scr, sp_scr, *, T):
    C = CHUNK
    N = T // C
    U = PREP_UNROLL
    h = pl.program_id(1)
    ri = _iota((C, C), 0)
    ci = _iota((C, C), 1)
    incl = (ri >= ci, ri <= ci)
    strict = (ri > ci, ri < ci)
    lane = _iota((C, LANES), 1)

    zpad = jnp.zeros((PAD, DN_HEAD_DIM), F32)
    for a, ref in enumerate((q_ref, k_ref, v_ref)):
        pad_scr[a, 0:PAD, :] = zpad
        pad_scr[a, T + PAD:T + 2 * PAD, :] = zpad

        def copy_rows(i, carry, a=a, ref=ref):
            r0 = pl.multiple_of(i * COPY_ROWS, COPY_ROWS)
            pad_scr[a, pl.ds(pl.multiple_of(r0 + PAD, PAD), COPY_ROWS), :] = ref[pl.ds(r0, COPY_ROWS), :]
            return carry

        lax.fori_loop(0, T // COPY_ROWS, copy_rows, 0)

    def conv_silu(a, cw_ref, c0):
        win = pad_scr.at[a, pl.ds(c0, C + 2 * PAD), :]
        off = PAD - DN_CONV // 2
        acc = win[off:off + C, :] * cw_ref[0:1, :]
        for j in range(1, DN_CONV):
            acc = acc + win[off + j:off + j + C, :] * cw_ref[j:j + 1, :]
        return _silu(acc)

    def l2norm(t):
        return t * lax.rsqrt(jnp.sum(t * t, axis=-1, keepdims=True) + EPS)

    def chunk_start(n):
        return n * C if isinstance(n, int) else pl.multiple_of(n * C, C)

    def prepare_chunks(ns):
        c0s = [chunk_start(n) for n in ns]
        qs = [l2norm(conv_silu(0, cwq_ref, c0)) * (DN_HEAD_DIM ** -0.5) for c0 in c0s]
        ks = [l2norm(conv_silu(1, cwk_ref, c0)) for c0 in c0s]
        vs = [conv_silu(2, cwv_ref, c0) for c0 in c0s]
        kbs = [k.astype(BF16) for k in ks]
        grams = [lax.dot_general(jnp.concatenate([q.astype(BF16), kb], axis=0), kb, NT_DIMS,
                                 preferred_element_type=F32) for q, kb in zip(qs, kbs)]
        l2s, rhss, kds = [], [], []
        for u in range(U):
            rows = pl.ds(c0s[u], C)
            q, k, v = qs[u], ks[u], vs[u]
            gr = grow_ref[:, rows]
            gt = jnp.concatenate([gr, jnp.zeros((LANES - N_GATES, C), F32)], axis=0).T
            col = lambda idx: jnp.sum(jnp.where(lane == idx, gt, 0.0), axis=1, keepdims=True)
            beta = (col(h), col(DN_HEADS + h))
            gcc = (col(2 * DN_HEADS + h), col(3 * DN_HEADS + h))
            sub = _iota(gr.shape, 0)
            row = lambda idx: jnp.sum(jnp.where(sub == idx, gr, 0.0), axis=0, keepdims=True)
            gcr = (row(2 * DN_HEADS + h), row(3 * DN_HEADS + h))
            glast = (gcr[0][:, C - 1:C], gcr[1][:, 0:1])
            qk = grams[u][:C]
            kk = grams[u][C:]
            l2 = []
            for d in range(2):
                decay = jnp.where(incl[d], jnp.exp(jnp.where(incl[d], gcc[d] - gcr[d], 0.0)), 0.0)
                l2.append(jnp.where(strict[d], beta[d] * kk * decay, 0.0))
                eg = jnp.exp(gcc[d])
                qg_scr[d, rows, :] = (q * eg).astype(BF16)
                kds.append((k * jnp.exp(glast[d] - gcc[d])).astype(BF16))
                at_scr[d, rows, :] = (qk * decay).astype(BF16)
                gl_scr[d, ns[u]] = jnp.broadcast_to(jnp.exp(glast[d]), (8, LANES))
                rhss.append(jnp.concatenate([v * beta[d], (k * beta[d]) * eg], axis=1).astype(BF16))
            l2s.append(jnp.concatenate(l2, axis=1))
        ainvs = _unit_tri_inverse_pairs(l2s)
        for u in range(U):
            rows = pl.ds(c0s[u], C)
            for d in range(2):
                uw = jnp.dot(ainvs[u][:, d * C:(d + 1) * C].astype(BF16), rhss[2 * u + d],
                             preferred_element_type=F32)
                uwb = uw.astype(BF16)
                u_scr[d, rows, :] = uw[:, :DN_HEAD_DIM]
                w_scr[d, rows, :] = uwb[:, DN_HEAD_DIM:]
                kuw = lax.dot_general(kds[2 * u + d], uwb, TN_DIMS, preferred_element_type=F32)
                ku_scr[d, rows, :] = kuw[:, :DN_HEAD_DIM]
                kw_scr[d, rows, :] = kuw[:, DN_HEAD_DIM:].astype(BF16)

    def scan_step(d, n, state):
        rows = pl.ds(chunk_start(n), C)
        sb = state.astype(BF16)
        sp_scr[d, rows, :] = sb
        return (state * gl_scr[d, n][0:1, :] + ku_scr[d, rows, :]
                - jnp.dot(kw_scr[d, rows, :], sb, preferred_element_type=F32))

    def scan_steps(i0, states):
        for j in range(half):
            i = i0 + j
            states = (scan_step(0, i, states[0]), scan_step(1, N - 1 - i, states[1]))
        return states

    def output_chunks(ns, between=()):
        pending = list(between)

        def emit(count):
            for _ in range(min(count, len(pending))):
                pending.pop(0)()

        rows = [pl.ds(chunk_start(n), C) for n in ns]
        items = [(r, d) for r in rows for d in range(2)]
        emit(1)
        sbs = [sp_scr[d, r, :] for r, d in items]
        ws_qs = [jnp.dot(jnp.concatenate([w_scr[d, r, :], qg_scr[d, r, :]], axis=0), sb,
                         preferred_element_type=F32) for (r, d), sb in zip(items, sbs)]
        emit(1)
        v_news = [(u_scr[d, r, :] - x[:C]).astype(BF16) for (r, d), x in zip(items, ws_qs)]
        os_ = [x[C:] + jnp.dot(at_scr[d, r, :], vn, preferred_element_type=F32)
               for (r, d), x, vn in zip(items, ws_qs, v_news)]
        emit(len(pending))
        for u, r in enumerate(rows):
            o = os_[2 * u] + os_[2 * u + 1]
            o = o * lax.rsqrt(jnp.mean(o * o, axis=-1, keepdims=True) + EPS) * nw_ref[...]
            o_ref[r, :] = (o * _silu(z_ref[r, :])).astype(BF16)

    half = U // 2
    G = N // U
    front_back = lambda g: ([g * half + j for j in range(half)]
                            + [N - (g + 1) * half + j for j in range(half)])
    finished = lambda g: ([N // 2 + g * half + j for j in range(half)]
                          + [N // 2 - 1 - g * half - j for j in range(half)])
    s0 = jnp.zeros((DN_HEAD_DIM, DN_HEAD_DIM), F32)
    prepare_chunks(front_back(0))

    def prepare_and_scan(g, states):
        states = scan_steps((g - 1) * half, states)
        prepare_chunks(front_back(g))
        return states

    states = lax.fori_loop(1, G, prepare_and_scan, (s0, s0))
    states = scan_steps((G - 1) * half, states)
    states = scan_steps(N // 2, states)

    def scan_and_output(g, states):
        holder = [states]

        def one_step(j):
            def run():
                i = N // 2 + g * half + j
                holder[0] = (scan_step(0, i, holder[0][0]), scan_step(1, N - 1 - i, holder[0][1]))
            return run

        output_chunks(finished(g - 1), between=[one_step(j) for j in range(half)])
        return holder[0]

    lax.fori_loop(1, G, scan_and_output, states)
    output_chunks(finished(G - 1))


def _deltanet(dqkv, dz, grow, conv_w, norm_w, B, T):
    H = DN_HEADS
    N = T // CHUNK
    seq = lambda off: pl.BlockSpec((T, DN_HEAD_DIM), lambda b, h: (b, off + h))
    cw = lambda off: pl.BlockSpec((DN_CONV, DN_HEAD_DIM), lambda b, h: (0, off + h))
    return pl.pallas_call(
        functools.partial(_deltanet_kernel, T=T),
        grid=(B, H),
        in_specs=[seq(0), seq(H), seq(2 * H), seq(0),
                  pl.BlockSpec((N_GATES, T), lambda b, h: (0, b)),
                  cw(0), cw(H), cw(2 * H),
                  pl.BlockSpec((1, DN_HEAD_DIM), lambda b, h: (0, 0))],
        out_specs=seq(0),
        out_shape=jax.ShapeDtypeStruct((B * T, DN_W), BF16),
        scratch_shapes=[pltpu.VMEM((3, T + 2 * PAD, DN_HEAD_DIM), F32),
                        pltpu.VMEM((2, T, DN_HEAD_DIM), F32),
                        pltpu.VMEM((2, T, DN_HEAD_DIM), BF16),
                        pltpu.VMEM((2, T, DN_HEAD_DIM), BF16),
                        pltpu.VMEM((2, T, CHUNK), BF16),
                        pltpu.VMEM((2, N, 8, LANES), F32),
                        pltpu.VMEM((2, T, DN_HEAD_DIM), F32),
                        pltpu.VMEM((2, T, DN_HEAD_DIM), BF16),
                        pltpu.VMEM((2, T, DN_HEAD_DIM), BF16)],
        compiler_params=_params("parallel", "arbitrary"),
        name="deltanet",
    )(dqkv, dqkv, dqkv, dz, grow, conv_w, conv_w, conv_w, norm_w)


def _mix_xattn_kernel(att_t_ref, dn_ref, x_ref, mem_ref, woa_ref, wod_ref, g_mix_ref, g_xa_ref, wq_ref,
                      g_mem_ref, wkv_ref, wo_ref, g_out_ref, x2_ref, kv_scr):
    @pl.when(pl.program_id(1) == 0)
    def _():
        kv_scr[...] = _mm(_rms(mem_ref[...], g_mem_ref[...]), wkv_ref[...]).astype(BF16)

    subs = [slice(i * TM, (i + 1) * TM) for i in range(SUB_TILES)]
    mixes = [lax.dot_general(att_t_ref[:, r], woa_ref[...], TN_DIMS, preferred_element_type=F32)
             + jnp.dot(dn_ref[r, :], wod_ref[...], preferred_element_type=F32) for r in subs]
    x1s = [x_ref[r, :] + _rms(mix, g_mix_ref[...]) for r, mix in zip(subs, mixes)]
    qxs = [(_mm(_rms(x1, g_xa_ref[...]), wq_ref[...]) * XA_Q_SCALE).astype(BF16) for x1 in x1s]
    items = [(i, hd) for i in range(SUB_TILES) for hd in range(X_HEADS)]
    ksl = lambda hd: slice(hd * X_HEAD_DIM, (hd + 1) * X_HEAD_DIM)
    vsl = lambda hd: slice(X_W + hd * X_HEAD_DIM, X_W + (hd + 1) * X_HEAD_DIM)
    ss = [lax.dot_general(qxs[i][:, ksl(hd)], kv_scr[:, ksl(hd)], NT_DIMS, preferred_element_type=F32)
          for i, hd in items]
    ps = [jnp.exp2(s - jnp.max(s, axis=-1, keepdims=True)) for s in ss]
    dens = [jnp.sum(p, axis=-1, keepdims=True) for p in ps]
    os_ = [jnp.dot(p.astype(BF16), kv_scr[:, vsl(hd)], preferred_element_type=F32) / den
           for (i, hd), p, den in zip(items, ps, dens)]
    for i, r in enumerate(subs):
        o = jnp.concatenate(os_[i * X_HEADS:(i + 1) * X_HEADS], axis=1)
        x2_ref[r, :] = x1s[i] + _rms(_mm(o, wo_ref[...]), g_out_ref[...])


def _mix_xattn(att_t, dn, x2d, mem2d, woa, wod, g_mix, g_xa, wq, g_mem, wkv, wo, g_out, B, T):
    n_t = T // TS
    mem_len = mem2d.shape[0] // B
    row = lambda b, i: (b * n_t + i, 0)
    full = lambda a: pl.BlockSpec(a.shape, lambda b, i: (0, 0))
    return pl.pallas_call(
        _mix_xattn_kernel,
        grid=(B, n_t),
        in_specs=[pl.BlockSpec((ATT_Q, TS), lambda b, i: (0, b * n_t + i)), pl.BlockSpec((TS, DN_W), row),
                  pl.BlockSpec((TS, D_MODEL), row), pl.BlockSpec((mem_len, D_MODEL), lambda b, i: (b, 0)),
                  full(woa), full(wod), full(g_mix), full(g_xa), full(wq), full(g_mem), full(wkv), full(wo),
                  full(g_out)],
        out_specs=pl.BlockSpec((TS, D_MODEL), row),
        out_shape=jax.ShapeDtypeStruct((B * T, D_MODEL), F32),
        scratch_shapes=[pltpu.VMEM((mem_len, 2 * X_W), BF16)],
        compiler_params=_params("parallel", "arbitrary"),
        name="mix_xattn",
    )(att_t, dn, x2d, mem2d, woa, wod, g_mix, g_xa, wq, g_mem, wkv, wo, g_out)


def _mlp_kernel(x_ref, g1_ref, w1_ref, w2_ref, g2_ref, y_ref):
    subs = [slice(i * TM, (i + 1) * TM) for i in range(SUB_TILES)]
    hbs = [_rms(x_ref[r, :], g1_ref[...]).astype(BF16) for r in subs]
    accs = [jnp.zeros((TM, D_MODEL), F32) for _ in subs]
    for c in range(D_FF // FF_BLOCK):
        sl = slice(c * FF_BLOCK, (c + 1) * FF_BLOCK)
        hidden = [jnp.maximum(jnp.dot(hb, w1_ref[:, sl], preferred_element_type=F32), 0.0) for hb in hbs]
        accs = [acc + jnp.dot((a * a).astype(BF16), w2_ref[sl, :], preferred_element_type=F32)
                for acc, a in zip(accs, hidden)]
    for r, acc in zip(subs, accs):
        y_ref[r, :] = x_ref[r, :] + _rms(acc, g2_ref[...])


def _mlp(x2, g1, w1, w2, g2):
    BT = x2.shape[0]
    row = lambda i: (i, 0)
    full = lambda a: pl.BlockSpec(a.shape, lambda i: (0, 0), pipeline_mode=pl.Buffered(1))
    return pl.pallas_call(
        _mlp_kernel,
        grid=(BT // TS,),
        in_specs=[pl.BlockSpec((TS, D_MODEL), row), full(g1), full(w1), full(w2), full(g2)],
        out_specs=pl.BlockSpec((TS, D_MODEL), row),
        out_shape=jax.ShapeDtypeStruct((BT, D_MODEL), F32),
        compiler_params=_params("parallel"),
        name="mlp",
    )(x2, g1, w1, w2, g2)


def _rope_tables(T):
    half = ROPE_DIM // 2
    inv = ROPE_THETA ** (-(jnp.arange(half, dtype=F32) * 2.0 / ROPE_DIM))
    ang = jnp.arange(T).astype(F32)[:, None] * inv[None, :]
    tile = lambda t: jnp.broadcast_to(t[:, None, :], (T, LANES // half, half)).reshape(T, LANES)
    cos, sin = tile(jnp.cos(ang)), tile(jnp.sin(ang))
    d = jnp.arange(LANES) % ATT_HEAD_DIM
    lo, hi = (d < half)[None, :], ((d >= half) & (d < ROPE_DIM))[None, :]
    return jnp.where(lo | hi, cos, 1.0), jnp.where(lo, -sin, 0.0), jnp.where(hi, sin, 0.0)


def _gate_params(a_log_f, a_log_b, dt_f, dt_b):
    zeros = jnp.zeros((2 * DN_HEADS,), F32)
    a_vec = jnp.concatenate([zeros, a_log_f.astype(F32), a_log_b.astype(F32)])
    d_vec = jnp.concatenate([zeros, dt_f.astype(F32), dt_b.astype(F32)])
    pad = jnp.zeros((LANES - N_GATES,), F32)
    return jnp.stack([jnp.concatenate([a_vec, pad]), jnp.concatenate([d_vec, pad])])


def _layer(x, mem, rope_tables, w_in, attn_sink, conv_w, a_log_f, a_log_b, dt_f, dt_b, dn_norm_w, w_out,
           xa_wq, xa_wkv, xa_wo, mem_g, w1, w2, g_pre_mix, g_post_mix, g_pre_xa, g_post_xa,
           g_pre_mlp, g_post_mlp):
    B, T, _ = x.shape
    assert T % TS == 0 and T % TQ_STEP == 0 and T % (PREP_UNROLL * CHUNK) == 0 and T >= TQ + 2 * ATT_WINDOW
    x2d = x.reshape(B * T, D_MODEL)
    vec = lambda g: g.astype(F32).reshape(1, -1)
    hpg = ATT_HEADS // ATT_KV_HEADS

    def pair_heads(w, axis):
        shape = w.shape
        w = w.reshape(shape[:axis] + (ATT_KV_HEADS, hpg, ATT_HEAD_DIM) + shape[axis + 1:])
        return jnp.swapaxes(w, axis, axis + 1).reshape(shape)

    w_proj = jnp.pad(w_in.astype(BF16), ((0, 0), (0, LANES - N_GATES)))
    w_proj = w_proj.at[:, :ATT_Q].set(pair_heads(w_proj[:, :ATT_Q], 1))
    cos, sa, sb = rope_tables
    pc = _gate_params(a_log_f, a_log_b, dt_f, dt_b)

    aq, ak, avt, dqkv, dz, grow = _in_proj(x2d, T, vec(g_pre_mix), w_proj, cos, sa, sb, pc)
    att_t = _win_attn(aq, ak, avt, attn_sink.astype(F32), B, T)
    dn = _deltanet(dqkv, dz, grow, conv_w.astype(F32), vec(dn_norm_w), B, T)
    x2 = _mix_xattn(att_t, dn, x2d, mem.reshape(-1, D_MODEL),
                    pair_heads(w_out[:ATT_Q], 0).astype(BF16), w_out[ATT_Q:].astype(BF16),
                    vec(g_post_mix), vec(g_pre_xa), xa_wq.astype(BF16), vec(mem_g), xa_wkv.astype(BF16),
                    xa_wo.astype(BF16), vec(g_post_xa), B, T)
    y = _mlp(x2, vec(g_pre_mlp), w1.astype(BF16), w2.astype(BF16), vec(g_post_mlp))
    return y.reshape(B, T, D_MODEL)


def kernel(x_prompt, x_sample, mem_prompt, mem_sample, w_in, attn_sink, dn_conv_w, dn_A_log_f, dn_A_log_b,
           dn_dt_bias_f, dn_dt_bias_b, dn_norm_w, w_out, xa_wq, xa_wkv, xa_wo, mem_norm_g, mlp_w1, mlp_w2,
           norm_pre_mix, norm_post_mix, norm_pre_xa, norm_post_xa, norm_pre_mlp, norm_post_mlp):
    weights = (w_in, attn_sink, dn_conv_w, dn_A_log_f, dn_A_log_b, dn_dt_bias_f, dn_dt_bias_b, dn_norm_w,
               w_out, xa_wq, xa_wkv, xa_wo, mem_norm_g, mlp_w1, mlp_w2, norm_pre_mix, norm_post_mix,
               norm_pre_xa, norm_post_xa, norm_pre_mlp, norm_post_mlp)
    rope_tables = _rope_tables(max(x_prompt.shape[1], x_sample.shape[1]))
    outs = []
    for x, mem in ((x_prompt, mem_prompt), (x_sample, mem_sample)):
        for l in range(w_in.shape[0]):
            x = _layer(x, mem, rope_tables, *(w[l] for w in weights))
        outs.append(x)
    return tuple(outs)
```
